```python
import jax
import jax.numpy as jnp
from jax import lax
import numpy as np

D_MODEL = 1024
BATCH = 16
SEQ = 2048
DEPTH = 1

CHUNK = 64
HEAD_DIM = 64
RWKV_WIDTH = D_MODEL // 2
RWKV_HEADS = RWKV_WIDTH // HEAD_DIM
DSA_WIDTH = D_MODEL - RWKV_WIDTH
DSA_HEADS = DSA_WIDTH // HEAD_DIM
MIX_WIDTH = RWKV_WIDTH + DSA_WIDTH
DECAY_LORA = 64
AAA_LORA = 64
GATE_LORA = 128
GN_EPS = HEAD_DIM * 1e-5
KV_LATENT = 128
IDX_HEADS = 4
IDX_DIM = 64
TOPK_MAX = 256
Q_BLOCK = 128
N_GROUPS = 4
EXPERTS_PER_GROUP = 8
N_EXPERTS = N_GROUPS * EXPERTS_PER_GROUP
D_EXPERT = 512
TOP_K_EXPERT = 2
MOE_BLOCK = 128
NORM_EPS = 1e-6
ADA_INIT = 0.5

RWKV_SPLITS = (RWKV_WIDTH, RWKV_WIDTH, RWKV_WIDTH, DECAY_LORA, AAA_LORA, GATE_LORA)
DSA_SPLITS = (DSA_WIDTH, KV_LATENT, IDX_HEADS * IDX_DIM, IDX_DIM, IDX_HEADS)
RWKV_COLS = sum(RWKV_SPLITS)
IN_COLS = RWKV_COLS + sum(DSA_SPLITS)

kernel_name = 'hybrid_rwkv7_dsa_hmoe_block'


def rms_norm(x, g):
    xf = x.astype(jnp.float32)
    y = xf * lax.rsqrt(jnp.mean(xf * xf, axis=-1, keepdims=True) + NORM_EPS)
    return (y * g.astype(jnp.float32)).astype(x.dtype)


def modulate(h, shift, scale):
    return h * (1 + scale[:, None, :]) + shift[:, None, :]


def split_cols(t, sizes):
    return jnp.split(t, np.cumsum(sizes)[:-1].tolist(), axis=-1)


def rwkv7_time_mix(p, shift_mu, w0, w_decay_up, a0, w_aaa_up, w_gate_up, k_k, k_a, r_k, gn_g, gn_b):
    B, S, _ = p.shape
    f32 = jnp.float32
    p_prev = jnp.pad(p, ((0, 0), (1, 0), (0, 0)))[:, :-1]
    p = p + shift_mu * (p_prev - p)
    r, k, v, wl, al, gl = split_cols(p, RWKV_SPLITS)
    log_w = -jax.nn.softplus(-(w0 + jnp.tanh(wl) @ w_decay_up)) - 0.5
    decay = jnp.exp(-jnp.exp(log_w.astype(f32)))
    a = jax.nn.sigmoid(a0 + al @ w_aaa_up)
    g = jax.nn.sigmoid(gl) @ w_gate_up
    heads = lambda t: t.reshape(B, S, RWKV_HEADS, HEAD_DIM).astype(f32)
    kk = heads(k * k_k)
    kk = kk / jnp.maximum(jnp.sqrt(jnp.sum(kk * kk, axis=-1, keepdims=True)), 1e-12)
    k = k * (1 + (a - 1) * k_a)
    r_h, k_h, v_h, a_h, w_h = heads(r), heads(k), heads(v), heads(a), heads(decay)

    def step(state, inp):
        r_t, w_t, k_t, v_t, kk_t, a_t = inp
        sa = jnp.einsum('bhij,bhj->bhi', state, -kk_t)
        state = (state * w_t[:, :, None, :] + sa[..., None] * (kk_t * a_t)[:, :, None, :]
                 + v_t[..., None] * k_t[:, :, None, :])
        return state, jnp.einsum('bhij,bhj->bhi', state, r_t)

    seq_first = lambda t: jnp.swapaxes(t, 0, 1)
    init = jnp.zeros((B, RWKV_HEADS, HEAD_DIM, HEAD_DIM), f32)
    _, y = lax.scan(step, init, (seq_first(r_h), seq_first(w_h), seq_first(k_h),
                                 seq_first(v_h), seq_first(kk), seq_first(a_h)))
    y = seq_first(y)
    mu = jnp.mean(y, axis=-1, keepdims=True)
    var = jnp.mean(jnp.square(y - mu), axis=-1, keepdims=True)
    y = ((y - mu) * lax.rsqrt(var + GN_EPS)).reshape(B, S, RWKV_WIDTH) * gn_g + gn_b
    bonus = (jnp.sum(r_h * k_h * r_k, axis=-1, keepdims=True) * v_h).reshape(B, S, RWKV_WIDTH)
    return ((y + bonus) * g).astype(p.dtype)


def dsa_mix(q, ckv, q_idx, k_idx, w_idx, kv_norm_g, k_idx_norm_g, w_uk, w_uv):
    B, S, _ = q.shape
    topk = min(TOPK_MAX, S // 4)
    f32 = jnp.float32
    ckv = rms_norm(ckv, kv_norm_g)
    q_lat = jnp.einsum('bshd,chd->bshc', q.reshape(B, S, DSA_HEADS, HEAD_DIM), w_uk)
    q_idx = q_idx.reshape(B, S, IDX_HEADS, IDX_DIM)
    k_idx = rms_norm(k_idx, k_idx_norm_g)
    key_chunk = jnp.arange(S) // CHUNK
    gather = jax.vmap(lambda table, idx: table[idx])

    def block(start):
        qi = lax.dynamic_slice_in_dim(q_idx, start, Q_BLOCK, axis=1)
        wi = lax.dynamic_slice_in_dim(w_idx, start, Q_BLOCK, axis=1)
        ql = lax.dynamic_slice_in_dim(q_lat, start, Q_BLOCK, axis=1)
        q_chunk = (start + jnp.arange(Q_BLOCK)) // CHUNK
        admissible = key_chunk[None, :] <= q_chunk[:, None]
        rel = jax.nn.relu(jnp.einsum('bqhd,bsd->bqhs', qi, k_idx).astype(f32) * IDX_DIM ** -0.5)
        score = jnp.einsum('bqh,bqhs->bqs', wi.astype(f32) * IDX_HEADS ** -0.5, rel)
        score = jnp.where(admissible[None], score, -jnp.inf)
        _, sel = lax.top_k(score, topk)
        valid = (sel // CHUNK) <= q_chunk[None, :, None]
        kv = gather(ckv, sel)
        logits = jnp.einsum('bqhc,bqkc->bqhk', ql, kv).astype(f32) * HEAD_DIM ** -0.5
        logits = jnp.where(valid[:, :, None, :], logits, -jnp.inf)
        prob = jax.nn.softmax(logits, axis=-1).astype(kv.dtype)
        o_lat = jnp.einsum('bqhk,bqkc->bqhc', prob, kv)
        return jnp.einsum('bqhc,chd->bqhd', o_lat, w_uv).reshape(B, Q_BLOCK, DSA_WIDTH)

    out = lax.map(block, jnp.arange(S // Q_BLOCK) * Q_BLOCK)
    return jnp.swapaxes(out, 0, 1).reshape(B, S, DSA_WIDTH)


def hierarchical_moe(h, w_group, b_group, w_expert, b_expert, e_gate, e_up, e_down):
    B, S, D = h.shape
    f32 = jnp.float32
    n_tok = B * S
    xf = h.reshape(n_tok, D)
    g_logits = (xf @ w_group).astype(f32) + b_group.astype(f32)
    g_sel = jnp.argmax(g_logits, axis=-1)
    p_group = jnp.take_along_axis(jax.nn.softmax(g_logits, axis=-1), g_sel[:, None], axis=1)
    e_logits = ((xf @ w_expert).astype(f32) + b_expert.astype(f32)).reshape(n_tok, N_GROUPS, EXPERTS_PER_GROUP)
    e_logits = jnp.take_along_axis(e_logits, g_sel[:, None, None], axis=1)[:, 0]
    top_p, top_i = lax.top_k(jax.nn.softmax(e_logits, axis=-1), TOP_K_EXPERT)
    weight = (p_group * top_p / jnp.sum(top_p, axis=-1, keepdims=True)).reshape(-1)
    expert = (g_sel[:, None] * EXPERTS_PER_GROUP + top_i).reshape(-1)
    token = jnp.repeat(jnp.arange(n_tok), TOP_K_EXPERT)
    n_assign = n_tok * TOP_K_EXPERT
    order = jnp.argsort(expert)
    s_exp, s_tok, s_w = expert[order], token[order], weight[order]
    counts = jnp.bincount(expert, length=N_EXPERTS)
    start = jnp.cumsum(counts) - counts
    padded = (counts + MOE_BLOCK - 1) // MOE_BLOCK * MOE_BLOCK
    pend = jnp.cumsum(padded)
    dest = (pend - padded)[s_exp] + jnp.arange(n_assign) - start[s_exp]
    n_blocks = -(-n_assign // MOE_BLOCK) + N_EXPERTS
    n_rows = n_blocks * MOE_BLOCK
    row_tok = jnp.zeros((n_rows,), jnp.int32).at[dest].set(s_tok)
    row_w = jnp.zeros((n_rows,), f32).at[dest].set(s_w)
    block_exp = jnp.minimum(jnp.searchsorted(pend, jnp.arange(n_blocks) * MOE_BLOCK, side='right'), N_EXPERTS - 1)

    def expert_block(args):
        tok, w, e = args
        xb = xf[tok]
        hid = jax.nn.silu(xb @ e_gate[e]) * (xb @ e_up[e])
        return (hid @ e_down[e]) * w[:, None].astype(xb.dtype)

    y = lax.map(expert_block, (row_tok.reshape(n_blocks, MOE_BLOCK), row_w.reshape(n_blocks, MOE_BLOCK), block_exp))
    out = jnp.zeros_like(xf).at[row_tok].add(y.reshape(n_rows, D))
    return out.reshape(B, S, D)


def setup_inputs(seed: int = 0) -> dict:
    key = jax.random.key(seed)
    ks = iter(jax.random.split(key, 40))
    f32 = jnp.float32
    L = DEPTH
    nrm = lambda shape, scale: jax.random.normal(next(ks), shape, f32) * scale
    gain = lambda shape: 1.0 + nrm(shape, 0.02)
    return {
        'x': nrm((BATCH, SEQ, D_MODEL), 1.0),
        'c': nrm((BATCH, D_MODEL), 1.0),
        'ada_w': nrm((L, D_MODEL, 6 * D_MODEL), ADA_INIT * D_MODEL ** -0.5),
        'ada_b': nrm((L, 6 * D_MODEL), 0.01),
        'norm1_g': gain((L, D_MODEL)),
        'w_in': nrm((L, D_MODEL, IN_COLS), D_MODEL ** -0.5),
        'shift_mu': jax.random.uniform(next(ks), (L, RWKV_COLS), f32),
        'w0': jax.random.uniform(next(ks), (L, RWKV_WIDTH), f32, minval=-6.0, maxval=0.0),
        'w_decay_up': nrm((L, DECAY_LORA, RWKV_WIDTH), 0.5 * DECAY_LORA ** -0.5),
        'a0': nrm((L, RWKV_WIDTH), 0.1),
        'w_aaa_up': nrm((L, AAA_LORA, RWKV_WIDTH), AAA_LORA ** -0.5),
        'w_gate_up': nrm((L, GATE_LORA, RWKV_WIDTH), GATE_LORA ** -0.5),
        'k_k': 0.85 + nrm((L, RWKV_WIDTH), 0.02),
        'k_a': gain((L, RWKV_WIDTH)),
        'r_k': nrm((L, RWKV_HEADS, HEAD_DIM), 0.1),
        'gn_g': gain((L, RWKV_WIDTH)),
        'gn_b': nrm((L, RWKV_WIDTH), 0.01),
        'kv_norm_g': gain((L, KV_LATENT)),
        'k_idx_norm_g': gain((L, IDX_DIM)),
        'w_uk': nrm((L, KV_LATENT, DSA_HEADS, HEAD_DIM), KV_LATENT ** -0.5),
        'w_uv': nrm((L, KV_LATENT, DSA_HEADS, HEAD_DIM), KV_LATENT ** -0.5),
        'w_out': nrm((L, MIX_WIDTH, D_MODEL), MIX_WIDTH ** -0.5),
        'norm2_g': gain((L, D_MODEL)),
        'w_group': nrm((L, D_MODEL, N_GROUPS), D_MODEL ** -0.5),
        'b_group': nrm((L, N_GROUPS), 0.01),
        'w_expert': nrm((L, D_MODEL, N_EXPERTS), D_MODEL ** -0.5),
        'b_expert': nrm((L, N_EXPERTS), 0.01),
        'e_gate': nrm((L, N_EXPERTS, D_MODEL, D_EXPERT), D_MODEL ** -0.5),
        'e_up': nrm((L, N_EXPERTS, D_MODEL, D_EXPERT), D_MODEL ** -0.5),
        'e_down': nrm((L, N_EXPERTS, D_EXPERT, D_MODEL), D_EXPERT ** -0.5),
        'final_ada_w': nrm((D_MODEL, 2 * D_MODEL), ADA_INIT * D_MODEL ** -0.5),
        'final_ada_b': nrm((2 * D_MODEL,), 0.01),
        'final_norm_g': gain((D_MODEL,)),
    }


def reference(x, c, ada_w, ada_b, norm1_g, w_in, shift_mu, w0, w_decay_up, a0, w_aaa_up, w_gate_up,
              k_k, k_a, r_k, gn_g, gn_b, kv_norm_g, k_idx_norm_g, w_uk, w_uv, w_out, norm2_g,
              w_group, b_group, w_expert, b_expert, e_gate, e_up, e_down,
              final_ada_w, final_ada_b, final_norm_g):
    c_act = jax.nn.silu(c)
    for l in range(DEPTH):
        shift1, scale1, gate1, shift2, scale2, gate2 = jnp.split(c_act @ ada_w[l] + ada_b[l], 6, axis=-1)
        h = modulate(rms_norm(x, norm1_g[l]), shift1, scale1)
        proj = h @ w_in[l]
        y_rwkv = rwkv7_time_mix(proj[..., :RWKV_COLS], shift_mu[l], w0[l], w_decay_up[l], a0[l],
                                w_aaa_up[l], w_gate_up[l], k_k[l], k_a[l], r_k[l], gn_g[l], gn_b[l])
        q, ckv, q_idx, k_idx, w_idx = split_cols(proj[..., RWKV_COLS:], DSA_SPLITS)
        y_dsa = dsa_mix(q, ckv, q_idx, k_idx, w_idx, kv_norm_g[l], k_idx_norm_g[l], w_uk[l], w_uv[l])
        mixed = jnp.concatenate([y_rwkv, y_dsa], axis=-1) @ w_out[l]
        x = x + (1 + gate1[:, None, :]) * mixed
        h = modulate(rms_norm(x, norm2_g[l]), shift2, scale2)
        x = x + (1 + gate2[:, None, :]) * hierarchical_moe(h, w_group[l], b_group[l], w_expert[l], b_expert[l],
                                                           e_gate[l], e_up[l], e_down[l])
    shift_f, scale_f = jnp.split(c_act @ final_ada_w + final_ada_b, 2, axis=-1)
    return modulate(rms_norm(x, final_norm_g), shift_f, scale_f)
```

```python
import functools

import numpy as np
import jax
import jax.numpy as jnp
from jax import lax
from jax.experimental import pallas as pl
from jax.experimental.pallas import tpu as pltpu

F32 = jnp.float32
BF16 = jnp.bfloat16
I32 = jnp.int32

HEAD_DIM = 64
RWKV_WIDTH = 512
RWKV_HEADS = RWKV_WIDTH // HEAD_DIM
GROUP_HEADS = 4
DSA_WIDTH = 512
DSA_HEADS = DSA_WIDTH // HEAD_DIM
DECAY_LORA = 64
AAA_LORA = 64
GATE_LORA = 128
KV_LATENT = 128
IDX_HEADS = 4
IDX_DIM = 64
RWKV_COLS = 3 * RWKV_WIDTH + DECAY_LORA + AAA_LORA + GATE_LORA
DSA_COLS = DSA_WIDTH + KV_LATENT + IDX_HEADS * IDX_DIM + IDX_DIM + IDX_HEADS
DSA_COLS_PAD = 1024
TOPK_MAX = 256
ATTN_CHUNK_LOG2 = 6
Q_BLOCK = 128
RWKV_CHUNK = 64
N_GROUPS = 4
EXPERTS_PER_GROUP = 8
N_EXPERTS = N_GROUPS * EXPERTS_PER_GROUP
ROUTER_ROWS = 40
NORM_EPS = 1e-6
GN_EPS = HEAD_DIM * 1e-5
NEG_BIG = -1e30
INT_MIN = -2 ** 31

VMEM_LIMIT_BYTES = 56 * 1024 * 1024
EXPERT_BLOCK = 256
TOKEN_TILE = 512

_NN = (((1,), (0,)), ((), ()))
_NT = (((1,), (1,)), ((), ()))
_TN = (((0,), (0,)), ((), ()))


def _dot(a, b, dims=_NN):
    return lax.dot_general(a, b, dims, preferred_element_type=F32)


def _split(x):
    hi = x.astype(BF16)
    lo = (x - hi.astype(F32)).astype(BF16)
    return hi, lo


def _dot3(a, b, dims=_NN):
    ah, al = _split(a)
    bh, bl = _split(b)
    return _dot(ah, bh, dims) + _dot(ah, bl, dims) + _dot(al, bh, dims)


def _dot_exact_rhs(a, b_bf16, dims=_NN):
    ah, al = _split(a)
    return _dot(ah, b_bf16, dims) + _dot(al, b_bf16, dims)


def _sigmoid(x):
    return 1.0 / (1.0 + jnp.exp(-x))


def _softplus(x):
    return jnp.maximum(x, 0.0) + jnp.log(1.0 + jnp.exp(-jnp.abs(x)))


def _params(n_axes):
    return pltpu.CompilerParams(dimension_semantics=("arbitrary",) * n_axes,
                                vmem_limit_bytes=VMEM_LIMIT_BYTES)


def _ada_kernel(c_ref, w_ref, b_ref, o_ref):
    c = c_ref[...]
    o_ref[...] = _dot3(c * _sigmoid(c), w_ref[...]) + b_ref[...]


def _ada(c, w, b):
    bsz, d = c.shape
    n = w.shape[1]
    tn = 1024
    return pl.pallas_call(
        _ada_kernel,
        grid=(n // tn,),
        in_specs=[pl.BlockSpec((bsz, d), lambda j: (0, 0)),
                  pl.BlockSpec((d, tn), lambda j: (0, j)),
                  pl.BlockSpec((1, tn), lambda j: (0, j))],
        out_specs=pl.BlockSpec((bsz, tn), lambda j: (0, j)),
        out_shape=jax.ShapeDtypeStruct((bsz, n), F32),
        compiler_params=_params(1),
        name="ada",
    )(c, w, b.reshape(1, n))


def _proj_kernel(x_ref, g_ref, sc_ref, sh_ref, w1_ref, w2_ref, o1_ref, o2_ref):
    x = x_ref[...]
    ms = jnp.mean(x * x, axis=-1, keepdims=True)
    y = x * lax.rsqrt(ms + NORM_EPS) * g_ref[...]
    h = (y * (1.0 + sc_ref[0]) + sh_ref[0]).astype(BF16)
    o1_ref[...] = _dot(h, w1_ref[...])
    o2_ref[...] = _dot(h, w2_ref[...])


def _proj(x2, g, mod, w1, w2, seq, tm):
    n, d = x2.shape
    tiles_per_seq = seq // tm
    return pl.pallas_call(
        _proj_kernel,
        grid=(n // tm,),
        in_specs=[pl.BlockSpec((tm, d), lambda i: (i, 0)),
                  pl.BlockSpec((1, d), lambda i: (0, 0)),
                  pl.BlockSpec((1, 1, d), lambda i: ((i // tiles_per_seq) * 6 + 1, 0, 0)),
                  pl.BlockSpec((1, 1, d), lambda i: ((i // tiles_per_seq) * 6 + 0, 0, 0)),
                  pl.BlockSpec(w1.shape, lambda i: (0, 0)),
                  pl.BlockSpec(w2.shape, lambda i: (0, 0))],
        out_specs=[pl.BlockSpec((tm, w1.shape[1]), lambda i: (i, 0)),
                   pl.BlockSpec((tm, w2.shape[1]), lambda i: (i, 0))],
        out_shape=[jax.ShapeDtypeStruct((n, w1.shape[1]), F32),
                   jax.ShapeDtypeStruct((n, w2.shape[1]), F32)],
        compiler_params=_params(1),
        name="proj",
    )(x2, g, mod, mod, w1, w2)


def _rwkv_kernel(p_ref, mu_ref, w0_ref, wda_ref, a0_ref, wg_ref, kk_ref, ka_ref, rk_ref,
                 gng_ref, gnb_ref, blk_ref, tril_ref, masks_ref, o_ref, s_ref, prev_ref, y_ref):
    c = RWKV_CHUNK
    w = RWKV_WIDTH

    @pl.when(pl.program_id(1) == 0)
    def _():
        s_ref[...] = jnp.zeros_like(s_ref)
        prev_ref[...] = jnp.zeros_like(prev_ref)

    p = p_ref[...]
    row = lax.broadcasted_iota(I32, p.shape, 0)
    p_prev = jnp.where(row == 0, prev_ref[...], pltpu.roll(p, 1, 0))
    prev_ref[...] = p[c - 1:c, :]
    ps = p + mu_ref[...] * (p_prev - p)

    r = ps[:, 0:w]
    k = ps[:, w:2 * w]
    v = ps[:, 2 * w:3 * w]
    lora_in = ps[:, 3 * w:3 * w + DECAY_LORA + AAA_LORA]
    gate_in = ps[:, 3 * w + DECAY_LORA + AAA_LORA:]

    lane = lax.broadcasted_iota(I32, lora_in.shape, 1)
    lora_act = jnp.where(lane < DECAY_LORA, jnp.tanh(lora_in), lora_in).astype(BF16)
    da = _dot(lora_act, wda_ref[...])
    log_w = -_softplus(-(w0_ref[...] + da[:, :w])) - 0.5
    ld = -jnp.exp(log_w)
    a = _sigmoid(a0_ref[...] + da[:, w:])
    g = _dot(_sigmoid(gate_in).astype(BF16), wg_ref[...])

    blk = blk_ref[...]
    kk = k * kk_ref[...]
    kk = kk / jnp.maximum(jnp.sqrt(_dot_exact_rhs(kk * kk, blk)), 1e-12)
    k2 = k * (1.0 + (a - 1.0) * ka_ref[...])

    cs = _dot_exact_rhs_lhs(tril_ref[...], ld)
    cs_last = cs[c - 1:c, :]
    e_neg = jnp.exp(-cs)
    e_rem = jnp.exp(cs_last - cs)
    kka = kk * a
    a_t = (-kk) * jnp.exp(cs - ld)
    b_t = kka * e_neg
    k_t = k2 * e_neg
    r_t = r * jnp.exp(cs)
    b_h = kka * e_rem
    k_h = k2 * e_rem
    w_c = jnp.exp(cs_last)

    bd = masks_ref[0]
    strict = masks_ref[1]
    incl = masks_ref[2]
    eye = masks_ref[3]
    gw = GROUP_HEADS * HEAD_DIM
    stack = lambda z: jnp.concatenate([z] * GROUP_HEADS, axis=0)

    for grp in range(RWKV_HEADS // GROUP_HEADS):
        sl = slice(grp * gw, (grp + 1) * gw)
        a_x = (stack(a_t[:, sl]) * bd).astype(BF16)
        r_x = (stack(r_t[:, sl]) * bd).astype(BF16)
        v_x = (stack(v[:, sl]) * bd).astype(BF16)
        b_r = stack(b_t[:, sl].astype(BF16))
        k_r = stack(k_t[:, sl].astype(BF16))
        l_ab = _dot(a_x, b_r, _NT) * strict
        l_ak = (_dot(a_x, k_r, _NT) * strict).astype(BF16)
        m_rb = (_dot(r_x, b_r, _NT) * incl).astype(BF16)
        m_rk = (_dot(r_x, k_r, _NT) * incl).astype(BF16)

        t_inv = eye + l_ab
        pw = l_ab
        for _ in range(5):
            pb = pw.astype(BF16)
            pw = _dot(pb, pb)
            t_inv = t_inv + _dot(t_inv.astype(BF16), pw.astype(BF16))

        s0 = s_ref[grp]
        sb = s0.astype(BF16)
        rhs = _dot(a_x, sb, _NT) + _dot(l_ak, v_x)
        ub = _dot(t_inv.astype(BF16), rhs.astype(BF16)).astype(BF16)
        y_x = _dot(r_x, sb, _NT) + _dot(m_rb, ub) + _dot(m_rk, v_x)
        y_ref[:, sl] = y_x[0:c] + y_x[c:2 * c] + y_x[2 * c:3 * c] + y_x[3 * c:4 * c]
        upd = _dot(ub, stack(b_h[:, sl].astype(BF16)), _TN) + _dot(v_x, stack(k_h[:, sl].astype(BF16)), _TN)
        s_ref[grp] = (s0 * w_c[:, sl] + upd) * bd

    y = y_ref[...]
    inv_n = 1.0 / HEAD_DIM
    mean = _dot_exact_rhs(y, blk) * inv_n
    dlt = y - mean
    var = _dot_exact_rhs(dlt * dlt, blk) * inv_n
    yn = dlt * lax.rsqrt(var + GN_EPS) * gng_ref[...] + gnb_ref[...]
    bonus = _dot_exact_rhs(r * k2 * rk_ref[...], blk) * v
    o_ref[...] = ((yn + bonus) * g).astype(o_ref.dtype)


def _dot_exact_rhs_lhs(a_bf16, b):
    bh, bl = _split(b)
    return _dot(a_bf16, bh) + _dot(a_bf16, bl)


def _rwkv(p, bsz, seq, mu, w0, wda, a0, wg, k_k, k_a, r_k, gn_g, gn_b):
    c = RWKV_CHUNK
    w = RWKV_WIDTH
    n_chunks = seq // c
    head_of = np.arange(w) // HEAD_DIM
    blk = jnp.asarray(head_of[:, None] == head_of[None, :], BF16)
    tril = jnp.asarray(np.tril(np.ones((c, c), np.float32)), BF16)
    assert c == HEAD_DIM and GROUP_HEADS == 4
    gi = np.arange(GROUP_HEADS * c)
    same = (gi[:, None] // c) == (gi[None, :] // c)
    later = (gi[:, None] % c) > (gi[None, :] % c)
    masks = jnp.asarray(np.stack([same, same & later, same & (later | (gi[:, None] % c == gi[None, :] % c)),
                                  gi[:, None] == gi[None, :]]), F32)
    row = lambda a: a.reshape(1, -1)
    vec = lambda n: pl.BlockSpec((1, n), lambda b, t: (0, 0))
    full = lambda a: pl.BlockSpec(a.shape, lambda b, t: (0, 0))
    return pl.pallas_call(
        _rwkv_kernel,
        grid=(bsz, n_chunks),
        in_specs=[pl.BlockSpec((c, RWKV_COLS), lambda b, t: (b * n_chunks + t, 0)),
                  vec(RWKV_COLS), vec(w), full(wda), vec(w), full(wg), vec(w), vec(w), vec(w),
                  vec(w), vec(w), full(blk), full(tril),
                  pl.BlockSpec(masks.shape, lambda b, t: (0, 0, 0))],
        out_specs=pl.BlockSpec((c, w), lambda b, t: (b * n_chunks + t, 0)),
        out_shape=jax.ShapeDtypeStruct((bsz * seq, w), BF16),
        scratch_shapes=[pltpu.VMEM((RWKV_HEADS // GROUP_HEADS, GROUP_HEADS * HEAD_DIM, GROUP_HEADS * HEAD_DIM), F32),
                        pltpu.VMEM((1, RWKV_COLS), F32),
                        pltpu.VMEM((c, w), F32)],
        compiler_params=_params(2),
        name="rwkv",
    )(p, row(mu), row(w0), wda, row(a0), wg, row(k_k), row(k_a), row(r_k), row(gn_g), row(gn_b),
      blk, tril, masks)


def _dsa_kernel(pd_ref, kvg_ref, kig_ref, wuk_ref, wuv_ref, triu_ref, o_ref,
                ckv_s, ckv1_s, kh_s, kl_s, score_s, *, nq, n_cls, topk):
    qb = Q_BLOCK
    j = pl.program_id(1)

    @pl.when(j == 0)
    def _():
        ckv_s[...] = jnp.zeros_like(ckv_s)
        ckv1_s[...] = jnp.zeros_like(ckv1_s)
        kh_s[...] = jnp.zeros_like(kh_s)
        kl_s[...] = jnp.zeros_like(kl_s)

    pd = pd_ref[...]
    q = pd[:, :DSA_WIDTH]
    ckv = pd[:, DSA_WIDTH:DSA_WIDTH + KV_LATENT]
    qi = pd[:, DSA_WIDTH + KV_LATENT:DSA_WIDTH + KV_LATENT + IDX_HEADS * IDX_DIM]
    tail = pd[:, DSA_WIDTH + KV_LATENT + IDX_HEADS * IDX_DIM:]
    ki = tail[:, :IDX_DIM]

    ckv_n = ckv * lax.rsqrt(jnp.mean(ckv * ckv, axis=-1, keepdims=True) + NORM_EPS) * kvg_ref[...]
    ki_n = ki * lax.rsqrt(jnp.mean(ki * ki, axis=-1, keepdims=True) + NORM_EPS) * kig_ref[...]
    ki_hi, ki_lo = _split(ki_n)
    rows = pl.ds(pl.multiple_of(j * qb, qb), qb)
    ckv_b = ckv_n.astype(BF16)
    lane = lax.broadcasted_iota(I32, ckv_b.shape, 1)
    ckv_s[rows, :] = ckv_b
    ckv1_s[rows, :] = jnp.concatenate([ckv_b, jnp.where(lane == 0, 1.0, 0.0).astype(BF16)], axis=1)
    kh_s[rows, :] = ki_hi
    kl_s[rows, :] = ki_lo

    per = nq // n_cls
    for cls in range(n_cls):
        @pl.when((j >= cls * per) & (j < (cls + 1) * per))
        def _(sk=(cls + 1) * per * qb):
            _dsa_block(j, q, qi, tail, wuk_ref, wuv_ref, triu_ref, o_ref, ckv_s, ckv1_s, kh_s, kl_s,
                       score_s, sk=sk, topk=topk)


def _dsa_block(j, q, qi, tail, wuk_ref, wuv_ref, triu_ref, o_ref, ckv_s, ckv1_s, kh_s, kl_s, score_s,
               *, sk, topk):
    qb = Q_BLOCK
    seq = sk
    kh = kh_s[0:sk, :]
    kl = kl_s[0:sk, :]
    score = jnp.zeros((qb, seq), F32)
    for hh in range(IDX_HEADS):
        q_hi, q_lo = _split(qi[:, hh * IDX_DIM:(hh + 1) * IDX_DIM])
        dots = _dot(q_hi, kh, _NT) + _dot(q_hi, kl, _NT) + _dot(q_lo, kh, _NT)
        rel = jnp.maximum(dots * (IDX_DIM ** -0.5), 0.0)
        score = score + (tail[:, IDX_DIM + hh:IDX_DIM + hh + 1] * (IDX_HEADS ** -0.5)) * rel

    key_pos = lax.broadcasted_iota(I32, (qb, seq), 1)
    q_pos = j * qb + lax.broadcasted_iota(I32, (qb, seq), 0)
    adm = jnp.right_shift(key_pos, ATTN_CHUNK_LOG2) <= jnp.right_shift(q_pos, ATTN_CHUNK_LOG2)

    score_s[:, 0:sk] = jnp.where(adm, score, NEG_BIG)

    def as_float(okey):
        return lax.bitcast_convert_type(okey ^ ((okey >> 31) & 0x7FFFFFFF), F32)

    def descend(i, tu):
        cand_u = tu | jnp.left_shift(jnp.int32(1), 31 - i)
        cand = as_float(cand_u ^ INT_MIN)
        cnt = jnp.sum(jnp.where(score_s[:, 0:sk] >= cand, 1.0, 0.0), axis=1, keepdims=True)
        return jnp.where(cnt >= topk, cand_u, tu)

    thr = as_float(lax.fori_loop(0, 32, descend, jnp.zeros((qb, 1), I32)) ^ INT_MIN)

    score = score_s[:, 0:sk]
    gt = score > thr
    eq = score == thr
    need = topk - jnp.sum(jnp.where(gt, 1.0, 0.0), axis=1, keepdims=True)
    eq_b = jnp.where(eq, 1.0, 0.0).astype(BF16)
    tb = triu_ref.shape[0]
    off = jnp.zeros((qb, 1), F32)
    pieces = []
    for kb in range(seq // tb):
        pre = _dot(eq_b[:, kb * tb:(kb + 1) * tb], triu_ref[...])
        pieces.append(pre + off)
        off = off + pre[:, tb - 1:tb]
    prefix = jnp.concatenate(pieces, axis=1)
    tie_bias = jnp.where(prefix <= need, 0.0, NEG_BIG)
    bias = jnp.where(gt, 0.0, jnp.where(eq, tie_bias, NEG_BIG))
    bias = jnp.where(adm, bias, NEG_BIG)

    ckv_all = ckv_s[0:sk, :]
    ckv_one = ckv1_s[0:sk, :]
    q_lat = (_dot(q.astype(BF16), wuk_ref[...]) * (HEAD_DIM ** -0.5)).astype(BF16)
    outs = []
    for h in range(DSA_HEADS):
        logits = _dot(q_lat[:, h * KV_LATENT:(h + 1) * KV_LATENT], ckv_all, _NT) + bias
        mx = jnp.max(logits, axis=1, keepdims=True)
        pr = jnp.exp((logits - mx).astype(BF16))
        acc = _dot(pr, ckv_one)
        outs.append((acc[:, :KV_LATENT] / acc[:, KV_LATENT:KV_LATENT + 1]).astype(BF16))
    o_lat = jnp.concatenate(outs, axis=1)
    o_ref[...] = _dot(o_lat, wuv_ref[...]).astype(o_ref.dtype)


def _dsa(pd, bsz, seq, kv_g, ki_g, w_uk, w_uv):
    qb = Q_BLOCK
    nq = seq // qb
    topk = min(TOPK_MAX, seq // 4)
    hc = DSA_HEADS * KV_LATENT
    head_of_q = np.arange(DSA_WIDTH) // HEAD_DIM
    head_of_l = np.arange(hc) // KV_LATENT
    mask = jnp.asarray(head_of_q[:, None] == head_of_l[None, :], F32)
    uk = jnp.transpose(w_uk, (1, 2, 0)).reshape(DSA_WIDTH, KV_LATENT)
    wuk_bd = (jnp.tile(uk, (1, DSA_HEADS)) * mask).astype(BF16)
    uv = w_uv.reshape(KV_LATENT, DSA_WIDTH)
    wuv_bd = (jnp.tile(uv, (DSA_HEADS, 1)) * mask.T).astype(BF16)
    n_cls = 4 if nq % 4 == 0 else 1
    tb = 256 if ((nq // n_cls) * qb) % 256 == 0 else 128
    triu = jnp.asarray(np.triu(np.ones((tb, tb), np.float32)), BF16)
    full = lambda a: pl.BlockSpec(a.shape, lambda b, j: (0, 0))
    return pl.pallas_call(
        functools.partial(_dsa_kernel, nq=nq, n_cls=n_cls, topk=topk),
        grid=(bsz, nq),
        in_specs=[pl.BlockSpec((qb, DSA_COLS_PAD), lambda b, j: (b * nq + j, 0)),
                  pl.BlockSpec((1, KV_LATENT), lambda b, j: (0, 0)),
                  pl.BlockSpec((1, IDX_DIM), lambda b, j: (0, 0)),
                  full(wuk_bd), full(wuv_bd), full(triu)],
        out_specs=pl.BlockSpec((qb, DSA_WIDTH), lambda b, j: (b * nq + j, 0)),
        out_shape=jax.ShapeDtypeStruct((bsz * seq, DSA_WIDTH), BF16),
        scratch_shapes=[pltpu.VMEM((seq, KV_LATENT), BF16),
                        pltpu.VMEM((seq, 2 * KV_LATENT), BF16),
                        pltpu.VMEM((seq, IDX_DIM), BF16),
                        pltpu.VMEM((seq, IDX_DIM), BF16),
                        pltpu.VMEM((qb, seq), F32)],
        compiler_params=_params(2),
        name="dsa",
    )(pd, kv_g.reshape(1, -1), ki_g.reshape(1, -1), wuk_bd, wuv_bd, triu)


def _mix_kernel(yr_ref, yd_ref, x_ref, wo1_ref, wo2_ref, g1_ref, n2_ref, sc_ref, sh_ref,
                wr_ref, br_ref, x1_ref, h2_ref, ids_ref, wts_ref):
    mixed = _dot(yr_ref[...], wo1_ref[...]) + _dot(yd_ref[...], wo2_ref[...])
    x1 = x_ref[...] + (1.0 + g1_ref[0]) * mixed
    x1_ref[...] = x1
    ms = jnp.mean(x1 * x1, axis=-1, keepdims=True)
    h2 = x1 * lax.rsqrt(ms + NORM_EPS) * n2_ref[...] * (1.0 + sc_ref[0]) + sh_ref[0]
    h2_ref[...] = h2

    lt = _dot3(wr_ref[...], h2, _NT) + br_ref[...]
    gl = lt[0:N_GROUPS]
    el = lt[8:8 + N_EXPERTS]
    tm = gl.shape[1]
    gmax = jnp.max(gl, axis=0, keepdims=True)
    gidx = lax.broadcasted_iota(I32, (N_GROUPS, tm), 0)
    gsel = jnp.min(jnp.where(gl == gmax, gidx, N_GROUPS), axis=0, keepdims=True)
    p_group = 1.0 / jnp.sum(jnp.exp(gl - gmax), axis=0, keepdims=True)
    eidx = lax.broadcasted_iota(I32, (N_EXPERTS, tm), 0)
    el = jnp.where(jnp.right_shift(eidx, 3) == gsel, el, NEG_BIG)
    m1 = jnp.max(el, axis=0, keepdims=True)
    i1 = jnp.min(jnp.where(el == m1, eidx, N_EXPERTS), axis=0, keepdims=True)
    el2 = jnp.where(eidx == i1, NEG_BIG, el)
    m2 = jnp.max(el2, axis=0, keepdims=True)
    i2 = jnp.min(jnp.where(el2 == m2, eidx, N_EXPERTS), axis=0, keepdims=True)
    e2 = jnp.exp(m2 - m1)
    w1 = p_group / (1.0 + e2)
    ids_ref[0] = jnp.concatenate([i1, i2], axis=0)
    wts_ref[0] = jnp.concatenate([w1, w1 * e2], axis=0)


def _mix(yr, yd, x2, wo1, wo2, mod, n2g, wr, br, seq, tm):
    n, d = x2.shape
    nt = n // tm
    tps = seq // tm
    modspec = lambda k: pl.BlockSpec((1, 1, d), lambda i: ((i // tps) * 6 + k, 0, 0))
    full = lambda a: pl.BlockSpec(a.shape, lambda i: (0,) * a.ndim)
    return pl.pallas_call(
        _mix_kernel,
        grid=(nt,),
        in_specs=[pl.BlockSpec((tm, yr.shape[1]), lambda i: (i, 0)),
                  pl.BlockSpec((tm, yd.shape[1]), lambda i: (i, 0)),
                  pl.BlockSpec((tm, d), lambda i: (i, 0)),
                  full(wo1), full(wo2), modspec(2), full(n2g), modspec(4), modspec(3),
                  full(wr), full(br)],
        out_specs=[pl.BlockSpec((tm, d), lambda i: (i, 0)),
                   pl.BlockSpec((tm, d), lambda i: (i, 0)),
                   pl.BlockSpec((1, 2, tm), lambda i: (i, 0, 0)),
                   pl.BlockSpec((1, 2, tm), lambda i: (i, 0, 0))],
        out_shape=[jax.ShapeDtypeStruct((n, d), F32),
                   jax.ShapeDtypeStruct((n, d), F32),
                   jax.ShapeDtypeStruct((nt, 2, tm), I32),
                   jax.ShapeDtypeStruct((nt, 2, tm), F32)],
        compiler_params=_params(1),
        name="mix",
    )(yr, yd, x2, wo1, wo2, mod, n2g, mod, mod, wr, br)


def _sort_kernel(ids_ref, triu_ref, tril_ref, dest_ref, bexp_ref, nused_ref,
                 cnt_ref, run_ref, start_ref, *, n_blocks_pad):
    phase = pl.program_id(0)
    i = pl.program_id(1)
    ids = ids_ref[0]
    tm = ids.shape[1]
    eidx = lax.broadcasted_iota(I32, (N_EXPERTS, tm), 0)
    hit0 = eidx == ids[0:1]
    hit1 = eidx == ids[1:2]
    onehot = jnp.where(hit0, 1.0, 0.0) + jnp.where(hit1, 1.0, 0.0)

    @pl.when((phase == 0) & (i == 0))
    def _():
        cnt_ref[...] = jnp.zeros_like(cnt_ref)

    @pl.when(phase == 0)
    def _():
        cnt_ref[...] += jnp.sum(onehot, axis=1, keepdims=True)

    @pl.when((phase == 1) & (i == 0))
    def _():
        run_ref[...] = jnp.zeros_like(run_ref)
        nblk = jnp.floor((cnt_ref[...] + (EXPERT_BLOCK - 1)) * (1.0 / EXPERT_BLOCK))
        nblk_b = jnp.broadcast_to(nblk, (N_EXPERTS, 128))
        first_blk = _dot_exact_rhs_lhs(tril_ref[...], nblk_b)
        start_ref[...] = first_blk[:, 0:1] * EXPERT_BLOCK
        end_blk = first_blk + nblk_b
        bidx = lax.broadcasted_iota(I32, (N_EXPERTS, n_blocks_pad), 1).astype(F32)
        owner = jnp.sum(jnp.where(end_blk[:, 0:1] <= bidx, 1.0, 0.0), axis=0, keepdims=True)
        bexp_ref[...] = jnp.minimum(owner, N_EXPERTS - 1).astype(I32)
        nused_ref[...] = jnp.broadcast_to(end_blk[N_EXPERTS - 1:N_EXPERTS, 0:1], (1, 128)).astype(I32)

    @pl.when(phase == 1)
    def _():
        before = _dot(onehot.astype(BF16), triu_ref[...])
        pos = start_ref[...] + run_ref[...] + before
        d0 = jnp.sum(jnp.where(hit0, pos, 0.0), axis=0, keepdims=True)
        d1 = jnp.sum(jnp.where(hit1, pos, 0.0), axis=0, keepdims=True)
        dest_ref[0] = jnp.concatenate([d0, d1], axis=0).astype(I32)
        run_ref[...] += jnp.sum(onehot, axis=1, keepdims=True)


def _sort(ids, n_blocks_pad):
    nt, _, tm = ids.shape
    triu = jnp.asarray(np.triu(np.ones((tm, tm), np.float32), 1), BF16)
    tril = jnp.asarray(np.tril(np.ones((N_EXPERTS, N_EXPERTS), np.float32), -1), BF16)
    return pl.pallas_call(
        functools.partial(_sort_kernel, n_blocks_pad=n_blocks_pad),
        grid=(2, nt),
        in_specs=[pl.BlockSpec((1, 2, tm), lambda p, i: (i, 0, 0)),
                  pl.BlockSpec(triu.shape, lambda p, i: (0, 0)),
                  pl.BlockSpec(tril.shape, lambda p, i: (0, 0))],
        out_specs=[pl.BlockSpec((1, 2, tm), lambda p, i: (i * p, 0, 0)),
                   pl.BlockSpec((1, n_blocks_pad), lambda p, i: (0, 0)),
                   pl.BlockSpec((1, 128), lambda p, i: (0, 0))],
        out_shape=[jax.ShapeDtypeStruct((nt, 2, tm), I32),
                   jax.ShapeDtypeStruct((1, n_blocks_pad), I32),
                   jax.ShapeDtypeStruct((1, 128), I32)],
        scratch_shapes=[pltpu.VMEM((N_EXPERTS, 1), F32),
                        pltpu.VMEM((N_EXPERTS, 1), F32),
                        pltpu.VMEM((N_EXPERTS, 1), F32)],
        compiler_params=_params(2),
        name="sort",
    )(ids, triu, tril)


def _row_copy(src_ref, src_row, dst_ref, dst_row, sem):
    return pltpu.make_async_copy(src_ref.at[pl.ds(src_row, 1), :], dst_ref.at[pl.ds(dst_row, 1), :], sem)


def _dispatch_kernel(dest_ref, h_ref, zeros_ref, xs_ref, sem):
    del zeros_ref
    tm = h_ref.shape[0]

    def start(r, carry):
        _row_copy(h_ref, r, xs_ref, dest_ref[0, 0, r], sem).start()
        _row_copy(h_ref, r, xs_ref, dest_ref[0, 1, r], sem).start()
        return carry

    def wait(r, carry):
        _row_copy(h_ref, 0, xs_ref, 0, sem).wait()
        _row_copy(h_ref, 0, xs_ref, 0, sem).wait()
        return carry

    lax.fori_loop(0, tm, start, 0)
    lax.fori_loop(0, tm, wait, 0)


def _dispatch(dest, h2, n_rows):
    nt, _, tm = dest.shape
    n, d = h2.shape
    zeros = jnp.zeros((n_rows, d), F32)
    return pl.pallas_call(
        _dispatch_kernel,
        grid=(nt,),
        in_specs=[pl.BlockSpec((1, 2, tm), lambda i: (i, 0, 0), memory_space=pltpu.SMEM),
                  pl.BlockSpec((tm, d), lambda i: (i, 0)),
                  pl.BlockSpec(memory_space=pl.ANY)],
        out_specs=pl.BlockSpec(memory_space=pl.ANY),
        out_shape=jax.ShapeDtypeStruct((n_rows, d), F32),
        scratch_shapes=[pltpu.SemaphoreType.DMA(())],
        input_output_aliases={2: 0},
        compiler_params=_params(1),
        name="dispatch",
    )(dest, h2, zeros)


def _expert_kernel(bexp_ref, nused_ref, xs_ref, wg_ref, wu_ref, wd_ref, ys_ref):
    del bexp_ref
    used = pl.program_id(0) < nused_ref[0]

    @pl.when(used)
    def _():
        x = xs_ref[...].astype(BF16)
        hg = _dot(x, wg_ref[0])
        hu = _dot(x, wu_ref[0])
        hid = (hg * _sigmoid(hg) * hu).astype(BF16)
        ys_ref[...] = _dot(hid, wd_ref[0])

    @pl.when(jnp.logical_not(used))
    def _():
        ys_ref[...] = jnp.zeros_like(ys_ref)


def _experts(bexp, nused, xs, e_gate, e_up, e_down, n_blocks):
    n_rows, d = xs.shape
    de = e_gate.shape[2]
    blk = EXPERT_BLOCK
    last = lambda i, nu: jnp.minimum(i, nu[0] - 1)
    rows = lambda i, be, nu: (last(i, nu), 0)
    wsel = lambda i, be, nu: (be[last(i, nu)], 0, 0)
    return pl.pallas_call(
        _expert_kernel,
        grid_spec=pltpu.PrefetchScalarGridSpec(
            num_scalar_prefetch=2,
            grid=(n_blocks,),
            in_specs=[pl.BlockSpec((blk, d), rows),
                      pl.BlockSpec((1, d, de), wsel),
                      pl.BlockSpec((1, d, de), wsel),
                      pl.BlockSpec((1, de, d), wsel)],
            out_specs=pl.BlockSpec((blk, d), lambda i, be, nu: (i, 0))),
        out_shape=jax.ShapeDtypeStruct((n_rows, d), F32),
        compiler_params=_params(1),
        name="experts",
    )(bexp, nused, xs, e_gate, e_up, e_down)


def _combine_kernel(dest_ref, ys_ref, wts_ref, x1_ref, g2_ref, gf_ref, scf_ref, shf_ref, o_ref,
                    buf0, buf1, sem):
    tm = x1_ref.shape[0]

    def start(r, carry):
        _row_copy(ys_ref, dest_ref[0, 0, r], buf0, r, sem).start()
        _row_copy(ys_ref, dest_ref[0, 1, r], buf1, r, sem).start()
        return carry

    def wait(r, carry):
        _row_copy(ys_ref, 0, buf0, 0, sem).wait()
        _row_copy(ys_ref, 0, buf1, 0, sem).wait()
        return carry

    lax.fori_loop(0, tm, start, 0)
    lax.fori_loop(0, tm, wait, 0)
    wts = wts_ref[...]
    moe = buf0[...] * wts[:, 0:1] + buf1[...] * wts[:, 1:2]
    x2 = x1_ref[...] + (1.0 + g2_ref[0]) * moe
    ms = jnp.mean(x2 * x2, axis=-1, keepdims=True)
    y = x2 * lax.rsqrt(ms + NORM_EPS) * gf_ref[...]
    o_ref[...] = y * (1.0 + scf_ref[0]) + shf_ref[0]


def _combine(dest, ys, wts_col, x1, mod, modf, gf, seq):
    nt, _, tm = dest.shape
    n, d = x1.shape
    tps = seq // tm
    return pl.pallas_call(
        _combine_kernel,
        grid=(nt,),
        in_specs=[pl.BlockSpec((1, 2, tm), lambda i: (i, 0, 0), memory_space=pltpu.SMEM),
                  pl.BlockSpec(memory_space=pl.ANY),
                  pl.BlockSpec((tm, 2), lambda i: (i, 0)),
                  pl.BlockSpec((tm, d), lambda i: (i, 0)),
                  pl.BlockSpec((1, 1, d), lambda i: ((i // tps) * 6 + 5, 0, 0)),
                  pl.BlockSpec((1, d), lambda i: (0, 0)),
                  pl.BlockSpec((1, 1, d), lambda i: ((i // tps) * 2 + 1, 0, 0)),
                  pl.BlockSpec((1, 1, d), lambda i: ((i // tps) * 2 + 0, 0, 0))],
        out_specs=pl.BlockSpec((tm, d), lambda i: (i, 0)),
        out_shape=jax.ShapeDtypeStruct((n, d), F32),
        scratch_shapes=[pltpu.VMEM((tm, d), F32), pltpu.VMEM((tm, d), F32),
                        pltpu.SemaphoreType.DMA(())],
        compiler_params=_params(1),
        name="combine",
    )(dest, ys, wts_col, x1, mod, gf, modf, modf)


def kernel(x, c, ada_w, ada_b, norm1_g, w_in, shift_mu, w0, w_decay_up, a0, w_aaa_up, w_gate_up,
           k_k, k_a, r_k, gn_g, gn_b, kv_norm_g, k_idx_norm_g, w_uk, w_uv, w_out, norm2_g,
           w_group, b_group, w_expert, b_expert, e_gate, e_up, e_down,
           final_ada_w, final_ada_b, final_norm_g):
    bsz, seq, d = x.shape
    n = bsz * seq
    depth = ada_w.shape[0]
    tm = min(TOKEN_TILE, seq)
    x2 = x.reshape(n, d)

    modf = _ada(c, final_ada_w, final_ada_b).reshape(bsz * 2, 1, d)
    for l in range(depth):
        mod = _ada(c, ada_w[l], ada_b[l]).reshape(bsz * 6, 1, d)

        w1 = w_in[l][:, :RWKV_COLS].astype(BF16)
        w2 = jnp.pad(w_in[l][:, RWKV_COLS:], ((0, 0), (0, DSA_COLS_PAD - DSA_COLS))).astype(BF16)
        p_rwkv, p_dsa = _proj(x2, norm1_g[l].reshape(1, d), mod, w1, w2, seq, tm)

        zeros = jnp.zeros((DECAY_LORA, RWKV_WIDTH), F32)
        wda = jnp.concatenate([jnp.concatenate([w_decay_up[l], zeros], axis=1),
                               jnp.concatenate([zeros, w_aaa_up[l]], axis=1)], axis=0).astype(BF16)
        y_rwkv = _rwkv(p_rwkv, bsz, seq, shift_mu[l], w0[l], wda, a0[l], w_gate_up[l].astype(BF16),
                       k_k[l], k_a[l], r_k[l], gn_g[l], gn_b[l])
        y_dsa = _dsa(p_dsa, bsz, seq, kv_norm_g[l], k_idx_norm_g[l], w_uk[l], w_uv[l])

        wr = jnp.zeros((ROUTER_ROWS, d), F32)
        wr = wr.at[0:N_GROUPS].set(w_group[l].T).at[8:8 + N_EXPERTS].set(w_expert[l].T)
        br = jnp.zeros((ROUTER_ROWS, 1), F32)
        br = br.at[0:N_GROUPS, 0].set(b_group[l]).at[8:8 + N_EXPERTS, 0].set(b_expert[l])
        wo = w_out[l].astype(BF16)
        x1, h2, ids, wts = _mix(y_rwkv, y_dsa, x2, wo[:RWKV_WIDTH], wo[RWKV_WIDTH:], mod,
                                norm2_g[l].reshape(1, d), wr, br, seq, tm)

        n_blocks = (n * 2) // EXPERT_BLOCK + N_EXPERTS
        n_blocks_pad = -(-n_blocks // 128) * 128
        dest, bexp, nused = _sort(ids, n_blocks_pad)
        xs = _dispatch(dest, h2, n_blocks * EXPERT_BLOCK)
        ys = _experts(bexp.reshape(-1), nused.reshape(-1)[:1], xs, e_gate[l].astype(BF16),
                      e_up[l].astype(BF16), e_down[l].astype(BF16), n_blocks)
        wts_col = jnp.transpose(wts, (0, 2, 1)).reshape(n, 2)
        last = l == depth - 1
        if not last:
            raise NotImplementedError("stacked layers need a residual-only combine")
        x2 = _combine(dest, ys, wts_col, x1, mod, modf, final_norm_g.reshape(1, d), seq)
    return x2.reshape(bsz, seq, d)
```

```python
import functools

import numpy as np
import jax
import jax.numpy as jnp
from jax import lax
from jax.experimental import pallas as pl
from jax.experimental.pallas import tpu as pltpu

F32 = jnp.float32
BF16 = jnp.bfloat16
I32 = jnp.int32

HEAD_DIM = 64
RWKV_WIDTH = 512
RWKV_HEADS = RWKV_WIDTH // HEAD_DIM
GROUP_HEADS = 2
DSA_WIDTH = 512
DSA_HEADS = DSA_WIDTH // HEAD_DIM
DECAY_LORA = 64
AAA_LORA = 64
GATE_LORA = 128
KV_LATENT = 128
IDX_HEADS = 4
IDX_DIM = 64
RWKV_COLS = 3 * RWKV_WIDTH + DECAY_LORA + AAA_LORA + GATE_LORA
DSA_COLS = DSA_WIDTH + KV_LATENT + IDX_HEADS * IDX_DIM + IDX_DIM + IDX_HEADS
DSA_COLS_PAD = 1024
TOPK_MAX = 256
ATTN_CHUNK_LOG2 = 6
Q_BLOCK = 256
RWKV_CHUNK = 64
N_GROUPS = 4
EXPERTS_PER_GROUP = 8
N_EXPERTS = N_GROUPS * EXPERTS_PER_GROUP
ROUTER_ROWS = 40
NORM_EPS = 1e-6
GN_EPS = HEAD_DIM * 1e-5
NEG_BIG = -1e30
INT_MIN = -2 ** 31

VMEM_LIMIT_BYTES = 56 * 1024 * 1024
EXPERT_BLOCK = 256
TOKEN_TILE = 512

_NN = (((1,), (0,)), ((), ()))
_NT = (((1,), (1,)), ((), ()))
_TN = (((0,), (0,)), ((), ()))


def _dot(a, b, dims=_NN):
    return lax.dot_general(a, b, dims, preferred_element_type=F32)


def _split(x):
    hi = x.astype(BF16)
    lo = (x - hi.astype(F32)).astype(BF16)
    return hi, lo


def _dot3(a, b, dims=_NN):
    ah, al = _split(a)
    bh, bl = _split(b)
    return _dot(ah, bh, dims) + _dot(ah, bl, dims) + _dot(al, bh, dims)


def _dot_exact_rhs(a, b_bf16, dims=_NN):
    ah, al = _split(a)
    return _dot(ah, b_bf16, dims) + _dot(al, b_bf16, dims)


def _sigmoid(x):
    return 1.0 / (1.0 + jnp.exp(-x))


def _softplus(x):
    return jnp.maximum(x, 0.0) + jnp.log(1.0 + jnp.exp(-jnp.abs(x)))


def _params(n_axes):
    return pltpu.CompilerParams(dimension_semantics=("arbitrary",) * n_axes,
                                vmem_limit_bytes=VMEM_LIMIT_BYTES)


def _ada_kernel(c_ref, w_ref, b_ref, o_ref):
    c = c_ref[...]
    o_ref[...] = _dot3(c * _sigmoid(c), w_ref[...]) + b_ref[...]


def _ada(c, w, b):
    bsz, d = c.shape
    n = w.shape[1]
    tn = 1024
    return pl.pallas_call(
        _ada_kernel,
        grid=(n // tn,),
        in_specs=[pl.BlockSpec((bsz, d), lambda j: (0, 0)),
                  pl.BlockSpec((d, tn), lambda j: (0, j)),
                  pl.BlockSpec((1, tn), lambda j: (0, j))],
        out_specs=pl.BlockSpec((bsz, tn), lambda j: (0, j)),
        out_shape=jax.ShapeDtypeStruct((bsz, n), F32),
        compiler_params=_params(1),
        name="ada",
    )(c, w, b.reshape(1, n))


def _proj_kernel(x_ref, g_ref, sc_ref, sh_ref, w1_ref, w2_ref, o1_ref, o2_ref):
    x = x_ref[...]
    ms = jnp.mean(x * x, axis=-1, keepdims=True)
    y = x * lax.rsqrt(ms + NORM_EPS) * g_ref[...]
    h = (y * (1.0 + sc_ref[0]) + sh_ref[0]).astype(BF16)
    o1_ref[...] = _dot(h, w1_ref[...])
    o2_ref[...] = _dot(h, w2_ref[...])


def _proj(x2, g, mod, w1, w2, seq, tm):
    n, d = x2.shape
    tiles_per_seq = seq // tm
    return pl.pallas_call(
        _proj_kernel,
        grid=(n // tm,),
        in_specs=[pl.BlockSpec((tm, d), lambda i: (i, 0)),
                  pl.BlockSpec((1, d), lambda i: (0, 0)),
                  pl.BlockSpec((1, 1, d), lambda i: ((i // tiles_per_seq) * 6 + 1, 0, 0)),
                  pl.BlockSpec((1, 1, d), lambda i: ((i // tiles_per_seq) * 6 + 0, 0, 0)),
                  pl.BlockSpec(w1.shape, lambda i: (0, 0)),
                  pl.BlockSpec(w2.shape, lambda i: (0, 0))],
        out_specs=[pl.BlockSpec((tm, w1.shape[1]), lambda i: (i, 0)),
                   pl.BlockSpec((tm, w2.shape[1]), lambda i: (i, 0))],
        out_shape=[jax.ShapeDtypeStruct((n, w1.shape[1]), F32),
                   jax.ShapeDtypeStruct((n, w2.shape[1]), F32)],
        compiler_params=_params(1),
        name="proj",
    )(x2, g, mod, mod, w1, w2)


def _rwkv_kernel(p_ref, mu_ref, w0_ref, wda_ref, a0_ref, wg_ref, kk_ref, ka_ref, rk_ref,
                 gng_ref, gnb_ref, blk_ref, tril_ref, masks_ref, o_ref, s_ref, prev_ref):
    c = RWKV_CHUNK
    w = RWKV_WIDTH

    @pl.when(pl.program_id(1) == 0)
    def _():
        s_ref[...] = jnp.zeros_like(s_ref)
        prev_ref[...] = jnp.zeros_like(prev_ref)

    p = p_ref[...]
    row = lax.broadcasted_iota(I32, p.shape, 0)
    p_prev = jnp.where(row == 0, prev_ref[...], pltpu.roll(p, 1, 0))
    prev_ref[...] = p[c - 1:c, :]
    ps = p + mu_ref[...] * (p_prev - p)

    r = ps[:, 0:w]
    k = ps[:, w:2 * w]
    v = ps[:, 2 * w:3 * w]
    lora_in = ps[:, 3 * w:3 * w + DECAY_LORA + AAA_LORA]
    gate_in = ps[:, 3 * w + DECAY_LORA + AAA_LORA:]

    lane = lax.broadcasted_iota(I32, lora_in.shape, 1)
    lora_act = jnp.where(lane < DECAY_LORA, jnp.tanh(lora_in), lora_in).astype(BF16)
    da = _dot(lora_act, wda_ref[...])
    log_w = -_softplus(-(w0_ref[...] + da[:, :w])) - 0.5
    ld = -jnp.exp(log_w)
    a = _sigmoid(a0_ref[...] + da[:, w:])
    gate = _dot(_sigmoid(gate_in).astype(BF16), wg_ref[...])

    blk = blk_ref[...]

    def head_sums(z):
        nb = w // blk.shape[0]
        rows = jnp.concatenate([z[:, i * 128:(i + 1) * 128] for i in range(nb)], axis=0)
        sums = _dot_exact_rhs(rows, blk)
        return jnp.concatenate([sums[i * c:(i + 1) * c] for i in range(nb)], axis=1)

    kk = k * kk_ref[...]
    kk = kk / jnp.maximum(jnp.sqrt(head_sums(kk * kk)), 1e-12)
    k2 = k * (1.0 + (a - 1.0) * ka_ref[...])

    cs = _dot_exact_rhs_lhs(tril_ref[...], ld)
    cs_last = cs[c - 1:c, :]
    e_neg = jnp.exp(-cs)
    e_rem = jnp.exp(cs_last - cs)
    kka = kk * a
    a_t = (-kk) * jnp.exp(cs - ld)
    b_t = kka * e_neg
    k_t = k2 * e_neg
    r_t = r * jnp.exp(cs)
    b_h = kka * e_rem
    k_h = k2 * e_rem
    w_c = jnp.exp(cs_last)

    bd = masks_ref[0]
    strict = masks_ref[1]
    incl = masks_ref[2]
    eye = masks_ref[3]
    gw = GROUP_HEADS * HEAD_DIM
    stack = lambda z: jnp.concatenate([z] * GROUP_HEADS, axis=0)

    groups = range(RWKV_HEADS // GROUP_HEADS)
    sls = [slice(g * gw, (g + 1) * gw) for g in groups]
    a_x = [(stack(a_t[:, sl]) * bd).astype(BF16) for sl in sls]
    r_x = [(stack(r_t[:, sl]) * bd).astype(BF16) for sl in sls]
    v_x = [(stack(v[:, sl]) * bd).astype(BF16) for sl in sls]
    b_r = [stack(b_t[:, sl].astype(BF16)) for sl in sls]
    k_r = [stack(k_t[:, sl].astype(BF16)) for sl in sls]
    l_ab = [_dot(a_x[g], b_r[g], _NT) * strict for g in groups]
    l_ak = [(_dot(a_x[g], k_r[g], _NT) * strict).astype(BF16) for g in groups]
    m_rb = [(_dot(r_x[g], b_r[g], _NT) * incl).astype(BF16) for g in groups]
    m_rk = [(_dot(r_x[g], k_r[g], _NT) * incl).astype(BF16) for g in groups]

    t_inv = [eye + l for l in l_ab]
    pw = l_ab
    for _ in range(5):
        pb = [p.astype(BF16) for p in pw]
        pw = [_dot(p, p) for p in pb]
        t_inv = [t + _dot(t.astype(BF16), p.astype(BF16)) for t, p in zip(t_inv, pw)]

    s0 = [s_ref[g] for g in groups]
    sb = [s.astype(BF16) for s in s0]
    rhs = [_dot(a_x[g], sb[g], _NT) + _dot(l_ak[g], v_x[g]) for g in groups]
    ub = [_dot(t_inv[g].astype(BF16), rhs[g].astype(BF16)).astype(BF16) for g in groups]
    y_parts = []
    for g in groups:
        y_x = _dot(r_x[g], sb[g], _NT) + _dot(m_rb[g], ub[g]) + _dot(m_rk[g], v_x[g])
        y_parts.append(sum(y_x[hh * c:(hh + 1) * c] for hh in range(1, GROUP_HEADS)) + y_x[0:c])
        upd = (_dot(ub[g], stack(b_h[:, sls[g]].astype(BF16)), _TN)
               + _dot(v_x[g], stack(k_h[:, sls[g]].astype(BF16)), _TN))
        s_ref[g] = (s0[g] * w_c[:, sls[g]] + upd) * bd

    y = jnp.concatenate(y_parts, axis=1)
    inv_n = 1.0 / HEAD_DIM
    mean = head_sums(y) * inv_n
    dlt = y - mean
    var = head_sums(dlt * dlt) * inv_n
    yn = dlt * lax.rsqrt(var + GN_EPS) * gng_ref[...] + gnb_ref[...]
    bonus = head_sums(r * k2 * rk_ref[...]) * v
    o_ref[...] = ((yn + bonus) * gate).astype(o_ref.dtype)


def _dot_exact_rhs_lhs(a_bf16, b):
    bh, bl = _split(b)
    return _dot(a_bf16, bh) + _dot(a_bf16, bl)


def _rwkv(p, bsz, seq, mu, w0, wda, a0, wg, k_k, k_a, r_k, gn_g, gn_b):
    c = RWKV_CHUNK
    w = RWKV_WIDTH
    n_chunks = seq // c
    head_of = np.arange(128) // HEAD_DIM
    blk = jnp.asarray(head_of[:, None] == head_of[None, :], BF16)
    tril = jnp.asarray(np.tril(np.ones((c, c), np.float32)), BF16)
    assert c == HEAD_DIM and RWKV_HEADS % GROUP_HEADS == 0
    gi = np.arange(GROUP_HEADS * c)
    same = (gi[:, None] // c) == (gi[None, :] // c)
    later = (gi[:, None] % c) > (gi[None, :] % c)
    masks = jnp.asarray(np.stack([same, same & later, same & (later | (gi[:, None] % c == gi[None, :] % c)),
                                  gi[:, None] == gi[None, :]]), F32)
    row = lambda a: a.reshape(1, -1)
    vec = lambda n: pl.BlockSpec((1, n), lambda b, t: (0, 0))
    full = lambda a: pl.BlockSpec(a.shape, lambda b, t: (0, 0))
    return pl.pallas_call(
        _rwkv_kernel,
        grid=(bsz, n_chunks),
        in_specs=[pl.BlockSpec((c, RWKV_COLS), lambda b, t: (b * n_chunks + t, 0)),
                  vec(RWKV_COLS), vec(w), full(wda), vec(w), full(wg), vec(w), vec(w), vec(w),
                  vec(w), vec(w), full(blk), full(tril),
                  pl.BlockSpec(masks.shape, lambda b, t: (0, 0, 0))],
        out_specs=pl.BlockSpec((c, w), lambda b, t: (b * n_chunks + t, 0)),
        out_shape=jax.ShapeDtypeStruct((bsz * seq, w), BF16),
        scratch_shapes=[pltpu.VMEM((RWKV_HEADS // GROUP_HEADS, GROUP_HEADS * HEAD_DIM, GROUP_HEADS * HEAD_DIM), F32),
                        pltpu.VMEM((1, RWKV_COLS), F32)],
        compiler_params=_params(2),
        name="rwkv",
    )(p, row(mu), row(w0), wda, row(a0), wg, row(k_k), row(k_a), row(r_k), row(gn_g), row(gn_b),
      blk, tril, masks)


SLAB_ROWS = 64
ONES_ROWS = 16


def _fold_rows(x, op):
    acc = x[0:SLAB_ROWS]
    for r0 in range(SLAB_ROWS, x.shape[0], SLAB_ROWS):
        acc = op(acc, x[r0:r0 + SLAB_ROWS])
    return acc


def _split3(x):
    hi = x.astype(BF16)
    rest = x - hi.astype(F32)
    mid = rest.astype(BF16)
    return hi, mid, (rest - mid.astype(F32)).astype(BF16)


def _dsa_kernel(pd_ref, kvg_ref, kig_ref, wuk_ref, wuv_ref, tril_ref, wsel_ref, o_ref,
                ckv_s, ckvt_s, kcat_s, score_s, *, nq, n_cls, topk):
    qb = Q_BLOCK
    j = pl.program_id(1)

    @pl.when(j == 0)
    def _():
        ckv_s[...] = jnp.zeros_like(ckv_s)
        ckvt_s[...] = jnp.zeros_like(ckvt_s)
        kcat_s[...] = jnp.zeros_like(kcat_s)

    pd = pd_ref[...]
    q = pd[:, :DSA_WIDTH]
    ckv = pd[:, DSA_WIDTH:DSA_WIDTH + KV_LATENT]
    qi = pd[:, DSA_WIDTH + KV_LATENT:DSA_WIDTH + KV_LATENT + IDX_HEADS * IDX_DIM]
    tail = pd[:, DSA_WIDTH + KV_LATENT + IDX_HEADS * IDX_DIM:]
    ki = tail[:, :IDX_DIM]

    ckv_n = ckv * lax.rsqrt(jnp.mean(ckv * ckv, axis=-1, keepdims=True) + NORM_EPS) * kvg_ref[...]
    ki_n = ki * lax.rsqrt(jnp.mean(ki * ki, axis=-1, keepdims=True) + NORM_EPS) * kig_ref[...]
    ki_hi, ki_lo = _split(ki_n)
    rows = pl.ds(pl.multiple_of(j * qb, qb), qb)
    ckv_s[rows, :] = ckv_n.astype(BF16)
    one_row = jnp.where(lax.broadcasted_iota(I32, (ONES_ROWS, qb), 0) == 0, 1.0, 0.0)
    ckvt_s[j] = jnp.concatenate([ckv_n.T, one_row], axis=0).astype(BF16)
    kcat_s[rows, :] = jnp.concatenate([ki_hi, ki_hi, ki_lo, jnp.zeros_like(ki_hi)], axis=1)

    w_t = sum(_dot(wsel_ref[...], part, _NT) for part in _split3(tail))

    per = nq // n_cls
    for cls in range(n_cls):
        @pl.when((j >= cls * per) & (j < (cls + 1) * per))
        def _(sk=(cls + 1) * per * qb):
            _dsa_block(j, q, qi, w_t, wuk_ref, wuv_ref, tril_ref, o_ref, ckv_s, ckvt_s, kcat_s,
                       score_s, sk=sk, topk=topk)


def _dsa_block(j, q, qi, w_t, wuk_ref, wuv_ref, tril_ref, o_ref, ckv_s, ckvt_s, kcat_s, score_s,
               *, sk, topk):
    qb = Q_BLOCK
    kcat = kcat_s[0:sk, :]
    score = jnp.zeros((sk, qb), F32)
    w_s = w_t * (IDX_HEADS ** -0.5)
    for hh in range(IDX_HEADS):
        q_hi, q_lo = _split(qi[:, hh * IDX_DIM:(hh + 1) * IDX_DIM] * (IDX_DIM ** -0.5))
        dots = _dot(kcat, jnp.concatenate([q_hi, q_lo, q_hi, jnp.zeros_like(q_hi)], axis=1), _NT)
        score = score + w_s[hh:hh + 1, :] * jnp.maximum(dots, 0.0)

    key_pos = lax.broadcasted_iota(I32, (sk, qb), 0)
    q_pos = j * qb + lax.broadcasted_iota(I32, (sk, qb), 1)
    adm = jnp.right_shift(key_pos, ATTN_CHUNK_LOG2) <= jnp.right_shift(q_pos, ATTN_CHUNK_LOG2)

    score_s[0:sk, :] = jnp.where(adm, score, NEG_BIG)

    def as_float(okey):
        return lax.bitcast_convert_type(okey ^ ((okey >> 31) & 0x7FFFFFFF), F32)

    def descend(i, tu):
        cand_u = tu | jnp.left_shift(jnp.int32(1), 31 - i)
        cand = as_float(cand_u ^ INT_MIN)
        acc = jnp.zeros((SLAB_ROWS, qb), F32)
        for r0 in range(0, sk, SLAB_ROWS):
            acc = acc + jnp.where(score_s[r0:r0 + SLAB_ROWS, :] >= cand, 1.0, 0.0)
        cnt = jnp.sum(acc, axis=0, keepdims=True)
        return jnp.where(cnt >= topk, cand_u, tu)

    thr = as_float(lax.fori_loop(0, 32, descend, jnp.zeros((1, qb), I32)) ^ INT_MIN)

    score = score_s[0:sk, :]
    gt = score > thr
    eq = score == thr
    need = topk - jnp.sum(_fold_rows(jnp.where(gt, 1.0, 0.0), jnp.add), axis=0, keepdims=True)
    eq_b = jnp.where(eq, 1.0, 0.0).astype(BF16)
    tb = tril_ref.shape[0]
    off = jnp.zeros((1, qb), F32)
    pieces = []
    for kb in range(sk // tb):
        pre = _dot(tril_ref[...], eq_b[kb * tb:(kb + 1) * tb, :])
        pieces.append(pre + off)
        off = off + pre[tb - 1:tb, :]
    prefix = jnp.concatenate(pieces, axis=0)
    tie_bias = jnp.where(prefix <= need, 0.0, NEG_BIG)
    bias = jnp.where(gt, 0.0, jnp.where(eq, tie_bias, NEG_BIG))
    bias = jnp.where(adm, bias, NEG_BIG)

    ckv_all = ckv_s[0:sk, :]
    ckv_t = jnp.concatenate([ckvt_s[b] for b in range(sk // qb)], axis=1)
    q_lat = (_dot(q.astype(BF16), wuk_ref[...]) * (HEAD_DIM ** -0.5)).astype(BF16)
    head_logits = lambda h: _dot(ckv_all, q_lat[:, h * KV_LATENT:(h + 1) * KV_LATENT], _NT)
    outs = []
    nxt = head_logits(0)
    for h in range(DSA_HEADS):
        logits = nxt + bias
        if h + 1 < DSA_HEADS:
            nxt = head_logits(h + 1)
        mx = jnp.max(_fold_rows(logits, jnp.maximum), axis=0, keepdims=True)
        pr = jnp.exp((logits - mx).astype(BF16))
        acc = _dot(ckv_t, pr)
        outs.append((acc[:KV_LATENT] / acc[KV_LATENT:KV_LATENT + 1]).astype(BF16))
    o_lat_t = jnp.concatenate(outs, axis=0)
    o_ref[...] = _dot(o_lat_t, wuv_ref[...], _TN).astype(o_ref.dtype)


def _dsa(pd, bsz, seq, kv_g, ki_g, w_uk, w_uv):
    qb = Q_BLOCK
    nq = seq // qb
    topk = min(TOPK_MAX, seq // 4)
    hc = DSA_HEADS * KV_LATENT
    head_of_q = np.arange(DSA_WIDTH) // HEAD_DIM
    head_of_l = np.arange(hc) // KV_LATENT
    mask = jnp.asarray(head_of_q[:, None] == head_of_l[None, :], F32)
    uk = jnp.transpose(w_uk, (1, 2, 0)).reshape(DSA_WIDTH, KV_LATENT)
    wuk_bd = (jnp.tile(uk, (1, DSA_HEADS)) * mask).astype(BF16)
    uv = w_uv.reshape(KV_LATENT, DSA_WIDTH)
    wuv_bd = (jnp.tile(uv, (DSA_HEADS, 1)) * mask.T).astype(BF16)
    n_cls = 4 if nq % 4 == 0 else 1
    tb = 256 if ((nq // n_cls) * qb) % 256 == 0 else 128
    tril = jnp.asarray(np.tril(np.ones((tb, tb), np.float32)), BF16)
    wsel = jnp.asarray(np.arange(128)[None, :] == (IDX_DIM + np.arange(8))[:, None], BF16)
    wsel = wsel * jnp.asarray(np.arange(8)[:, None] < IDX_HEADS, BF16)
    full = lambda a: pl.BlockSpec(a.shape, lambda b, j: (0, 0))
    return pl.pallas_call(
        functools.partial(_dsa_kernel, nq=nq, n_cls=n_cls, topk=topk),
        grid=(bsz, nq),
        in_specs=[pl.BlockSpec((qb, DSA_COLS_PAD), lambda b, j: (b * nq + j, 0)),
                  pl.BlockSpec((1, KV_LATENT), lambda b, j: (0, 0)),
                  pl.BlockSpec((1, IDX_DIM), lambda b, j: (0, 0)),
                  full(wuk_bd), full(wuv_bd), full(tril), full(wsel)],
        out_specs=pl.BlockSpec((qb, DSA_WIDTH), lambda b, j: (b * nq + j, 0)),
        out_shape=jax.ShapeDtypeStruct((bsz * seq, DSA_WIDTH), BF16),
        scratch_shapes=[pltpu.VMEM((seq, KV_LATENT), BF16),
                        pltpu.VMEM((nq, KV_LATENT + ONES_ROWS, qb), BF16),
                        pltpu.VMEM((seq, 4 * IDX_DIM), BF16),
                        pltpu.VMEM((seq, qb), F32)],
        compiler_params=_params(2),
        name="dsa",
    )(pd, kv_g.reshape(1, -1), ki_g.reshape(1, -1), wuk_bd, wuv_bd, tril, wsel)


def _mix_kernel(yr_ref, yd_ref, x_ref, wo1_ref, wo2_ref, g1_ref, n2_ref, sc_ref, sh_ref,
                wr_ref, br_ref, x1_ref, h2_ref, ids_ref, wts_ref):
    mixed = _dot(yr_ref[...], wo1_ref[...]) + _dot(yd_ref[...], wo2_ref[...])
    x1 = x_ref[...] + (1.0 + g1_ref[0]) * mixed
    x1_ref[...] = x1
    ms = jnp.mean(x1 * x1, axis=-1, keepdims=True)
    h2 = x1 * lax.rsqrt(ms + NORM_EPS) * n2_ref[...] * (1.0 + sc_ref[0]) + sh_ref[0]
    h2_ref[...] = h2

    lt = _dot3(wr_ref[...], h2, _NT) + br_ref[...]
    gl = lt[0:N_GROUPS]
    el = lt[8:8 + N_EXPERTS]
    tm = gl.shape[1]
    gmax = jnp.max(gl, axis=0, keepdims=True)
    gidx = lax.broadcasted_iota(I32, (N_GROUPS, tm), 0)
    gsel = jnp.min(jnp.where(gl == gmax, gidx, N_GROUPS), axis=0, keepdims=True)
    p_group = 1.0 / jnp.sum(jnp.exp(gl - gmax), axis=0, keepdims=True)
    eidx = lax.broadcasted_iota(I32, (N_EXPERTS, tm), 0)
    el = jnp.where(jnp.right_shift(eidx, 3) == gsel, el, NEG_BIG)
    m1 = jnp.max(el, axis=0, keepdims=True)
    i1 = jnp.min(jnp.where(el == m1, eidx, N_EXPERTS), axis=0, keepdims=True)
    el2 = jnp.where(eidx == i1, NEG_BIG, el)
    m2 = jnp.max(el2, axis=0, keepdims=True)
    i2 = jnp.min(jnp.where(el2 == m2, eidx, N_EXPERTS), axis=0, keepdims=True)
    e2 = jnp.exp(m2 - m1)
    w1 = p_group / (1.0 + e2)
    ids_ref[0] = jnp.concatenate([i1, i2], axis=0)
    wts_ref[0] = jnp.concatenate([w1, w1 * e2], axis=0)


def _mix(yr, yd, x2, wo1, wo2, mod, n2g, wr, br, seq, tm):
    n, d = x2.shape
    nt = n // tm
    tps = seq // tm
    modspec = lambda k: pl.BlockSpec((1, 1, d), lambda i: ((i // tps) * 6 + k, 0, 0))
    full = lambda a: pl.BlockSpec(a.shape, lambda i: (0,) * a.ndim)
    return pl.pallas_call(
        _mix_kernel,
        grid=(nt,),
        in_specs=[pl.BlockSpec((tm, yr.shape[1]), lambda i: (i, 0)),
                  pl.BlockSpec((tm, yd.shape[1]), lambda i: (i, 0)),
                  pl.BlockSpec((tm, d), lambda i: (i, 0)),
                  full(wo1), full(wo2), modspec(2), full(n2g), modspec(4), modspec(3),
                  full(wr), full(br)],
        out_specs=[pl.BlockSpec((tm, d), lambda i: (i, 0)),
                   pl.BlockSpec((tm, d), lambda i: (i, 0)),
                   pl.BlockSpec((1, 2, tm), lambda i: (i, 0, 0)),
                   pl.BlockSpec((1, 2, tm), lambda i: (i, 0, 0))],
        out_shape=[jax.ShapeDtypeStruct((n, d), F32),
                   jax.ShapeDtypeStruct((n, d), F32),
                   jax.ShapeDtypeStruct((nt, 2, tm), I32),
                   jax.ShapeDtypeStruct((nt, 2, tm), F32)],
        compiler_params=_params(1),
        name="mix",
    )(yr, yd, x2, wo1, wo2, mod, n2g, mod, mod, wr, br)


def _sort_kernel(ids_ref, triu_ref, tril_ref, dest_ref, bexp_ref, eend_ref,
                 cnt_ref, run_ref, start_ref, *, n_blocks_pad):
    phase = pl.program_id(0)
    i = pl.program_id(1)
    ids = ids_ref[0]
    tm = ids.shape[1]
    eidx = lax.broadcasted_iota(I32, (N_EXPERTS, tm), 0)
    hit0 = eidx == ids[0:1]
    hit1 = eidx == ids[1:2]
    onehot = jnp.where(hit0, 1.0, 0.0) + jnp.where(hit1, 1.0, 0.0)

    @pl.when((phase == 0) & (i == 0))
    def _():
        cnt_ref[...] = jnp.zeros_like(cnt_ref)

    @pl.when(phase == 0)
    def _():
        cnt_ref[...] += jnp.sum(onehot, axis=1, keepdims=True)

    @pl.when((phase == 1) & (i == 0))
    def _():
        run_ref[...] = jnp.zeros_like(run_ref)
        nblk = jnp.floor((cnt_ref[...] + (EXPERT_BLOCK - 1)) * (1.0 / EXPERT_BLOCK))
        nblk_b = jnp.broadcast_to(nblk, (N_EXPERTS, 128))
        first_blk = _dot_exact_rhs_lhs(tril_ref[...], nblk_b)
        start_ref[...] = first_blk[:, 0:1] * EXPERT_BLOCK
        end_blk = first_blk + nblk_b
        bidx = lax.broadcasted_iota(I32, (N_EXPERTS, n_blocks_pad), 1).astype(F32)
        owner = jnp.sum(jnp.where(end_blk[:, 0:1] <= bidx, 1.0, 0.0), axis=0, keepdims=True)
        bexp_ref[...] = jnp.minimum(owner, N_EXPERTS - 1).astype(I32)
        on_diag = (lax.broadcasted_iota(I32, (N_EXPERTS, 128), 0)
                   == lax.broadcasted_iota(I32, (N_EXPERTS, 128), 1))
        eend_ref[...] = jnp.sum(jnp.where(on_diag, end_blk, 0.0), axis=0, keepdims=True).astype(I32)

    @pl.when(phase == 1)
    def _():
        before = _dot(onehot.astype(BF16), triu_ref[...])
        pos = start_ref[...] + run_ref[...] + before
        d0 = jnp.sum(jnp.where(hit0, pos, 0.0), axis=0, keepdims=True)
        d1 = jnp.sum(jnp.where(hit1, pos, 0.0), axis=0, keepdims=True)
        dest_ref[0] = jnp.concatenate([d0, d1], axis=0).astype(I32)
        run_ref[...] += jnp.sum(onehot, axis=1, keepdims=True)


def _sort(ids, n_blocks_pad):
    nt, _, tm = ids.shape
    triu = jnp.asarray(np.triu(np.ones((tm, tm), np.float32), 1), BF16)
    tril = jnp.asarray(np.tril(np.ones((N_EXPERTS, N_EXPERTS), np.float32), -1), BF16)
    return pl.pallas_call(
        functools.partial(_sort_kernel, n_blocks_pad=n_blocks_pad),
        grid=(2, nt),
        in_specs=[pl.BlockSpec((1, 2, tm), lambda p, i: (i, 0, 0)),
                  pl.BlockSpec(triu.shape, lambda p, i: (0, 0)),
                  pl.BlockSpec(tril.shape, lambda p, i: (0, 0))],
        out_specs=[pl.BlockSpec((1, 2, tm), lambda p, i: (i * p, 0, 0)),
                   pl.BlockSpec((1, n_blocks_pad), lambda p, i: (0, 0)),
                   pl.BlockSpec((1, 128), lambda p, i: (0, 0))],
        out_shape=[jax.ShapeDtypeStruct((nt, 2, tm), I32),
                   jax.ShapeDtypeStruct((1, n_blocks_pad), I32),
                   jax.ShapeDtypeStruct((1, 128), I32)],
        scratch_shapes=[pltpu.VMEM((N_EXPERTS, 1), F32),
                        pltpu.VMEM((N_EXPERTS, 1), F32),
                        pltpu.VMEM((N_EXPERTS, 1), F32)],
        compiler_params=_params(2),
        name="sort",
    )(ids, triu, tril)


def _row_copy(src_ref, src_row, dst_ref, dst_row, sem):
    return pltpu.make_async_copy(src_ref.at[pl.ds(src_row, 1), :], dst_ref.at[pl.ds(dst_row, 1), :], sem)


ROW_UNROLL = 8


def _dispatch_kernel(dest_ref, eend_ref, h_ref, xs_ref, zbuf, sem, zsem):
    tm = h_ref.shape[0]

    @pl.when(pl.program_id(0) == 0)
    def _():
        zbuf[...] = jnp.zeros_like(zbuf)

        def last_block_copy(e):
            end_b = eend_ref[0, e]
            begin_b = jnp.where(e == 0, 0, eend_ref[0, jnp.maximum(e - 1, 0)])
            row0 = pl.multiple_of((end_b - 1) * EXPERT_BLOCK, EXPERT_BLOCK)
            return end_b > begin_b, pltpu.make_async_copy(zbuf, xs_ref.at[pl.ds(row0, EXPERT_BLOCK), :], zsem)

        def fill(e, carry):
            owns_rows, cp = last_block_copy(e)
            pl.when(owns_rows)(cp.start)
            return carry

        def drain(e, carry):
            owns_rows, cp = last_block_copy(e)
            pl.when(owns_rows)(cp.wait)
            return carry

        lax.fori_loop(0, N_EXPERTS, fill, 0)
        lax.fori_loop(0, N_EXPERTS, drain, 0)

        def spare_block_copy(b):
            row0 = pl.multiple_of(b * EXPERT_BLOCK, EXPERT_BLOCK)
            return pltpu.make_async_copy(zbuf, xs_ref.at[pl.ds(row0, EXPERT_BLOCK), :], zsem)

        n_used = eend_ref[0, N_EXPERTS - 1]
        n_blocks = xs_ref.shape[0] // EXPERT_BLOCK
        lax.fori_loop(n_used, n_blocks, lambda b, carry: (spare_block_copy(b).start(), carry)[1], 0)
        lax.fori_loop(n_used, n_blocks, lambda b, carry: (spare_block_copy(b).wait(), carry)[1], 0)

    def start(g, carry):
        for u in range(ROW_UNROLL):
            r = g * ROW_UNROLL + u
            _row_copy(h_ref, r, xs_ref, dest_ref[0, 0, r], sem).start(priority=u % 2)
            _row_copy(h_ref, r, xs_ref, dest_ref[0, 1, r], sem).start(priority=(u + 1) % 2)
        return carry

    lax.fori_loop(0, tm // ROW_UNROLL, start, 0)
    all_rows = pltpu.make_async_copy(h_ref, xs_ref.at[pl.ds(0, tm), :], sem)
    all_rows.wait()
    all_rows.wait()


def _dispatch(dest, eend, h2, n_rows):
    nt, _, tm = dest.shape
    n, d = h2.shape
    return pl.pallas_call(
        _dispatch_kernel,
        grid=(nt,),
        in_specs=[pl.BlockSpec((1, 2, tm), lambda i: (i, 0, 0), memory_space=pltpu.SMEM),
                  pl.BlockSpec(eend.shape, lambda i: (0, 0), memory_space=pltpu.SMEM),
                  pl.BlockSpec((tm, d), lambda i: (i, 0))],
        out_specs=pl.BlockSpec(memory_space=pl.ANY),
        out_shape=jax.ShapeDtypeStruct((n_rows, d), F32),
        scratch_shapes=[pltpu.VMEM((EXPERT_BLOCK, d), F32),
                        pltpu.SemaphoreType.DMA(()), pltpu.SemaphoreType.DMA(())],
        compiler_params=_params(1),
        name="dispatch",
    )(dest, eend, h2)


def _expert_kernel(bexp_ref, eend_ref, xs_ref, wg_ref, wu_ref, wd_ref, ys_ref, wg_b, wu_b, wd_b):
    i = pl.program_id(0)
    used = i < eend_ref[N_EXPERTS - 1]
    new_expert = jnp.logical_or(i == 0, bexp_ref[i] != bexp_ref[jnp.maximum(i - 1, 0)])

    @pl.when(jnp.logical_and(used, new_expert))
    def _():
        wg_b[...] = wg_ref[0].astype(BF16)
        wu_b[...] = wu_ref[0].astype(BF16)
        wd_b[...] = wd_ref[0].astype(BF16)

    @pl.when(used)
    def _():
        x = xs_ref[...].astype(BF16)
        hg = _dot(x, wg_b[...])
        hu = _dot(x, wu_b[...])
        hid = (hg * _sigmoid(hg) * hu).astype(BF16)
        ys_ref[...] = _dot(hid, wd_b[...])

    @pl.when(jnp.logical_not(used))
    def _():
        ys_ref[...] = jnp.zeros_like(ys_ref)


def _experts(bexp, eend, xs, e_gate, e_up, e_down, n_blocks):
    n_rows, d = xs.shape
    de = e_gate.shape[2]
    blk = EXPERT_BLOCK
    last = lambda i, nu: jnp.minimum(i, nu[N_EXPERTS - 1] - 1)
    rows = lambda i, be, nu: (last(i, nu), 0)
    wsel = lambda i, be, nu: (be[last(i, nu)], 0, 0)
    return pl.pallas_call(
        _expert_kernel,
        grid_spec=pltpu.PrefetchScalarGridSpec(
            num_scalar_prefetch=2,
            grid=(n_blocks,),
            in_specs=[pl.BlockSpec((blk, d), rows),
                      pl.BlockSpec((1, d, de), wsel),
                      pl.BlockSpec((1, d, de), wsel),
                      pl.BlockSpec((1, de, d), wsel)],
            out_specs=pl.BlockSpec((blk, d), lambda i, be, nu: (i, 0)),
            scratch_shapes=[pltpu.VMEM((d, de), BF16), pltpu.VMEM((d, de), BF16),
                            pltpu.VMEM((de, d), BF16)]),
        out_shape=jax.ShapeDtypeStruct((n_rows, d), F32),
        compiler_params=_params(1),
        name="experts",
    )(bexp, eend, xs, e_gate, e_up, e_down)


def _combine_kernel(dest_ref, dnext_ref, ys_ref, wts_ref, x1_ref, g2_ref, gf_ref, scf_ref, shf_ref, o_ref,
                    buf, sems):
    tm = x1_ref.shape[0]
    i = pl.program_id(0)
    cur = lax.rem(i, 2)

    def gather(d_ref, s):
        def start(g, carry):
            for u in range(ROW_UNROLL):
                r = g * ROW_UNROLL + u
                _row_copy(ys_ref, d_ref[0, 0, r], buf.at[s, 0], r, sems.at[s]).start(priority=u % 2)
                _row_copy(ys_ref, d_ref[0, 1, r], buf.at[s, 1], r, sems.at[s]).start(priority=(u + 1) % 2)
            return carry

        lax.fori_loop(0, tm // ROW_UNROLL, start, 0)

    @pl.when(i == 0)
    def _():
        gather(dest_ref, 0)

    @pl.when(i + 1 < pl.num_programs(0))
    def _():
        gather(dnext_ref, 1 - cur)

    all_rows = pltpu.make_async_copy(ys_ref.at[pl.ds(0, tm), :], buf.at[cur, 0], sems.at[cur])
    all_rows.wait()
    all_rows.wait()
    wts = wts_ref[...]
    moe = buf[cur, 0] * wts[:, 0:1] + buf[cur, 1] * wts[:, 1:2]
    x2 = x1_ref[...] + (1.0 + g2_ref[0]) * moe
    ms = jnp.mean(x2 * x2, axis=-1, keepdims=True)
    y = x2 * lax.rsqrt(ms + NORM_EPS) * gf_ref[...]
    o_ref[...] = y * (1.0 + scf_ref[0]) + shf_ref[0]


def _combine(dest, ys, wts_col, x1, mod, modf, gf, seq):
    nt, _, tm = dest.shape
    n, d = x1.shape
    tps = seq // tm
    return pl.pallas_call(
        _combine_kernel,
        grid=(nt,),
        in_specs=[pl.BlockSpec((1, 2, tm), lambda i: (i, 0, 0), memory_space=pltpu.SMEM),
                  pl.BlockSpec((1, 2, tm), lambda i: (jnp.minimum(i + 1, nt - 1), 0, 0),
                               memory_space=pltpu.SMEM),
                  pl.BlockSpec(memory_space=pl.ANY),
                  pl.BlockSpec((tm, 2), lambda i: (i, 0)),
                  pl.BlockSpec((tm, d), lambda i: (i, 0)),
                  pl.BlockSpec((1, 1, d), lambda i: ((i // tps) * 6 + 5, 0, 0)),
                  pl.BlockSpec((1, d), lambda i: (0, 0)),
                  pl.BlockSpec((1, 1, d), lambda i: ((i // tps) * 2 + 1, 0, 0)),
                  pl.BlockSpec((1, 1, d), lambda i: ((i // tps) * 2 + 0, 0, 0))],
        out_specs=pl.BlockSpec((tm, d), lambda i: (i, 0)),
        out_shape=jax.ShapeDtypeStruct((n, d), F32),
        scratch_shapes=[pltpu.VMEM((2, 2, tm, d), F32), pltpu.SemaphoreType.DMA((2,))],
        compiler_params=_params(1),
        name="combine",
    )(dest, dest, ys, wts_col, x1, mod, gf, modf, modf)


def kernel(x, c, ada_w, ada_b, norm1_g, w_in, shift_mu, w0, w_decay_up, a0, w_aaa_up, w_gate_up,
           k_k, k_a, r_k, gn_g, gn_b, kv_norm_g, k_idx_norm_g, w_uk, w_uv, w_out, norm2_g,
           w_group, b_group, w_expert, b_expert, e_gate, e_up, e_down,
           final_ada_w, final_ada_b, final_norm_g):
    bsz, seq, d = x.shape
    n = bsz * seq
    depth = ada_w.shape[0]
    tm = min(TOKEN_TILE, seq)
    x2 = x.reshape(n, d)

    modf = _ada(c, final_ada_w, final_ada_b).reshape(bsz * 2, 1, d)
    for l in range(depth):
        mod = _ada(c, ada_w[l], ada_b[l]).reshape(bsz * 6, 1, d)

        w1 = w_in[l][:, :RWKV_COLS].astype(BF16)
        w2 = jnp.pad(w_in[l][:, RWKV_COLS:], ((0, 0), (0, DSA_COLS_PAD - DSA_COLS))).astype(BF16)
        p_rwkv, p_dsa = _proj(x2, norm1_g[l].reshape(1, d), mod, w1, w2, seq, tm)

        zeros = jnp.zeros((DECAY_LORA, RWKV_WIDTH), F32)
        wda = jnp.concatenate([jnp.concatenate([w_decay_up[l], zeros], axis=1),
                               jnp.concatenate([zeros, w_aaa_up[l]], axis=1)], axis=0).astype(BF16)
        y_rwkv = _rwkv(p_rwkv, bsz, seq, shift_mu[l], w0[l], wda, a0[l], w_gate_up[l].astype(BF16),
                       k_k[l], k_a[l], r_k[l], gn_g[l], gn_b[l])
        y_dsa = _dsa(p_dsa, bsz, seq, kv_norm_g[l], k_idx_norm_g[l], w_uk[l], w_uv[l])

        wr = jnp.zeros((ROUTER_ROWS, d), F32)
        wr = wr.at[0:N_GROUPS].set(w_group[l].T).at[8:8 + N_EXPERTS].set(w_expert[l].T)
        br = jnp.zeros((ROUTER_ROWS, 1), F32)
        br = br.at[0:N_GROUPS, 0].set(b_group[l]).at[8:8 + N_EXPERTS, 0].set(b_expert[l])
        wo = w_out[l].astype(BF16)
        x1, h2, ids, wts = _mix(y_rwkv, y_dsa, x2, wo[:RWKV_WIDTH], wo[RWKV_WIDTH:], mod,
                                norm2_g[l].reshape(1, d), wr, br, seq, tm)

        n_blocks = (n * 2) // EXPERT_BLOCK + N_EXPERTS
        n_blocks_pad = -(-n_blocks // 128) * 128
        dest, bexp, eend = _sort(ids, n_blocks_pad)
        xs = _dispatch(dest, eend, h2, n_blocks * EXPERT_BLOCK)
        ys = _experts(bexp.reshape(-1), eend.reshape(-1), xs, e_gate[l], e_up[l], e_down[l], n_blocks)
        wts_col = jnp.transpose(wts, (0, 2, 1)).reshape(n, 2)
        last = l == depth - 1
        if not last:
            raise NotImplementedError("stacked layers need a residual-only combine")
        x2 = _combine(dest, ys, wts_col, x1, mod, modf, final_norm_g.reshape(1, d), seq)
    return x2.reshape(bsz, seq, d)
```

```python
import functools

import numpy as np
import jax
import jax.numpy as jnp
from jax import lax
from jax.experimental import pallas as pl
from jax.experimental.pallas import tpu as pltpu

F32 = jnp.float32
BF16 = jnp.bfloat16
I32 = jnp.int32

HEAD_DIM = 64
RWKV_WIDTH = 512
RWKV_HEADS = RWKV_WIDTH // HEAD_DIM
GROUP_HEADS = 2
DSA_WIDTH = 512
DSA_HEADS = DSA_WIDTH // HEAD_DIM
DECAY_LORA = 64
AAA_LORA = 64
GATE_LORA = 128
KV_LATENT = 128
IDX_HEADS = 4
IDX_DIM = 64
RWKV_COLS = 3 * RWKV_WIDTH + DECAY_LORA + AAA_LORA + GATE_LORA
DSA_COLS = DSA_WIDTH + KV_LATENT + IDX_HEADS * IDX_DIM + IDX_DIM + IDX_HEADS
DSA_COLS_PAD = 1024
TOPK_MAX = 256
ATTN_CHUNK_LOG2 = 6
Q_BLOCK = 256
RWKV_CHUNK = 64
RWKV_STEP_CHUNKS = 4
N_GROUPS = 4
EXPERTS_PER_GROUP = 8
N_EXPERTS = N_GROUPS * EXPERTS_PER_GROUP
ROUTER_ROWS = 40
NORM_EPS = 1e-6
GN_EPS = HEAD_DIM * 1e-5
NEG_BIG = -1e30
INT_MIN = -2 ** 31

VMEM_LIMIT_BYTES = 56 * 1024 * 1024
EXPERT_BLOCK = 512
TOKEN_TILE = 512

_NN = (((1,), (0,)), ((), ()))
_NT = (((1,), (1,)), ((), ()))
_TN = (((0,), (0,)), ((), ()))


def _dot(a, b, dims=_NN):
    return lax.dot_general(a, b, dims, preferred_element_type=F32)


def _split(x):
    hi = x.astype(BF16)
    lo = (x - hi.astype(F32)).astype(BF16)
    return hi, lo


def _dot3(a, b, dims=_NN):
    ah, al = _split(a)
    bh, bl = _split(b)
    return _dot(ah, bh, dims) + _dot(ah, bl, dims) + _dot(al, bh, dims)


def _dot_exact_rhs(a, b_bf16, dims=_NN):
    ah, al = _split(a)
    return _dot(ah, b_bf16, dims) + _dot(al, b_bf16, dims)


def _sigmoid(x):
    return 1.0 / (1.0 + jnp.exp(-x))


def _softplus(x):
    return jnp.maximum(x, 0.0) + jnp.log(1.0 + jnp.exp(-jnp.abs(x)))


def _params(n_axes):
    return pltpu.CompilerParams(dimension_semantics=("arbitrary",) * n_axes,
                                vmem_limit_bytes=VMEM_LIMIT_BYTES)


def _ada_kernel(c_ref, w_ref, b_ref, o_ref):
    c = c_ref[...]
    o_ref[...] = _dot3(c * _sigmoid(c), w_ref[...]) + b_ref[...]


def _ada(c, w, b):
    bsz, d = c.shape
    n = w.shape[1]
    tn = 1024
    return pl.pallas_call(
        _ada_kernel,
        grid=(n // tn,),
        in_specs=[pl.BlockSpec((bsz, d), lambda j: (0, 0)),
                  pl.BlockSpec((d, tn), lambda j: (0, j)),
                  pl.BlockSpec((1, tn), lambda j: (0, j))],
        out_specs=pl.BlockSpec((bsz, tn), lambda j: (0, j)),
        out_shape=jax.ShapeDtypeStruct((bsz, n), F32),
        compiler_params=_params(1),
        name="ada",
    )(c, w, b.reshape(1, n))


def _proj_kernel(x_ref, g_ref, sc_ref, sh_ref, w1_ref, w2_ref, o1_ref, o2_ref):
    x = x_ref[...]
    ms = jnp.mean(x * x, axis=-1, keepdims=True)
    y = x * lax.rsqrt(ms + NORM_EPS) * g_ref[...]
    h = (y * (1.0 + sc_ref[0]) + sh_ref[0]).astype(BF16)
    o1_ref[...] = _dot(h, w1_ref[...])
    o2_ref[...] = _dot(h, w2_ref[...])


def _proj(x2, g, mod, w1, w2, seq, tm):
    n, d = x2.shape
    tiles_per_seq = seq // tm
    return pl.pallas_call(
        _proj_kernel,
        grid=(n // tm,),
        in_specs=[pl.BlockSpec((tm, d), lambda i: (i, 0)),
                  pl.BlockSpec((1, d), lambda i: (0, 0)),
                  pl.BlockSpec((1, 1, d), lambda i: ((i // tiles_per_seq) * 6 + 1, 0, 0)),
                  pl.BlockSpec((1, 1, d), lambda i: ((i // tiles_per_seq) * 6 + 0, 0, 0)),
                  pl.BlockSpec(w1.shape, lambda i: (0, 0)),
                  pl.BlockSpec(w2.shape, lambda i: (0, 0))],
        out_specs=[pl.BlockSpec((tm, w1.shape[1]), lambda i: (i, 0)),
                   pl.BlockSpec((tm, w2.shape[1]), lambda i: (i, 0))],
        out_shape=[jax.ShapeDtypeStruct((n, w1.shape[1]), F32),
                   jax.ShapeDtypeStruct((n, w2.shape[1]), F32)],
        compiler_params=_params(1),
        name="proj",
    )(x2, g, mod, mod, w1, w2)


def _rwkv_kernel(p_ref, mu_ref, w0_ref, wda_ref, a0_ref, wg_ref, kk_ref, ka_ref, rk_ref,
                 gng_ref, gnb_ref, blk_ref, tril_ref, masks_ref, o_ref, s_ref, prev_ref):
    c = RWKV_CHUNK
    w = RWKV_WIDTH
    tt = p_ref.shape[0]
    chunks = range(tt // c)

    @pl.when(pl.program_id(1) == 0)
    def _():
        s_ref[...] = jnp.zeros_like(s_ref)
        prev_ref[...] = jnp.zeros_like(prev_ref)

    p = p_ref[...]
    row = lax.broadcasted_iota(I32, p.shape, 0)
    p_prev = jnp.where(row == 0, prev_ref[...], pltpu.roll(p, 1, 0))
    prev_ref[...] = p[tt - 1:tt, :]
    ps = p + mu_ref[...] * (p_prev - p)

    r = ps[:, 0:w]
    k = ps[:, w:2 * w]
    v = ps[:, 2 * w:3 * w]
    lora_in = ps[:, 3 * w:3 * w + DECAY_LORA + AAA_LORA]
    gate_in = ps[:, 3 * w + DECAY_LORA + AAA_LORA:]

    lane = lax.broadcasted_iota(I32, lora_in.shape, 1)
    lora_act = jnp.where(lane < DECAY_LORA, jnp.tanh(lora_in), lora_in).astype(BF16)
    da = _dot(lora_act, wda_ref[...])
    log_w = -_softplus(-(w0_ref[...] + da[:, :w])) - 0.5
    ld = -jnp.exp(log_w)
    a = _sigmoid(a0_ref[...] + da[:, w:])
    gate = _dot(_sigmoid(gate_in).astype(BF16), wg_ref[...])

    blk = blk_ref[...]

    def head_sums(z):
        nb = w // blk.shape[0]
        rows = jnp.concatenate([z[:, i * 128:(i + 1) * 128] for i in range(nb)], axis=0)
        sums = _dot(rows.astype(BF16), blk)
        return jnp.concatenate([sums[i * tt:(i + 1) * tt] for i in range(nb)], axis=1)

    kk = k * kk_ref[...]
    kk = kk / jnp.maximum(jnp.sqrt(head_sums(kk * kk)), 1e-12)
    k2 = k * (1.0 + (a - 1.0) * ka_ref[...])

    cs = _dot_exact_rhs_lhs(tril_ref[...], ld)
    cs_last = [cs[(ch + 1) * c - 1:(ch + 1) * c, :] for ch in chunks]
    cs_end = jnp.concatenate([jnp.broadcast_to(cl, (c, w)) for cl in cs_last], axis=0)
    e_neg = jnp.exp(-cs)
    e_rem = jnp.exp(cs_end - cs)
    kka = kk * a
    a_t = (-kk) * jnp.exp(cs - ld)
    b_t = kka * e_neg
    k_t = k2 * e_neg
    r_t = r * jnp.exp(cs)
    b_h = kka * e_rem
    k_h = k2 * e_rem
    w_c = [jnp.exp(cl) for cl in cs_last]

    bd = masks_ref[0]
    strict = masks_ref[1]
    incl = masks_ref[2]
    eye = masks_ref[3]
    gw = GROUP_HEADS * HEAD_DIM
    stack = lambda z: jnp.concatenate([z] * GROUP_HEADS, axis=0)

    groups = range(RWKV_HEADS // GROUP_HEADS)
    units = [(ch, g) for ch in chunks for g in groups]
    piece = lambda z, u: z[u[0] * c:(u[0] + 1) * c, u[1] * gw:(u[1] + 1) * gw]
    v_x = [(stack(piece(v, u)) * bd).astype(BF16) for u in units]
    ar_x = [jnp.concatenate([(stack(piece(a_t, u)) * bd).astype(BF16),
                             (stack(piece(r_t, u)) * bd).astype(BF16)], axis=0) for u in units]
    bk_r = [jnp.concatenate([stack(piece(b_t, u).astype(BF16)), stack(piece(k_t, u).astype(BF16))], axis=0)
            for u in units]
    bk_h = [jnp.concatenate([stack(piece(b_h, u).astype(BF16)), stack(piece(k_h, u).astype(BF16))], axis=0)
            for u in units]
    quad = [_dot(x, y, _NT) for x, y in zip(ar_x, bk_r)]
    l_ab = [q4[:gw, :gw] * strict for q4 in quad]
    l_ak = [(q4[:gw, gw:] * strict).astype(BF16) for q4 in quad]
    m_cat = [jnp.concatenate([(q4[gw:, :gw] * incl).astype(BF16), (q4[gw:, gw:] * incl).astype(BF16)], axis=1)
             for q4 in quad]

    pw = [p.astype(BF16) for p in l_ab]
    t_inv = [eye + l for l in l_ab]
    pw = [_dot(p, p) for p in pw]
    for _ in range(4):
        pb = [p.astype(BF16) for p in pw]
        both = [_dot(jnp.concatenate([t.astype(BF16), p], axis=0), p) for t, p in zip(t_inv, pb)]
        t_inv = [t + tp[:gw] for t, tp in zip(t_inv, both)]
        pw = [tp[gw:] for tp in both]
    t_inv = [(t + _dot(t.astype(BF16), p.astype(BF16))).astype(BF16) for t, p in zip(t_inv, pw)]

    state = [s_ref[g] for g in groups]
    y_rows = []
    for ch in chunks:
        us = [ch * len(groups) + g for g in groups]
        ar_s = [_dot(ar_x[u], state[g].astype(BF16), _NT) for g, u in zip(groups, us)]
        rhs = [ar_s[g][:gw] + _dot(l_ak[u], v_x[u]) for g, u in zip(groups, us)]
        ub = [_dot(t_inv[u], rhs[g].astype(BF16)).astype(BF16) for g, u in zip(groups, us)]
        uv = [jnp.concatenate([ub[g], v_x[u]], axis=0) for g, u in zip(groups, us)]
        y_x = [ar_s[g][gw:] + _dot(m_cat[u], uv[g]) for g, u in zip(groups, us)]
        y_rows.append(jnp.concatenate(
            [sum(yx[hh * c:(hh + 1) * c] for hh in range(1, GROUP_HEADS)) + yx[0:c] for yx in y_x], axis=1))
        state = [(state[g] * w_c[ch][:, g * gw:(g + 1) * gw] + _dot(uv[g], bk_h[u], _TN)) * bd
                 for g, u in zip(groups, us)]
    for g in groups:
        s_ref[g] = state[g]

    y = jnp.concatenate(y_rows, axis=0)
    inv_n = 1.0 / HEAD_DIM
    mean = head_sums(y) * inv_n
    dlt = y - mean
    var = head_sums(dlt * dlt) * inv_n
    yn = dlt * lax.rsqrt(var + GN_EPS) * gng_ref[...] + gnb_ref[...]
    bonus = head_sums(r * k2 * rk_ref[...]) * v
    o_ref[...] = ((yn + bonus) * gate).astype(o_ref.dtype)


def _dot_exact_rhs_lhs(a_bf16, b):
    bh, bl = _split(b)
    return _dot(a_bf16, bh) + _dot(a_bf16, bl)


def _rwkv(p, bsz, seq, mu, w0, wda, a0, wg, k_k, k_a, r_k, gn_g, gn_b):
    c = RWKV_CHUNK
    w = RWKV_WIDTH
    tt = c * RWKV_STEP_CHUNKS if seq % (c * RWKV_STEP_CHUNKS) == 0 else c
    n_steps = seq // tt
    head_of = np.arange(128) // HEAD_DIM
    blk = jnp.asarray(head_of[:, None] == head_of[None, :], BF16)
    ti = np.arange(tt)
    tril = jnp.asarray((ti[:, None] >= ti[None, :]) & (ti[:, None] // c == ti[None, :] // c), BF16)
    assert c == HEAD_DIM and RWKV_HEADS % GROUP_HEADS == 0
    gi = np.arange(GROUP_HEADS * c)
    same = (gi[:, None] // c) == (gi[None, :] // c)
    later = (gi[:, None] % c) > (gi[None, :] % c)
    masks = jnp.asarray(np.stack([same, same & later, same & (later | (gi[:, None] % c == gi[None, :] % c)),
                                  gi[:, None] == gi[None, :]]), F32)
    row = lambda a: a.reshape(1, -1)
    vec = lambda n: pl.BlockSpec((1, n), lambda b, t: (0, 0))
    full = lambda a: pl.BlockSpec(a.shape, lambda b, t: (0, 0))
    return pl.pallas_call(
        _rwkv_kernel,
        grid=(bsz, n_steps),
        in_specs=[pl.BlockSpec((tt, RWKV_COLS), lambda b, t: (b * n_steps + t, 0)),
                  vec(RWKV_COLS), vec(w), full(wda), vec(w), full(wg), vec(w), vec(w), vec(w),
                  vec(w), vec(w), full(blk), full(tril),
                  pl.BlockSpec(masks.shape, lambda b, t: (0, 0, 0))],
        out_specs=pl.BlockSpec((tt, w), lambda b, t: (b * n_steps + t, 0)),
        out_shape=jax.ShapeDtypeStruct((bsz * seq, w), BF16),
        scratch_shapes=[pltpu.VMEM((RWKV_HEADS // GROUP_HEADS, GROUP_HEADS * HEAD_DIM, GROUP_HEADS * HEAD_DIM), F32),
                        pltpu.VMEM((1, RWKV_COLS), F32)],
        compiler_params=_params(2),
        name="rwkv",
    )(p, row(mu), row(w0), wda, row(a0), wg, row(k_k), row(k_a), row(r_k), row(gn_g), row(gn_b),
      blk, tril, masks)


SLAB_ROWS = 64
ONES_ROWS = 16


def _fold_rows(x, op):
    acc = x[0:SLAB_ROWS]
    for r0 in range(SLAB_ROWS, x.shape[0], SLAB_ROWS):
        acc = op(acc, x[r0:r0 + SLAB_ROWS])
    return acc


def _split3(x):
    hi = x.astype(BF16)
    rest = x - hi.astype(F32)
    mid = rest.astype(BF16)
    return hi, mid, (rest - mid.astype(F32)).astype(BF16)


def _dsa_kernel(pd_ref, kvg_ref, kig_ref, wuk_ref, wuv_ref, tril_ref, wsel_ref, o_ref,
                ckv_s, ckvt_s, kcat_s, score_s, *, nq, n_cls, topk):
    qb = Q_BLOCK
    j = pl.program_id(1)

    @pl.when(j == 0)
    def _():
        ckv_s[...] = jnp.zeros_like(ckv_s)
        ckvt_s[...] = jnp.zeros_like(ckvt_s)
        kcat_s[...] = jnp.zeros_like(kcat_s)

    pd = pd_ref[...]
    q = pd[:, :DSA_WIDTH]
    ckv = pd[:, DSA_WIDTH:DSA_WIDTH + KV_LATENT]
    qi = pd[:, DSA_WIDTH + KV_LATENT:DSA_WIDTH + KV_LATENT + IDX_HEADS * IDX_DIM]
    tail = pd[:, DSA_WIDTH + KV_LATENT + IDX_HEADS * IDX_DIM:]
    ki = tail[:, :IDX_DIM]

    ckv_n = ckv * lax.rsqrt(jnp.mean(ckv * ckv, axis=-1, keepdims=True) + NORM_EPS) * kvg_ref[...]
    ki_n = ki * lax.rsqrt(jnp.mean(ki * ki, axis=-1, keepdims=True) + NORM_EPS) * kig_ref[...]
    ki_hi, ki_lo = _split(ki_n)
    rows = pl.ds(pl.multiple_of(j * qb, qb), qb)
    ckv_s[rows, :] = ckv_n.astype(BF16)
    one_row = jnp.where(lax.broadcasted_iota(I32, (ONES_ROWS, qb), 0) == 0, 1.0, 0.0)
    ckvt_s[j] = jnp.concatenate([ckv_n.T, one_row], axis=0).astype(BF16)
    kcat_s[rows, :] = jnp.concatenate([ki_hi, ki_hi, ki_lo, jnp.zeros_like(ki_hi)], axis=1)

    w_t = sum(_dot(wsel_ref[...], part, _NT) for part in _split3(tail))

    per = nq // n_cls
    for cls in range(n_cls):
        @pl.when((j >= cls * per) & (j < (cls + 1) * per))
        def _(sk=(cls + 1) * per * qb):
            _dsa_block(j, q, qi, w_t, wuk_ref, wuv_ref, tril_ref, o_ref, ckv_s, ckvt_s, kcat_s,
                       score_s, sk=sk, topk=topk)


def _dsa_block(j, q, qi, w_t, wuk_ref, wuv_ref, tril_ref, o_ref, ckv_s, ckvt_s, kcat_s, score_s,
               *, sk, topk):
    qb = Q_BLOCK
    kcat = kcat_s[0:sk, :]
    score = jnp.zeros((sk, qb), F32)
    w_s = w_t * (IDX_HEADS ** -0.5)
    for hh in range(IDX_HEADS):
        q_hi, q_lo = _split(qi[:, hh * IDX_DIM:(hh + 1) * IDX_DIM] * (IDX_DIM ** -0.5))
        dots = _dot(kcat, jnp.concatenate([q_hi, q_lo, q_hi, jnp.zeros_like(q_hi)], axis=1), _NT)
        score = score + w_s[hh:hh + 1, :] * jnp.maximum(dots, 0.0)

    key_pos = lax.broadcasted_iota(I32, (sk, qb), 0)
    q_pos = j * qb + lax.broadcasted_iota(I32, (sk, qb), 1)
    adm = jnp.right_shift(key_pos, ATTN_CHUNK_LOG2) <= jnp.right_shift(q_pos, ATTN_CHUNK_LOG2)

    score_s[0:sk, :] = jnp.where(adm, score, NEG_BIG)

    def as_float(okey):
        return lax.bitcast_convert_type(okey ^ ((okey >> 31) & 0x7FFFFFFF), F32)

    def descend(i, tu):
        cand_u = tu | jnp.left_shift(jnp.int32(1), 31 - i)
        cand = as_float(cand_u ^ INT_MIN)
        acc = jnp.zeros((SLAB_ROWS, qb), F32)
        for r0 in range(0, sk, SLAB_ROWS):
            acc = acc + jnp.where(score_s[r0:r0 + SLAB_ROWS, :] >= cand, 1.0, 0.0)
        cnt = jnp.sum(acc, axis=0, keepdims=True)
        return jnp.where(cnt >= topk, cand_u, tu)

    thr = as_float(lax.fori_loop(0, 32, descend, jnp.zeros((1, qb), I32)) ^ INT_MIN)

    score = score_s[0:sk, :]
    gt = score > thr
    eq = score == thr
    need = topk - jnp.sum(_fold_rows(jnp.where(gt, 1.0, 0.0), jnp.add), axis=0, keepdims=True)
    eq_b = jnp.where(eq, 1.0, 0.0).astype(BF16)
    tb = tril_ref.shape[0]
    off = jnp.zeros((1, qb), F32)
    pieces = []
    for kb in range(sk // tb):
        pre = _dot(tril_ref[...], eq_b[kb * tb:(kb + 1) * tb, :])
        pieces.append(pre + off)
        off = off + pre[tb - 1:tb, :]
    prefix = jnp.concatenate(pieces, axis=0)
    tie_bias = jnp.where(prefix <= need, 0.0, NEG_BIG)
    bias = jnp.where(gt, 0.0, jnp.where(eq, tie_bias, NEG_BIG))
    bias = jnp.where(adm, bias, NEG_BIG)

    ckv_all = ckv_s[0:sk, :]
    ckv_t = jnp.concatenate([ckvt_s[b] for b in range(sk // qb)], axis=1)
    q_lat = (_dot(q.astype(BF16), wuk_ref[...]) * (HEAD_DIM ** -0.5)).astype(BF16)
    head_logits = lambda h: _dot(ckv_all, q_lat[:, h * KV_LATENT:(h + 1) * KV_LATENT], _NT)
    outs = []
    nxt = head_logits(0)
    for h in range(DSA_HEADS):
        logits = nxt + bias
        if h + 1 < DSA_HEADS:
            nxt = head_logits(h + 1)
        mx = jnp.max(_fold_rows(logits, jnp.maximum), axis=0, keepdims=True)
        pr = jnp.exp((logits - mx).astype(BF16))
        acc = _dot(ckv_t, pr)
        outs.append((acc[:KV_LATENT] / acc[KV_LATENT:KV_LATENT + 1]).astype(BF16))
    o_lat_t = jnp.concatenate(outs, axis=0)
    o_ref[...] = _dot(o_lat_t, wuv_ref[...], _TN).astype(o_ref.dtype)


def _dsa(pd, bsz, seq, kv_g, ki_g, w_uk, w_uv):
    qb = Q_BLOCK
    nq = seq // qb
    topk = min(TOPK_MAX, seq // 4)
    hc = DSA_HEADS * KV_LATENT
    head_of_q = np.arange(DSA_WIDTH) // HEAD_DIM
    head_of_l = np.arange(hc) // KV_LATENT
    mask = jnp.asarray(head_of_q[:, None] == head_of_l[None, :], F32)
    uk = jnp.transpose(w_uk, (1, 2, 0)).reshape(DSA_WIDTH, KV_LATENT)
    wuk_bd = (jnp.tile(uk, (1, DSA_HEADS)) * mask).astype(BF16)
    uv = w_uv.reshape(KV_LATENT, DSA_WIDTH)
    wuv_bd = (jnp.tile(uv, (DSA_HEADS, 1)) * mask.T).astype(BF16)
    n_cls = 4 if nq % 4 == 0 else 1
    tb = 256 if ((nq // n_cls) * qb) % 256 == 0 else 128
    tril = jnp.asarray(np.tril(np.ones((tb, tb), np.float32)), BF16)
    wsel = jnp.asarray(np.arange(128)[None, :] == (IDX_DIM + np.arange(8))[:, None], BF16)
    wsel = wsel * jnp.asarray(np.arange(8)[:, None] < IDX_HEADS, BF16)
    full = lambda a: pl.BlockSpec(a.shape, lambda b, j: (0, 0))
    return pl.pallas_call(
        functools.partial(_dsa_kernel, nq=nq, n_cls=n_cls, topk=topk),
        grid=(bsz, nq),
        in_specs=[pl.BlockSpec((qb, DSA_COLS_PAD), lambda b, j: (b * nq + j, 0)),
                  pl.BlockSpec((1, KV_LATENT), lambda b, j: (0, 0)),
                  pl.BlockSpec((1, IDX_DIM), lambda b, j: (0, 0)),
                  full(wuk_bd), full(wuv_bd), full(tril), full(wsel)],
        out_specs=pl.BlockSpec((qb, DSA_WIDTH), lambda b, j: (b * nq + j, 0)),
        out_shape=jax.ShapeDtypeStruct((bsz * seq, DSA_WIDTH), BF16),
        scratch_shapes=[pltpu.VMEM((seq, KV_LATENT), BF16),
                        pltpu.VMEM((nq, KV_LATENT + ONES_ROWS, qb), BF16),
                        pltpu.VMEM((seq, 4 * IDX_DIM), BF16),
                        pltpu.VMEM((seq, qb), F32)],
        compiler_params=_params(2),
        name="dsa",
    )(pd, kv_g.reshape(1, -1), ki_g.reshape(1, -1), wuk_bd, wuv_bd, tril, wsel)


def _mix_kernel(yr_ref, yd_ref, x_ref, wo1_ref, wo2_ref, g1_ref, n2_ref, sc_ref, sh_ref,
                wr_ref, br_ref, x1_ref, h2_ref, ids_ref, wts_ref):
    mixed = _dot(yr_ref[...], wo1_ref[...]) + _dot(yd_ref[...], wo2_ref[...])
    x1 = x_ref[...] + (1.0 + g1_ref[0]) * mixed
    x1_ref[...] = x1
    ms = jnp.mean(x1 * x1, axis=-1, keepdims=True)
    h2 = x1 * lax.rsqrt(ms + NORM_EPS) * n2_ref[...] * (1.0 + sc_ref[0]) + sh_ref[0]
    h2_ref[...] = h2

    lt = _dot3(wr_ref[...], h2, _NT) + br_ref[...]
    gl = lt[0:N_GROUPS]
    el = lt[8:8 + N_EXPERTS]
    tm = gl.shape[1]
    gmax = jnp.max(gl, axis=0, keepdims=True)
    gidx = lax.broadcasted_iota(I32, (N_GROUPS, tm), 0)
    gsel = jnp.min(jnp.where(gl == gmax, gidx, N_GROUPS), axis=0, keepdims=True)
    p_group = 1.0 / jnp.sum(jnp.exp(gl - gmax), axis=0, keepdims=True)
    eidx = lax.broadcasted_iota(I32, (N_EXPERTS, tm), 0)
    el = jnp.where(jnp.right_shift(eidx, 3) == gsel, el, NEG_BIG)
    m1 = jnp.max(el, axis=0, keepdims=True)
    i1 = jnp.min(jnp.where(el == m1, eidx, N_EXPERTS), axis=0, keepdims=True)
    el2 = jnp.where(eidx == i1, NEG_BIG, el)
    m2 = jnp.max(el2, axis=0, keepdims=True)
    i2 = jnp.min(jnp.where(el2 == m2, eidx, N_EXPERTS), axis=0, keepdims=True)
    e2 = jnp.exp(m2 - m1)
    w1 = p_group / (1.0 + e2)
    ids_ref[0] = jnp.concatenate([i1, i2], axis=0)
    wts_ref[0] = jnp.concatenate([w1, w1 * e2], axis=0)


def _mix(yr, yd, x2, wo1, wo2, mod, n2g, wr, br, seq, tm):
    n, d = x2.shape
    nt = n // tm
    tps = seq // tm
    modspec = lambda k: pl.BlockSpec((1, 1, d), lambda i: ((i // tps) * 6 + k, 0, 0))
    full = lambda a: pl.BlockSpec(a.shape, lambda i: (0,) * a.ndim)
    return pl.pallas_call(
        _mix_kernel,
        grid=(nt,),
        in_specs=[pl.BlockSpec((tm, yr.shape[1]), lambda i: (i, 0)),
                  pl.BlockSpec((tm, yd.shape[1]), lambda i: (i, 0)),
                  pl.BlockSpec((tm, d), lambda i: (i, 0)),
                  full(wo1), full(wo2), modspec(2), full(n2g), modspec(4), modspec(3),
                  full(wr), full(br)],
        out_specs=[pl.BlockSpec((tm, d), lambda i: (i, 0)),
                   pl.BlockSpec((tm, d), lambda i: (i, 0)),
                   pl.BlockSpec((1, 2, tm), lambda i: (i, 0, 0)),
                   pl.BlockSpec((1, 2, tm), lambda i: (i, 0, 0))],
        out_shape=[jax.ShapeDtypeStruct((n, d), F32),
                   jax.ShapeDtypeStruct((n, d), F32),
                   jax.ShapeDtypeStruct((nt, 2, tm), I32),
                   jax.ShapeDtypeStruct((nt, 2, tm), F32)],
        compiler_params=_params(1),
        name="mix",
    )(yr, yd, x2, wo1, wo2, mod, n2g, mod, mod, wr, br)


def _sort_kernel(ids_ref, triu_ref, tril_ref, dest_ref, bexp_ref, eend_ref,
                 cnt_ref, run_ref, start_ref, *, n_blocks_pad):
    phase = pl.program_id(0)
    i = pl.program_id(1)
    ids = ids_ref[0]
    tm = ids.shape[1]
    eidx = lax.broadcasted_iota(I32, (N_EXPERTS, tm), 0)
    hit0 = eidx == ids[0:1]
    hit1 = eidx == ids[1:2]
    onehot = jnp.where(hit0, 1.0, 0.0) + jnp.where(hit1, 1.0, 0.0)

    @pl.when((phase == 0) & (i == 0))
    def _():
        cnt_ref[...] = jnp.zeros_like(cnt_ref)

    @pl.when(phase == 0)
    def _():
        cnt_ref[...] += jnp.sum(onehot, axis=1, keepdims=True)

    @pl.when((phase == 1) & (i == 0))
    def _():
        run_ref[...] = jnp.zeros_like(run_ref)
        nblk = jnp.floor((cnt_ref[...] + (EXPERT_BLOCK - 1)) * (1.0 / EXPERT_BLOCK))
        nblk_b = jnp.broadcast_to(nblk, (N_EXPERTS, 128))
        first_blk = _dot_exact_rhs_lhs(tril_ref[...], nblk_b)
        start_ref[...] = first_blk[:, 0:1] * EXPERT_BLOCK
        end_blk = first_blk + nblk_b
        bidx = lax.broadcasted_iota(I32, (N_EXPERTS, n_blocks_pad), 1).astype(F32)
        owner = jnp.sum(jnp.where(end_blk[:, 0:1] <= bidx, 1.0, 0.0), axis=0, keepdims=True)
        bexp_ref[...] = jnp.minimum(owner, N_EXPERTS - 1).astype(I32)
        on_diag = (lax.broadcasted_iota(I32, (N_EXPERTS, 128), 0)
                   == lax.broadcasted_iota(I32, (N_EXPERTS, 128), 1))
        eend_ref[...] = jnp.sum(jnp.where(on_diag, end_blk, 0.0), axis=0, keepdims=True).astype(I32)

    @pl.when(phase == 1)
    def _():
        before = _dot(onehot.astype(BF16), triu_ref[...])
        pos = start_ref[...] + run_ref[...] + before
        d0 = jnp.sum(jnp.where(hit0, pos, 0.0), axis=0, keepdims=True)
        d1 = jnp.sum(jnp.where(hit1, pos, 0.0), axis=0, keepdims=True)
        dest_ref[0] = jnp.concatenate([d0, d1], axis=0).astype(I32)
        run_ref[...] += jnp.sum(onehot, axis=1, keepdims=True)


def _sort(ids, n_blocks_pad):
    nt, _, tm = ids.shape
    triu = jnp.asarray(np.triu(np.ones((tm, tm), np.float32), 1), BF16)
    tril = jnp.asarray(np.tril(np.ones((N_EXPERTS, N_EXPERTS), np.float32), -1), BF16)
    return pl.pallas_call(
        functools.partial(_sort_kernel, n_blocks_pad=n_blocks_pad),
        grid=(2, nt),
        in_specs=[pl.BlockSpec((1, 2, tm), lambda p, i: (i, 0, 0)),
                  pl.BlockSpec(triu.shape, lambda p, i: (0, 0)),
                  pl.BlockSpec(tril.shape, lambda p, i: (0, 0))],
        out_specs=[pl.BlockSpec((1, 2, tm), lambda p, i: (i * p, 0, 0)),
                   pl.BlockSpec((1, n_blocks_pad), lambda p, i: (0, 0)),
                   pl.BlockSpec((1, 128), lambda p, i: (0, 0))],
        out_shape=[jax.ShapeDtypeStruct((nt, 2, tm), I32),
                   jax.ShapeDtypeStruct((1, n_blocks_pad), I32),
                   jax.ShapeDtypeStruct((1, 128), I32)],
        scratch_shapes=[pltpu.VMEM((N_EXPERTS, 1), F32),
                        pltpu.VMEM((N_EXPERTS, 1), F32),
                        pltpu.VMEM((N_EXPERTS, 1), F32)],
        compiler_params=_params(2),
        name="sort",
    )(ids, triu, tril)


def _row_copy(src_ref, src_row, dst_ref, dst_row, sem):
    return pltpu.make_async_copy(src_ref.at[pl.ds(src_row, 1), :], dst_ref.at[pl.ds(dst_row, 1), :], sem)


ROW_UNROLL = 8


def _dispatch_kernel(dest_ref, eend_ref, h_ref, xs_ref, zbuf, sem, zsem):
    tm = h_ref.shape[0]

    @pl.when(pl.program_id(0) == 0)
    def _():
        zbuf[...] = jnp.zeros_like(zbuf)

        def last_block_copy(e):
            end_b = eend_ref[0, e]
            begin_b = jnp.where(e == 0, 0, eend_ref[0, jnp.maximum(e - 1, 0)])
            row0 = pl.multiple_of((end_b - 1) * EXPERT_BLOCK, EXPERT_BLOCK)
            return end_b > begin_b, pltpu.make_async_copy(zbuf, xs_ref.at[pl.ds(row0, EXPERT_BLOCK), :], zsem)

        def fill(e, carry):
            owns_rows, cp = last_block_copy(e)
            pl.when(owns_rows)(cp.start)
            return carry

        def drain(e, carry):
            owns_rows, cp = last_block_copy(e)
            pl.when(owns_rows)(cp.wait)
            return carry

        lax.fori_loop(0, N_EXPERTS, fill, 0)
        lax.fori_loop(0, N_EXPERTS, drain, 0)

        def spare_block_copy(b):
            row0 = pl.multiple_of(b * EXPERT_BLOCK, EXPERT_BLOCK)
            return pltpu.make_async_copy(zbuf, xs_ref.at[pl.ds(row0, EXPERT_BLOCK), :], zsem)

        n_used = eend_ref[0, N_EXPERTS - 1]
        n_blocks = xs_ref.shape[0] // EXPERT_BLOCK
        lax.fori_loop(n_used, n_blocks, lambda b, carry: (spare_block_copy(b).start(), carry)[1], 0)
        lax.fori_loop(n_used, n_blocks, lambda b, carry: (spare_block_copy(b).wait(), carry)[1], 0)

    def start(g, carry):
        for u in range(ROW_UNROLL):
            r = g * ROW_UNROLL + u
            _row_copy(h_ref, r, xs_ref, dest_ref[0, 0, r], sem).start(priority=u % 2)
            _row_copy(h_ref, r, xs_ref, dest_ref[0, 1, r], sem).start(priority=(u + 1) % 2)
        return carry

    lax.fori_loop(0, tm // ROW_UNROLL, start, 0)
    all_rows = pltpu.make_async_copy(h_ref, xs_ref.at[pl.ds(0, tm), :], sem)
    all_rows.wait()
    all_rows.wait()


def _dispatch(dest, eend, h2, n_rows):
    nt, _, tm = dest.shape
    n, d = h2.shape
    return pl.pallas_call(
        _dispatch_kernel,
        grid=(nt,),
        in_specs=[pl.BlockSpec((1, 2, tm), lambda i: (i, 0, 0), memory_space=pltpu.SMEM),
                  pl.BlockSpec(eend.shape, lambda i: (0, 0), memory_space=pltpu.SMEM),
                  pl.BlockSpec((tm, d), lambda i: (i, 0))],
        out_specs=pl.BlockSpec(memory_space=pl.ANY),
        out_shape=jax.ShapeDtypeStruct((n_rows, d), F32),
        scratch_shapes=[pltpu.VMEM((EXPERT_BLOCK, d), F32),
                        pltpu.SemaphoreType.DMA(()), pltpu.SemaphoreType.DMA(())],
        compiler_params=_params(1),
        name="dispatch",
    )(dest, eend, h2)


def _expert_kernel(bexp_ref, eend_ref, xs_ref, wg_ref, wu_ref, wd_ref, ys_ref, wg_b, wu_b, wd_b):
    i = pl.program_id(0)
    used = i < eend_ref[N_EXPERTS - 1]
    new_expert = jnp.logical_or(i == 0, bexp_ref[i] != bexp_ref[jnp.maximum(i - 1, 0)])

    @pl.when(jnp.logical_and(used, new_expert))
    def _():
        wg_b[...] = wg_ref[0].astype(BF16)
        wu_b[...] = wu_ref[0].astype(BF16)
        wd_b[...] = wd_ref[0].astype(BF16)

    @pl.when(used)
    def _():
        x = xs_ref[...].astype(BF16)
        hg = _dot(x, wg_b[...])
        hu = _dot(x, wu_b[...])
        hid = (hg * _sigmoid(hg) * hu).astype(BF16)
        ys_ref[...] = _dot(hid, wd_b[...])

    @pl.when(jnp.logical_not(used))
    def _():
        ys_ref[...] = jnp.zeros_like(ys_ref)


def _experts(bexp, eend, xs, e_gate, e_up, e_down, n_blocks):
    n_rows, d = xs.shape
    de = e_gate.shape[2]
    blk = EXPERT_BLOCK
    last = lambda i, nu: jnp.minimum(i, nu[N_EXPERTS - 1] - 1)
    rows = lambda i, be, nu: (last(i, nu), 0)
    wsel = lambda i, be, nu: (be[last(i, nu)], 0, 0)
    return pl.pallas_call(
        _expert_kernel,
        grid_spec=pltpu.PrefetchScalarGridSpec(
            num_scalar_prefetch=2,
            grid=(n_blocks,),
            in_specs=[pl.BlockSpec((blk, d), rows),
                      pl.BlockSpec((1, d, de), wsel),
                      pl.BlockSpec((1, d, de), wsel),
                      pl.BlockSpec((1, de, d), wsel)],
            out_specs=pl.BlockSpec((blk, d), lambda i, be, nu: (i, 0)),
            scratch_shapes=[pltpu.VMEM((d, de), BF16), pltpu.VMEM((d, de), BF16),
                            pltpu.VMEM((de, d), BF16)]),
        out_shape=jax.ShapeDtypeStruct((n_rows, d), F32),
        compiler_params=_params(1),
        name="experts",
    )(bexp, eend, xs, e_gate, e_up, e_down)


def _combine_kernel(dest_ref, dnext_ref, ys_ref, wts_ref, x1_ref, g2_ref, gf_ref, scf_ref, shf_ref, o_ref,
                    buf, sems):
    tm = x1_ref.shape[0]
    i = pl.program_id(0)
    cur = lax.rem(i, 2)

    def gather(d_ref, s):
        def start(g, carry):
            for u in range(ROW_UNROLL):
                r = g * ROW_UNROLL + u
                _row_copy(ys_ref, d_ref[0, 0, r], buf.at[s, 0], r, sems.at[s]).start(priority=u % 2)
                _row_copy(ys_ref, d_ref[0, 1, r], buf.at[s, 1], r, sems.at[s]).start(priority=(u + 1) % 2)
            return carry

        lax.fori_loop(0, tm // ROW_UNROLL, start, 0)

    @pl.when(i == 0)
    def _():
        gather(dest_ref, 0)

    @pl.when(i + 1 < pl.num_programs(0))
    def _():
        gather(dnext_ref, 1 - cur)

    all_rows = pltpu.make_async_copy(ys_ref.at[pl.ds(0, tm), :], buf.at[cur, 0], sems.at[cur])
    all_rows.wait()
    all_rows.wait()
    wts = wts_ref[...]
    moe = buf[cur, 0] * wts[:, 0:1] + buf[cur, 1] * wts[:, 1:2]
    x2 = x1_ref[...] + (1.0 + g2_ref[0]) * moe
    ms = jnp.mean(x2 * x2, axis=-1, keepdims=True)
    y = x2 * lax.rsqrt(ms + NORM_EPS) * gf_ref[...]
    o_ref[...] = y * (1.0 + scf_ref[0]) + shf_ref[0]


def _combine(dest, ys, wts_col, x1, mod, modf, gf, seq):
    nt, _, tm = dest.shape
    n, d = x1.shape
    tps = seq // tm
    return pl.pallas_call(
        _combine_kernel,
        grid=(nt,),
        in_specs=[pl.BlockSpec((1, 2, tm), lambda i: (i, 0, 0), memory_space=pltpu.SMEM),
                  pl.BlockSpec((1, 2, tm), lambda i: (jnp.minimum(i + 1, nt - 1), 0, 0),
                               memory_space=pltpu.SMEM),
                  pl.BlockSpec(memory_space=pl.ANY),
                  pl.BlockSpec((tm, 2), lambda i: (i, 0)),
                  pl.BlockSpec((tm, d), lambda i: (i, 0)),
                  pl.BlockSpec((1, 1, d), lambda i: ((i // tps) * 6 + 5, 0, 0)),
                  pl.BlockSpec((1, d), lambda i: (0, 0)),
                  pl.BlockSpec((1, 1, d), lambda i: ((i // tps) * 2 + 1, 0, 0)),
                  pl.BlockSpec((1, 1, d), lambda i: ((i // tps) * 2 + 0, 0, 0))],
        out_specs=pl.BlockSpec((tm, d), lambda i: (i, 0)),
        out_shape=jax.ShapeDtypeStruct((n, d), F32),
        scratch_shapes=[pltpu.VMEM((2, 2, tm, d), F32), pltpu.SemaphoreType.DMA((2,))],
        compiler_params=_params(1),
        name="combine",
    )(dest, dest, ys, wts_col, x1, mod, gf, modf, modf)


def kernel(x, c, ada_w, ada_b, norm1_g, w_in, shift_mu, w0, w_decay_up, a0, w_aaa_up, w_gate_up,
           k_k, k_a, r_k, gn_g, gn_b, kv_norm_g, k_idx_norm_g, w_uk, w_uv, w_out, norm2_g,
           w_group, b_group, w_expert, b_expert, e_gate, e_up, e_down,
           final_ada_w, final_ada_b, final_norm_g):
    bsz, seq, d = x.shape
    n = bsz * seq
    depth = ada_w.shape[0]
    tm = min(TOKEN_TILE, seq)
    x2 = x.reshape(n, d)

    modf = _ada(c, final_ada_w, final_ada_b).reshape(bsz * 2, 1, d)
    for l in range(depth):
        mod = _ada(c, ada_w[l], ada_b[l]).reshape(bsz * 6, 1, d)

        w1 = w_in[l][:, :RWKV_COLS].astype(BF16)
        w2 = jnp.pad(w_in[l][:, RWKV_COLS:], ((0, 0), (0, DSA_COLS_PAD - DSA_COLS))).astype(BF16)
        p_rwkv, p_dsa = _proj(x2, norm1_g[l].reshape(1, d), mod, w1, w2, seq, tm)

        zeros = jnp.zeros((DECAY_LORA, RWKV_WIDTH), F32)
        wda = jnp.concatenate([jnp.concatenate([w_decay_up[l], zeros], axis=1),
                               jnp.concatenate([zeros, w_aaa_up[l]], axis=1)], axis=0).astype(BF16)
        y_rwkv = _rwkv(p_rwkv, bsz, seq, shift_mu[l], w0[l], wda, a0[l], w_gate_up[l].astype(BF16),
                       k_k[l], k_a[l], r_k[l], gn_g[l], gn_b[l])
        y_dsa = _dsa(p_dsa, bsz, seq, kv_norm_g[l], k_idx_norm_g[l], w_uk[l], w_uv[l])

        wr = jnp.zeros((ROUTER_ROWS, d), F32)
        wr = wr.at[0:N_GROUPS].set(w_group[l].T).at[8:8 + N_EXPERTS].set(w_expert[l].T)
        br = jnp.zeros((ROUTER_ROWS, 1), F32)
        br = br.at[0:N_GROUPS, 0].set(b_group[l]).at[8:8 + N_EXPERTS, 0].set(b_expert[l])
        wo = w_out[l].astype(BF16)
        x1, h2, ids, wts = _mix(y_rwkv, y_dsa, x2, wo[:RWKV_WIDTH], wo[RWKV_WIDTH:], mod,
                                norm2_g[l].reshape(1, d), wr, br, seq, tm)

        n_blocks = (n * 2) // EXPERT_BLOCK + N_EXPERTS
        n_blocks_pad = -(-n_blocks // 128) * 128
        dest, bexp, eend = _sort(ids, n_blocks_pad)
        xs = _dispatch(dest, eend, h2, n_blocks * EXPERT_BLOCK)
        ys = _experts(bexp.reshape(-1), eend.reshape(-1), xs, e_gate[l], e_up[l], e_down[l], n_blocks)
        wts_col = jnp.transpose(wts, (0, 2, 1)).reshape(n, 2)
        last = l == depth - 1
        if not last:
            raise NotImplementedError("stacked layers need a residual-only combine")
        x2 = _combine(dest, ys, wts_col, x1, mod, modf, final_norm_g.reshape(1, d), seq)
    return x2.reshape(bsz, seq, d)
```

```python
import functools

import numpy as np
import jax
import jax.numpy as jnp
from jax import lax
from jax.experimental import pallas as pl
from jax.experimental.pallas import tpu as pltpu

F32 = jnp.float32
BF16 = jnp.bfloat16
I32 = jnp.int32

HEAD_DIM = 64
RWKV_WIDTH = 512
RWKV_HEADS = RWKV_WIDTH // HEAD_DIM
GROUP_HEADS = 2
DSA_WIDTH = 512
DSA_HEADS = DSA_WIDTH // HEAD_DIM
DECAY_LORA = 64
AAA_LORA = 64
GATE_LORA = 128
KV_LATENT = 128
IDX_HEADS = 4
IDX_DIM = 64
RWKV_COLS = 3 * RWKV_WIDTH + DECAY_LORA + AAA_LORA + GATE_LORA
DSA_COLS = DSA_WIDTH + KV_LATENT + IDX_HEADS * IDX_DIM + IDX_DIM + IDX_HEADS
DSA_COLS_PAD = 1024
TOPK_MAX = 256
ATTN_CHUNK_LOG2 = 6
Q_BLOCK = 256
RWKV_CHUNK = 64
RWKV_STEP_CHUNKS = 4
N_GROUPS = 4
EXPERTS_PER_GROUP = 8
N_EXPERTS = N_GROUPS * EXPERTS_PER_GROUP
ROUTER_ROWS = 40
NORM_EPS = 1e-6
GN_EPS = HEAD_DIM * 1e-5
NEG_BIG = -1e30
INT_MIN = -2 ** 31

VMEM_LIMIT_BYTES = 56 * 1024 * 1024
EXPERT_BLOCK = 512
TOKEN_TILE = 512

_NN = (((1,), (0,)), ((), ()))
_NT = (((1,), (1,)), ((), ()))
_TN = (((0,), (0,)), ((), ()))


def _dot(a, b, dims=_NN):
    return lax.dot_general(a, b, dims, preferred_element_type=F32)


def _split(x):
    hi = x.astype(BF16)
    lo = (x - hi.astype(F32)).astype(BF16)
    return hi, lo


def _dot3(a, b, dims=_NN):
    ah, al = _split(a)
    bh, bl = _split(b)
    return _dot(ah, bh, dims) + _dot(ah, bl, dims) + _dot(al, bh, dims)


def _dot_exact_rhs(a, b_bf16, dims=_NN):
    ah, al = _split(a)
    return _dot(ah, b_bf16, dims) + _dot(al, b_bf16, dims)


def _sigmoid(x):
    return 1.0 / (1.0 + jnp.exp(-x))


def _softplus(x):
    return jnp.maximum(x, 0.0) + jnp.log(1.0 + jnp.exp(-jnp.abs(x)))


def _params(n_axes):
    return pltpu.CompilerParams(dimension_semantics=("arbitrary",) * n_axes,
                                vmem_limit_bytes=VMEM_LIMIT_BYTES)


def _ada_kernel(c_ref, w_ref, b_ref, o_ref):
    c = c_ref[...]
    o_ref[...] = _dot3(c * _sigmoid(c), w_ref[...]) + b_ref[...]


def _ada(c, w, b):
    bsz, d = c.shape
    n = w.shape[1]
    tn = 1024
    return pl.pallas_call(
        _ada_kernel,
        grid=(n // tn,),
        in_specs=[pl.BlockSpec((bsz, d), lambda j: (0, 0)),
                  pl.BlockSpec((d, tn), lambda j: (0, j)),
                  pl.BlockSpec((1, tn), lambda j: (0, j))],
        out_specs=pl.BlockSpec((bsz, tn), lambda j: (0, j)),
        out_shape=jax.ShapeDtypeStruct((bsz, n), F32),
        compiler_params=_params(1),
        name="ada",
    )(c, w, b.reshape(1, n))


def _proj_kernel(x_ref, g_ref, sc_ref, sh_ref, w1_ref, w2_ref, o1_ref, o2_ref):
    x = x_ref[...]
    ms = jnp.mean(x * x, axis=-1, keepdims=True)
    y = x * lax.rsqrt(ms + NORM_EPS) * g_ref[...]
    h = (y * (1.0 + sc_ref[0]) + sh_ref[0]).astype(BF16)
    o1_ref[...] = _dot(h, w1_ref[...])
    o2_ref[...] = _dot(h, w2_ref[...])


def _proj(x2, g, mod, w1, w2, seq, tm):
    n, d = x2.shape
    tiles_per_seq = seq // tm
    return pl.pallas_call(
        _proj_kernel,
        grid=(n // tm,),
        in_specs=[pl.BlockSpec((tm, d), lambda i: (i, 0)),
                  pl.BlockSpec((1, d), lambda i: (0, 0)),
                  pl.BlockSpec((1, 1, d), lambda i: ((i // tiles_per_seq) * 6 + 1, 0, 0)),
                  pl.BlockSpec((1, 1, d), lambda i: ((i // tiles_per_seq) * 6 + 0, 0, 0)),
                  pl.BlockSpec(w1.shape, lambda i: (0, 0)),
                  pl.BlockSpec(w2.shape, lambda i: (0, 0))],
        out_specs=[pl.BlockSpec((tm, w1.shape[1]), lambda i: (i, 0)),
                   pl.BlockSpec((tm, w2.shape[1]), lambda i: (i, 0))],
        out_shape=[jax.ShapeDtypeStruct((n, w1.shape[1]), F32),
                   jax.ShapeDtypeStruct((n, w2.shape[1]), F32)],
        compiler_params=_params(1),
        name="proj",
    )(x2, g, mod, mod, w1, w2)


def _rwkv_kernel(p_ref, mu_ref, w0_ref, wda_ref, a0_ref, wg_ref, kk_ref, ka_ref, rk_ref,
                 gng_ref, gnb_ref, blk_ref, tril_ref, masks_ref, o_ref, s_ref, prev_ref):
    c = RWKV_CHUNK
    w = RWKV_WIDTH
    tt = p_ref.shape[0]
    chunks = range(tt // c)

    @pl.when(pl.program_id(1) == 0)
    def _():
        s_ref[...] = jnp.zeros_like(s_ref)
        prev_ref[...] = jnp.zeros_like(prev_ref)

    p = p_ref[...]
    row = lax.broadcasted_iota(I32, p.shape, 0)
    p_prev = jnp.where(row == 0, prev_ref[...], pltpu.roll(p, 1, 0))
    prev_ref[...] = p[tt - 1:tt, :]
    ps = p + mu_ref[...] * (p_prev - p)

    r = ps[:, 0:w]
    k = ps[:, w:2 * w]
    v = ps[:, 2 * w:3 * w]
    lora_in = ps[:, 3 * w:3 * w + DECAY_LORA + AAA_LORA]
    gate_in = ps[:, 3 * w + DECAY_LORA + AAA_LORA:]

    lane = lax.broadcasted_iota(I32, lora_in.shape, 1)
    lora_act = jnp.where(lane < DECAY_LORA, jnp.tanh(lora_in), lora_in).astype(BF16)
    da = _dot(lora_act, wda_ref[...])
    log_w = -_softplus(-(w0_ref[...] + da[:, :w])) - 0.5
    ld = -jnp.exp(log_w)
    a = _sigmoid(a0_ref[...] + da[:, w:])
    gate = _dot(_sigmoid(gate_in).astype(BF16), wg_ref[...])

    blk = blk_ref[...]

    def head_sums(z):
        nb = w // blk.shape[0]
        rows = jnp.concatenate([z[:, i * 128:(i + 1) * 128] for i in range(nb)], axis=0)
        sums = _dot(rows.astype(BF16), blk)
        return jnp.concatenate([sums[i * tt:(i + 1) * tt] for i in range(nb)], axis=1)

    kk = k * kk_ref[...]
    kk = kk / jnp.maximum(jnp.sqrt(head_sums(kk * kk)), 1e-12)
    k2 = k * (1.0 + (a - 1.0) * ka_ref[...])

    cs = _dot_exact_rhs_lhs(tril_ref[...], ld)
    cs_last = [cs[(ch + 1) * c - 1:(ch + 1) * c, :] for ch in chunks]
    cs_end = jnp.concatenate([jnp.broadcast_to(cl, (c, w)) for cl in cs_last], axis=0)
    e_neg = jnp.exp(-cs)
    e_rem = jnp.exp(cs_end - cs)
    kka = kk * a
    a_t = (-kk) * jnp.exp(cs - ld)
    b_t = kka * e_neg
    k_t = k2 * e_neg
    r_t = r * jnp.exp(cs)
    b_h = kka * e_rem
    k_h = k2 * e_rem
    w_c = [jnp.exp(cl) for cl in cs_last]

    bd = masks_ref[0]
    strict = masks_ref[1]
    incl = masks_ref[2]
    eye = masks_ref[3]
    gw = GROUP_HEADS * HEAD_DIM
    stack = lambda z: jnp.concatenate([z] * GROUP_HEADS, axis=0)

    groups = range(RWKV_HEADS // GROUP_HEADS)
    units = [(ch, g) for ch in chunks for g in groups]
    piece = lambda z, u: z[u[0] * c:(u[0] + 1) * c, u[1] * gw:(u[1] + 1) * gw]
    v_x = [(stack(piece(v, u)) * bd).astype(BF16) for u in units]
    ar_x = [jnp.concatenate([(stack(piece(a_t, u)) * bd).astype(BF16),
                             (stack(piece(r_t, u)) * bd).astype(BF16)], axis=0) for u in units]
    bk_r = [jnp.concatenate([stack(piece(b_t, u).astype(BF16)), stack(piece(k_t, u).astype(BF16))], axis=0)
            for u in units]
    bk_h = [jnp.concatenate([stack(piece(b_h, u).astype(BF16)), stack(piece(k_h, u).astype(BF16))], axis=0)
            for u in units]
    quad = [_dot(x, y, _NT) for x, y in zip(ar_x, bk_r)]
    l_ab = [q4[:gw, :gw] * strict for q4 in quad]
    l_ak = [(q4[:gw, gw:] * strict).astype(BF16) for q4 in quad]
    m_cat = [jnp.concatenate([(q4[gw:, :gw] * incl).astype(BF16), (q4[gw:, gw:] * incl).astype(BF16)], axis=1)
             for q4 in quad]

    pw = [p.astype(BF16) for p in l_ab]
    t_inv = [eye + l for l in l_ab]
    pw = [_dot(p, p) for p in pw]
    for _ in range(4):
        pb = [p.astype(BF16) for p in pw]
        both = [_dot(jnp.concatenate([t.astype(BF16), p], axis=0), p) for t, p in zip(t_inv, pb)]
        t_inv = [t + tp[:gw] for t, tp in zip(t_inv, both)]
        pw = [tp[gw:] for tp in both]
    t_inv = [(t + _dot(t.astype(BF16), p.astype(BF16))).astype(BF16) for t, p in zip(t_inv, pw)]

    state = [s_ref[g] for g in groups]
    y_rows = []
    for ch in chunks:
        us = [ch * len(groups) + g for g in groups]
        ar_s = [_dot(ar_x[u], state[g].astype(BF16), _NT) for g, u in zip(groups, us)]
        rhs = [ar_s[g][:gw] + _dot(l_ak[u], v_x[u]) for g, u in zip(groups, us)]
        ub = [_dot(t_inv[u], rhs[g].astype(BF16)).astype(BF16) for g, u in zip(groups, us)]
        uv = [jnp.concatenate([ub[g], v_x[u]], axis=0) for g, u in zip(groups, us)]
        y_x = [ar_s[g][gw:] + _dot(m_cat[u], uv[g]) for g, u in zip(groups, us)]
        y_rows.append(jnp.concatenate(
            [sum(yx[hh * c:(hh + 1) * c] for hh in range(1, GROUP_HEADS)) + yx[0:c] for yx in y_x], axis=1))
        state = [(state[g] * w_c[ch][:, g * gw:(g + 1) * gw] + _dot(uv[g], bk_h[u], _TN)) * bd
                 for g, u in zip(groups, us)]
    for g in groups:
        s_ref[g] = state[g]

    y = jnp.concatenate(y_rows, axis=0)
    inv_n = 1.0 / HEAD_DIM
    mean = head_sums(y) * inv_n
    dlt = y - mean
    var = head_sums(dlt * dlt) * inv_n
    yn = dlt * lax.rsqrt(var + GN_EPS) * gng_ref[...] + gnb_ref[...]
    bonus = head_sums(r * k2 * rk_ref[...]) * v
    o_ref[...] = ((yn + bonus) * gate).astype(o_ref.dtype)


def _dot_exact_rhs_lhs(a_bf16, b):
    bh, bl = _split(b)
    return _dot(a_bf16, bh) + _dot(a_bf16, bl)


def _rwkv(p, bsz, seq, mu, w0, wda, a0, wg, k_k, k_a, r_k, gn_g, gn_b):
    c = RWKV_CHUNK
    w = RWKV_WIDTH
    tt = c * RWKV_STEP_CHUNKS if seq % (c * RWKV_STEP_CHUNKS) == 0 else c
    n_steps = seq // tt
    head_of = np.arange(128) // HEAD_DIM
    blk = jnp.asarray(head_of[:, None] == head_of[None, :], BF16)
    ti = np.arange(tt)
    tril = jnp.asarray((ti[:, None] >= ti[None, :]) & (ti[:, None] // c == ti[None, :] // c), BF16)
    assert c == HEAD_DIM and RWKV_HEADS % GROUP_HEADS == 0
    gi = np.arange(GROUP_HEADS * c)
    same = (gi[:, None] // c) == (gi[None, :] // c)
    later = (gi[:, None] % c) > (gi[None, :] % c)
    masks = jnp.asarray(np.stack([same, same & later, same & (later | (gi[:, None] % c == gi[None, :] % c)),
                                  gi[:, None] == gi[None, :]]), F32)
    row = lambda a: a.reshape(1, -1)
    vec = lambda n: pl.BlockSpec((1, n), lambda b, t: (0, 0))
    full = lambda a: pl.BlockSpec(a.shape, lambda b, t: (0, 0))
    return pl.pallas_call(
        _rwkv_kernel,
        grid=(bsz, n_steps),
        in_specs=[pl.BlockSpec((tt, RWKV_COLS), lambda b, t: (b * n_steps + t, 0)),
                  vec(RWKV_COLS), vec(w), full(wda), vec(w), full(wg), vec(w), vec(w), vec(w),
                  vec(w), vec(w), full(blk), full(tril),
                  pl.BlockSpec(masks.shape, lambda b, t: (0, 0, 0))],
        out_specs=pl.BlockSpec((tt, w), lambda b, t: (b * n_steps + t, 0)),
        out_shape=jax.ShapeDtypeStruct((bsz * seq, w), BF16),
        scratch_shapes=[pltpu.VMEM((RWKV_HEADS // GROUP_HEADS, GROUP_HEADS * HEAD_DIM, GROUP_HEADS * HEAD_DIM), F32),
                        pltpu.VMEM((1, RWKV_COLS), F32)],
        compiler_params=_params(2),
        name="rwkv",
    )(p, row(mu), row(w0), wda, row(a0), wg, row(k_k), row(k_a), row(r_k), row(gn_g), row(gn_b),
      blk, tril, masks)


SLAB_ROWS = 64
ATTN_HEAD_GROUP = 8
ONES_ROWS = 16


def _fold_rows(x, op):
    acc = x[0:SLAB_ROWS]
    for r0 in range(SLAB_ROWS, x.shape[0], SLAB_ROWS):
        acc = op(acc, x[r0:r0 + SLAB_ROWS])
    return acc


def _split3(x):
    hi = x.astype(BF16)
    rest = x - hi.astype(F32)
    mid = rest.astype(BF16)
    return hi, mid, (rest - mid.astype(F32)).astype(BF16)


def _dsa_kernel(pd_ref, kvg_ref, kig_ref, wuk_ref, wuv_ref, tril_ref, wsel_ref, o_ref,
                ckv_s, ckvt_s, kcat_s, score_s, *, nq, n_cls, topk):
    qb = Q_BLOCK
    j = pl.program_id(1)

    @pl.when(j == 0)
    def _():
        ckv_s[...] = jnp.zeros_like(ckv_s)
        ckvt_s[...] = jnp.zeros_like(ckvt_s)
        kcat_s[...] = jnp.zeros_like(kcat_s)

    pd = pd_ref[...]
    q = pd[:, :DSA_WIDTH]
    ckv = pd[:, DSA_WIDTH:DSA_WIDTH + KV_LATENT]
    qi = pd[:, DSA_WIDTH + KV_LATENT:DSA_WIDTH + KV_LATENT + IDX_HEADS * IDX_DIM]
    tail = pd[:, DSA_WIDTH + KV_LATENT + IDX_HEADS * IDX_DIM:]
    ki = tail[:, :IDX_DIM]

    ckv_n = ckv * lax.rsqrt(jnp.mean(ckv * ckv, axis=-1, keepdims=True) + NORM_EPS) * kvg_ref[...]
    ki_n = ki * lax.rsqrt(jnp.mean(ki * ki, axis=-1, keepdims=True) + NORM_EPS) * kig_ref[...]
    ki_hi, ki_lo = _split(ki_n)
    rows = pl.ds(pl.multiple_of(j * qb, qb), qb)
    ckv_s[rows, :] = ckv_n.astype(BF16)
    one_row = jnp.where(lax.broadcasted_iota(I32, (ONES_ROWS, qb), 0) == 0, 1.0, 0.0)
    ckvt_s[j] = jnp.concatenate([ckv_n.T, one_row], axis=0).astype(BF16)
    kcat_s[rows, :] = jnp.concatenate([ki_hi, ki_hi, ki_lo, jnp.zeros_like(ki_hi)], axis=1)

    w_t = sum(_dot(wsel_ref[...], part, _NT) for part in _split3(tail))

    per = nq // n_cls
    for cls in range(n_cls):
        @pl.when((j >= cls * per) & (j < (cls + 1) * per))
        def _(sk=(cls + 1) * per * qb):
            _dsa_block(j, q, qi, w_t, wuk_ref, wuv_ref, tril_ref, o_ref, ckv_s, ckvt_s, kcat_s,
                       score_s, sk=sk, topk=topk)


def _dsa_block(j, q, qi, w_t, wuk_ref, wuv_ref, tril_ref, o_ref, ckv_s, ckvt_s, kcat_s, score_s,
               *, sk, topk):
    qb = Q_BLOCK
    kcat = kcat_s[0:sk, :]
    score = jnp.zeros((sk, qb), F32)
    w_s = w_t * (IDX_HEADS ** -0.5)
    for hh in range(IDX_HEADS):
        q_hi, q_lo = _split(qi[:, hh * IDX_DIM:(hh + 1) * IDX_DIM] * (IDX_DIM ** -0.5))
        dots = _dot(kcat, jnp.concatenate([q_hi, q_lo, q_hi, jnp.zeros_like(q_hi)], axis=1), _NT)
        score = score + w_s[hh:hh + 1, :] * jnp.maximum(dots, 0.0)

    key_pos = lax.broadcasted_iota(I32, (sk, qb), 0)
    q_pos = j * qb + lax.broadcasted_iota(I32, (sk, qb), 1)
    adm = jnp.right_shift(key_pos, ATTN_CHUNK_LOG2) <= jnp.right_shift(q_pos, ATTN_CHUNK_LOG2)

    score_s[0:sk, :] = jnp.where(adm, score, NEG_BIG)

    def as_float(okey):
        return lax.bitcast_convert_type(okey ^ ((okey >> 31) & 0x7FFFFFFF), F32)

    def descend(i, tu):
        cand_u = tu | jnp.left_shift(jnp.int32(1), 31 - i)
        cand = as_float(cand_u ^ INT_MIN)
        acc = jnp.zeros((SLAB_ROWS, qb), F32)
        for r0 in range(0, sk, SLAB_ROWS):
            acc = acc + jnp.where(score_s[r0:r0 + SLAB_ROWS, :] >= cand, 1.0, 0.0)
        cnt = jnp.sum(acc, axis=0, keepdims=True)
        return jnp.where(cnt >= topk, cand_u, tu)

    thr = as_float(lax.fori_loop(0, 32, descend, jnp.zeros((1, qb), I32)) ^ INT_MIN)

    score = score_s[0:sk, :]
    gt = score > thr
    eq = score == thr
    need = topk - jnp.sum(_fold_rows(jnp.where(gt, 1.0, 0.0), jnp.add), axis=0, keepdims=True)
    eq_b = jnp.where(eq, 1.0, 0.0).astype(BF16)
    tb = tril_ref.shape[0]
    off = jnp.zeros((1, qb), F32)
    pieces = []
    for kb in range(sk // tb):
        pre = _dot(tril_ref[...], eq_b[kb * tb:(kb + 1) * tb, :])
        pieces.append(pre + off)
        off = off + pre[tb - 1:tb, :]
    prefix = jnp.concatenate(pieces, axis=0)
    tie_bias = jnp.where(prefix <= need, 0.0, NEG_BIG)
    bias = jnp.where(gt, 0.0, jnp.where(eq, tie_bias, NEG_BIG))
    bias = jnp.where(adm, bias, NEG_BIG)

    ckv_all = ckv_s[0:sk, :]
    ckv_t = jnp.concatenate([ckvt_s[b] for b in range(sk // qb)], axis=1)
    q_lat = (_dot(q.astype(BF16), wuk_ref[...]) * (HEAD_DIM ** -0.5)).astype(BF16)
    head_logits = lambda h: _dot(ckv_all, q_lat[:, h * KV_LATENT:(h + 1) * KV_LATENT], _NT)
    outs = []
    nxt = head_logits(0)
    for h in range(DSA_HEADS):
        logits = nxt + bias
        if h + 1 < DSA_HEADS:
            nxt = head_logits(h + 1)
        mx = jnp.max(_fold_rows(logits, jnp.maximum), axis=0, keepdims=True)
        pr = jnp.exp((logits - mx).astype(BF16))
        acc = _dot(ckv_t, pr)
        outs.append((acc[:KV_LATENT] / acc[KV_LATENT:KV_LATENT + 1]).astype(BF16))
    o_lat_t = jnp.concatenate(outs, axis=0)
    o_ref[...] = _dot(o_lat_t, wuv_ref[...], _TN).astype(o_ref.dtype)


def _dsa(pd, bsz, seq, kv_g, ki_g, w_uk, w_uv):
    qb = Q_BLOCK
    nq = seq // qb
    topk = min(TOPK_MAX, seq // 4)
    hc = DSA_HEADS * KV_LATENT
    head_of_q = np.arange(DSA_WIDTH) // HEAD_DIM
    head_of_l = np.arange(hc) // KV_LATENT
    mask = jnp.asarray(head_of_q[:, None] == head_of_l[None, :], F32)
    uk = jnp.transpose(w_uk, (1, 2, 0)).reshape(DSA_WIDTH, KV_LATENT)
    wuk_bd = (jnp.tile(uk, (1, DSA_HEADS)) * mask).astype(BF16)
    uv = w_uv.reshape(KV_LATENT, DSA_WIDTH)
    wuv_bd = (jnp.tile(uv, (DSA_HEADS, 1)) * mask.T).astype(BF16)
    n_cls = max(d for d in (8, 4, 2, 1) if nq % d == 0)
    tb = 256 if ((nq // n_cls) * qb) % 256 == 0 else 128
    tril = jnp.asarray(np.tril(np.ones((tb, tb), np.float32)), BF16)
    wsel = jnp.asarray(np.arange(128)[None, :] == (IDX_DIM + np.arange(8))[:, None], BF16)
    wsel = wsel * jnp.asarray(np.arange(8)[:, None] < IDX_HEADS, BF16)
    full = lambda a: pl.BlockSpec(a.shape, lambda b, j: (0, 0))
    return pl.pallas_call(
        functools.partial(_dsa_kernel, nq=nq, n_cls=n_cls, topk=topk),
        grid=(bsz, nq),
        in_specs=[pl.BlockSpec((qb, DSA_COLS_PAD), lambda b, j: (b * nq + j, 0)),
                  pl.BlockSpec((1, KV_LATENT), lambda b, j: (0, 0)),
                  pl.BlockSpec((1, IDX_DIM), lambda b, j: (0, 0)),
                  full(wuk_bd), full(wuv_bd), full(tril), full(wsel)],
        out_specs=pl.BlockSpec((qb, DSA_WIDTH), lambda b, j: (b * nq + j, 0)),
        out_shape=jax.ShapeDtypeStruct((bsz * seq, DSA_WIDTH), BF16),
        scratch_shapes=[pltpu.VMEM((seq, KV_LATENT), BF16),
                        pltpu.VMEM((nq, KV_LATENT + ONES_ROWS, qb), BF16),
                        pltpu.VMEM((seq, 4 * IDX_DIM), BF16),
                        pltpu.VMEM((seq, qb), F32)],
        compiler_params=_params(2),
        name="dsa",
    )(pd, kv_g.reshape(1, -1), ki_g.reshape(1, -1), wuk_bd, wuv_bd, tril, wsel)


def _dsa_flash_kernel(pd_ref, kvg_ref, kig_ref, wuk_ref, wuv_ref, tril_ref, wsel_ref, o_ref,
                      ckv_s, ckvt_s, kcat_s, score_s, bias_s, acc_s, m_s, *, topk):
    qb = Q_BLOCK
    kb_rows = Q_BLOCK
    j = pl.program_id(1)
    n_kb = j + 1

    pd = pd_ref[...]
    q = pd[:, :DSA_WIDTH]
    ckv = pd[:, DSA_WIDTH:DSA_WIDTH + KV_LATENT]
    qi = pd[:, DSA_WIDTH + KV_LATENT:DSA_WIDTH + KV_LATENT + IDX_HEADS * IDX_DIM]
    tail = pd[:, DSA_WIDTH + KV_LATENT + IDX_HEADS * IDX_DIM:]
    ki = tail[:, :IDX_DIM]

    ckv_n = ckv * lax.rsqrt(jnp.mean(ckv * ckv, axis=-1, keepdims=True) + NORM_EPS) * kvg_ref[...]
    ki_n = ki * lax.rsqrt(jnp.mean(ki * ki, axis=-1, keepdims=True) + NORM_EPS) * kig_ref[...]
    ki_hi, ki_lo = _split(ki_n)
    rows = pl.ds(pl.multiple_of(j * qb, qb), qb)
    ckv_s[rows, :] = ckv_n.astype(BF16)
    one_row = jnp.where(lax.broadcasted_iota(I32, (ONES_ROWS, qb), 0) == 0, 1.0, 0.0)
    ckvt_s[j] = jnp.concatenate([ckv_n.T, one_row], axis=0).astype(BF16)
    kcat_s[rows, :] = jnp.concatenate([ki_hi, ki_hi, ki_lo, jnp.zeros_like(ki_hi)], axis=1)

    w_s = sum(_dot(wsel_ref[...], part, _NT) for part in _split3(tail)) * (IDX_HEADS ** -0.5)
    q_cat = []
    for hh in range(IDX_HEADS):
        q_hi, q_lo = _split(qi[:, hh * IDX_DIM:(hh + 1) * IDX_DIM] * (IDX_DIM ** -0.5))
        q_cat.append(jnp.concatenate([q_hi, q_lo, q_hi, jnp.zeros_like(q_hi)], axis=1))

    def key_rows(kb):
        return pl.ds(pl.multiple_of(kb * kb_rows, kb_rows), kb_rows)

    def admissible(kb):
        key_pos = kb * kb_rows + lax.broadcasted_iota(I32, (kb_rows, qb), 0)
        q_pos = j * qb + lax.broadcasted_iota(I32, (kb_rows, qb), 1)
        return jnp.right_shift(key_pos, ATTN_CHUNK_LOG2) <= jnp.right_shift(q_pos, ATTN_CHUNK_LOG2)

    def score_block(kb, carry):
        kcat = kcat_s[key_rows(kb), :]
        dots = [_dot(kcat, q_cat[hh], _NT) for hh in range(IDX_HEADS)]
        score = sum(w_s[hh:hh + 1, :] * jnp.maximum(dots[hh], 0.0) for hh in range(1, IDX_HEADS)) \
            + w_s[0:1, :] * jnp.maximum(dots[0], 0.0)
        score_s[key_rows(kb), :] = jnp.where(admissible(kb), score, NEG_BIG)
        return carry

    lax.fori_loop(0, n_kb, score_block, 0)

    def as_float(okey):
        return lax.bitcast_convert_type(okey ^ ((okey >> 31) & 0x7FFFFFFF), F32)

    def count_where(pred):
        def block(kb, acc):
            base = kb * kb_rows
            for r0 in range(0, kb_rows, SLAB_ROWS):
                slab = score_s[pl.ds(pl.multiple_of(base + r0, SLAB_ROWS), SLAB_ROWS), :]
                acc = acc + jnp.where(pred(slab), 1.0, 0.0)
            return acc
        acc = lax.fori_loop(0, n_kb, block, jnp.zeros((SLAB_ROWS, qb), F32))
        return jnp.sum(acc, axis=0, keepdims=True)

    def descend(i, tu):
        cand_u = tu | jnp.left_shift(jnp.int32(1), 31 - i)
        cand = as_float(cand_u ^ INT_MIN)
        return jnp.where(count_where(lambda s: s >= cand) >= topk, cand_u, tu)

    few_keys = n_kb * kb_rows <= topk
    n_bits = jnp.where(few_keys, 0, 32)
    thr = as_float(lax.fori_loop(0, n_bits, descend, jnp.zeros((1, qb), I32)) ^ INT_MIN)
    thr = jnp.where(few_keys, NEG_BIG, thr)
    need = topk - count_where(lambda s: s > thr)

    def bias_block(kb, tie_count):
        score = score_s[key_rows(kb), :]
        eq = score == thr
        prefix = _dot(tril_ref[...], jnp.where(eq, 1.0, 0.0).astype(BF16)) + tie_count
        tie_bias = jnp.where(eq, jnp.where(prefix <= need, 0.0, NEG_BIG), NEG_BIG)
        bias_s[key_rows(kb), :] = jnp.where(admissible(kb), jnp.where(score > thr, 0.0, tie_bias), NEG_BIG)
        return prefix[kb_rows - 1:kb_rows, :]

    lax.fori_loop(0, n_kb, bias_block, jnp.zeros((1, qb), F32))

    q_lat = (_dot(q.astype(BF16), wuk_ref[...]) * (HEAD_DIM ** -0.5)).astype(BF16)
    acc_s[...] = jnp.zeros_like(acc_s)
    m_s[...] = jnp.full(m_s.shape, NEG_BIG, F32)

    def attend_block(kb, carry):
        ckv_kb = ckv_s[key_rows(kb), :]
        ckvt_kb = ckvt_s[kb]
        bias = bias_s[key_rows(kb), :]
        for h0 in range(0, DSA_HEADS, ATTN_HEAD_GROUP):
            hs = range(h0, h0 + ATTN_HEAD_GROUP)
            logits = [_dot(ckv_kb, q_lat[:, h * KV_LATENT:(h + 1) * KV_LATENT], _NT) + bias for h in hs]
            m_old = [m_s[h] for h in hs]
            m_new = [jnp.maximum(mo, jnp.max(_fold_rows(lg, jnp.maximum), axis=0, keepdims=True))
                     for mo, lg in zip(m_old, logits)]
            pr = [jnp.exp((lg - mn).astype(BF16)) for lg, mn in zip(logits, m_new)]
            for i, h in enumerate(hs):
                acc_s[h] = acc_s[h] * jnp.exp(m_old[i] - m_new[i]) + _dot(ckvt_kb, pr[i])
                m_s[h] = m_new[i]
        return carry

    lax.fori_loop(0, n_kb, attend_block, 0)

    o_lat_t = jnp.concatenate(
        [(acc_s[h][:KV_LATENT] / acc_s[h][KV_LATENT:KV_LATENT + 1]).astype(BF16) for h in range(DSA_HEADS)],
        axis=0)
    o_ref[...] = _dot(o_lat_t, wuv_ref[...], _TN).astype(o_ref.dtype)


def _dsa_flash(pd, bsz, seq, kv_g, ki_g, w_uk, w_uv):
    qb = min(Q_BLOCK, seq)
    assert qb == Q_BLOCK and seq % qb == 0
    nq = seq // qb
    topk = min(TOPK_MAX, seq // 4)
    hc = DSA_HEADS * KV_LATENT
    head_of_q = np.arange(DSA_WIDTH) // HEAD_DIM
    head_of_l = np.arange(hc) // KV_LATENT
    mask = jnp.asarray(head_of_q[:, None] == head_of_l[None, :], F32)
    uk = jnp.transpose(w_uk, (1, 2, 0)).reshape(DSA_WIDTH, KV_LATENT)
    wuk_bd = (jnp.tile(uk, (1, DSA_HEADS)) * mask).astype(BF16)
    uv = w_uv.reshape(KV_LATENT, DSA_WIDTH)
    wuv_bd = (jnp.tile(uv, (DSA_HEADS, 1)) * mask.T).astype(BF16)
    tril = jnp.asarray(np.tril(np.ones((qb, qb), np.float32)), BF16)
    wsel = jnp.asarray((np.arange(128)[None, :] == (IDX_DIM + np.arange(8))[:, None])
                       & (np.arange(8)[:, None] < IDX_HEADS), BF16)
    full = lambda a: pl.BlockSpec(a.shape, lambda b, j: (0, 0))
    return pl.pallas_call(
        functools.partial(_dsa_flash_kernel, topk=topk),
        grid=(bsz, nq),
        in_specs=[pl.BlockSpec((qb, DSA_COLS_PAD), lambda b, j: (b * nq + j, 0)),
                  pl.BlockSpec((1, KV_LATENT), lambda b, j: (0, 0)),
                  pl.BlockSpec((1, IDX_DIM), lambda b, j: (0, 0)),
                  full(wuk_bd), full(wuv_bd), full(tril), full(wsel)],
        out_specs=pl.BlockSpec((qb, DSA_WIDTH), lambda b, j: (b * nq + j, 0)),
        out_shape=jax.ShapeDtypeStruct((bsz * seq, DSA_WIDTH), BF16),
        scratch_shapes=[pltpu.VMEM((seq, KV_LATENT), BF16),
                        pltpu.VMEM((nq, KV_LATENT + ONES_ROWS, qb), BF16),
                        pltpu.VMEM((seq, 4 * IDX_DIM), BF16),
                        pltpu.VMEM((seq, qb), F32),
                        pltpu.VMEM((seq, qb), F32),
                        pltpu.VMEM((DSA_HEADS, KV_LATENT + ONES_ROWS, qb), F32),
                        pltpu.VMEM((DSA_HEADS, 1, qb), F32)],
        compiler_params=_params(2),
        name="dsa",
    )(pd, kv_g.reshape(1, -1), ki_g.reshape(1, -1), wuk_bd, wuv_bd, tril, wsel)


def _mix_kernel(yr_ref, yd_ref, x_ref, wo1_ref, wo2_ref, g1_ref, n2_ref, sc_ref, sh_ref,
                wr_ref, br_ref, x1_ref, h2_ref, ids_ref, wts_ref):
    mixed = _dot(yr_ref[...], wo1_ref[...]) + _dot(yd_ref[...], wo2_ref[...])
    x1 = x_ref[...] + (1.0 + g1_ref[0]) * mixed
    x1_ref[...] = x1
    ms = jnp.mean(x1 * x1, axis=-1, keepdims=True)
    h2 = x1 * lax.rsqrt(ms + NORM_EPS) * n2_ref[...] * (1.0 + sc_ref[0]) + sh_ref[0]
    h2_ref[...] = h2

    lt = _dot3(wr_ref[...], h2, _NT) + br_ref[...]
    gl = lt[0:N_GROUPS]
    el = lt[8:8 + N_EXPERTS]
    tm = gl.shape[1]
    gmax = jnp.max(gl, axis=0, keepdims=True)
    gidx = lax.broadcasted_iota(I32, (N_GROUPS, tm), 0)
    gsel = jnp.min(jnp.where(gl == gmax, gidx, N_GROUPS), axis=0, keepdims=True)
    p_group = 1.0 / jnp.sum(jnp.exp(gl - gmax), axis=0, keepdims=True)
    eidx = lax.broadcasted_iota(I32, (N_EXPERTS, tm), 0)
    el = jnp.where(jnp.right_shift(eidx, 3) == gsel, el, NEG_BIG)
    m1 = jnp.max(el, axis=0, keepdims=True)
    i1 = jnp.min(jnp.where(el == m1, eidx, N_EXPERTS), axis=0, keepdims=True)
    el2 = jnp.where(eidx == i1, NEG_BIG, el)
    m2 = jnp.max(el2, axis=0, keepdims=True)
    i2 = jnp.min(jnp.where(el2 == m2, eidx, N_EXPERTS), axis=0, keepdims=True)
    e2 = jnp.exp(m2 - m1)
    w1 = p_group / (1.0 + e2)
    ids_ref[0] = jnp.concatenate([i1, i2], axis=0)
    wts_ref[0] = jnp.concatenate([w1, w1 * e2], axis=0)


def _mix(yr, yd, x2, wo1, wo2, mod, n2g, wr, br, seq, tm):
    n, d = x2.shape
    nt = n // tm
    tps = seq // tm
    modspec = lambda k: pl.BlockSpec((1, 1, d), lambda i: ((i // tps) * 6 + k, 0, 0))
    full = lambda a: pl.BlockSpec(a.shape, lambda i: (0,) * a.ndim)
    return pl.pallas_call(
        _mix_kernel,
        grid=(nt,),
        in_specs=[pl.BlockSpec((tm, yr.shape[1]), lambda i: (i, 0)),
                  pl.BlockSpec((tm, yd.shape[1]), lambda i: (i, 0)),
                  pl.BlockSpec((tm, d), lambda i: (i, 0)),
                  full(wo1), full(wo2), modspec(2), full(n2g), modspec(4), modspec(3),
                  full(wr), full(br)],
        out_specs=[pl.BlockSpec((tm, d), lambda i: (i, 0)),
                   pl.BlockSpec((tm, d), lambda i: (i, 0)),
                   pl.BlockSpec((1, 2, tm), lambda i: (i, 0, 0)),
                   pl.BlockSpec((1, 2, tm), lambda i: (i, 0, 0))],
        out_shape=[jax.ShapeDtypeStruct((n, d), F32),
                   jax.ShapeDtypeStruct((n, d), F32),
                   jax.ShapeDtypeStruct((nt, 2, tm), I32),
                   jax.ShapeDtypeStruct((nt, 2, tm), F32)],
        compiler_params=_params(1),
        name="mix",
    )(yr, yd, x2, wo1, wo2, mod, n2g, mod, mod, wr, br)


def _sort_kernel(ids_ref, triu_ref, tril_ref, dest_ref, bexp_ref, eend_ref,
                 cnt_ref, run_ref, start_ref, *, n_blocks_pad):
    phase = pl.program_id(0)
    i = pl.program_id(1)
    ids = ids_ref[0]
    tm = ids.shape[1]
    eidx = lax.broadcasted_iota(I32, (N_EXPERTS, tm), 0)
    hit0 = eidx == ids[0:1]
    hit1 = eidx == ids[1:2]
    onehot = jnp.where(hit0, 1.0, 0.0) + jnp.where(hit1, 1.0, 0.0)

    @pl.when((phase == 0) & (i == 0))
    def _():
        cnt_ref[...] = jnp.zeros_like(cnt_ref)

    @pl.when(phase == 0)
    def _():
        cnt_ref[...] += jnp.sum(onehot, axis=1, keepdims=True)

    @pl.when((phase == 1) & (i == 0))
    def _():
        run_ref[...] = jnp.zeros_like(run_ref)
        nblk = jnp.floor((cnt_ref[...] + (EXPERT_BLOCK - 1)) * (1.0 / EXPERT_BLOCK))
        nblk_b = jnp.broadcast_to(nblk, (N_EXPERTS, 128))
        first_blk = _dot_exact_rhs_lhs(tril_ref[...], nblk_b)
        start_ref[...] = first_blk[:, 0:1] * EXPERT_BLOCK
        end_blk = first_blk + nblk_b
        bidx = lax.broadcasted_iota(I32, (N_EXPERTS, n_blocks_pad), 1).astype(F32)
        owner = jnp.sum(jnp.where(end_blk[:, 0:1] <= bidx, 1.0, 0.0), axis=0, keepdims=True)
        bexp_ref[...] = jnp.minimum(owner, N_EXPERTS - 1).astype(I32)
        on_diag = (lax.broadcasted_iota(I32, (N_EXPERTS, 128), 0)
                   == lax.broadcasted_iota(I32, (N_EXPERTS, 128), 1))
        eend_ref[...] = jnp.sum(jnp.where(on_diag, end_blk, 0.0), axis=0, keepdims=True).astype(I32)

    @pl.when(phase == 1)
    def _():
        before = _dot(onehot.astype(BF16), triu_ref[...])
        pos = start_ref[...] + run_ref[...] + before
        d0 = jnp.sum(jnp.where(hit0, pos, 0.0), axis=0, keepdims=True)
        d1 = jnp.sum(jnp.where(hit1, pos, 0.0), axis=0, keepdims=True)
        dest_ref[0] = jnp.concatenate([d0, d1], axis=0).astype(I32)
        run_ref[...] += jnp.sum(onehot, axis=1, keepdims=True)


def _sort(ids, n_blocks_pad):
    nt, _, tm = ids.shape
    triu = jnp.asarray(np.triu(np.ones((tm, tm), np.float32), 1), BF16)
    tril = jnp.asarray(np.tril(np.ones((N_EXPERTS, N_EXPERTS), np.float32), -1), BF16)
    return pl.pallas_call(
        functools.partial(_sort_kernel, n_blocks_pad=n_blocks_pad),
        grid=(2, nt),
        in_specs=[pl.BlockSpec((1, 2, tm), lambda p, i: (i, 0, 0)),
                  pl.BlockSpec(triu.shape, lambda p, i: (0, 0)),
                  pl.BlockSpec(tril.shape, lambda p, i: (0, 0))],
        out_specs=[pl.BlockSpec((1, 2, tm), lambda p, i: (i * p, 0, 0)),
                   pl.BlockSpec((1, n_blocks_pad), lambda p, i: (0, 0)),
                   pl.BlockSpec((1, 128), lambda p, i: (0, 0))],
        out_shape=[jax.ShapeDtypeStruct((nt, 2, tm), I32),
                   jax.ShapeDtypeStruct((1, n_blocks_pad), I32),
                   jax.ShapeDtypeStruct((1, 128), I32)],
        scratch_shapes=[pltpu.VMEM((N_EXPERTS, 1), F32),
                        pltpu.VMEM((N_EXPERTS, 1), F32),
                        pltpu.VMEM((N_EXPERTS, 1), F32)],
        compiler_params=_params(2),
        name="sort",
    )(ids, triu, tril)


def _row_copy(src_ref, src_row, dst_ref, dst_row, sem):
    return pltpu.make_async_copy(src_ref.at[pl.ds(src_row, 1), :], dst_ref.at[pl.ds(dst_row, 1), :], sem)


ROW_UNROLL = 8


def _dispatch_kernel(dest_ref, eend_ref, h_ref, xs_ref, zbuf, sem, zsem):
    tm = h_ref.shape[0]

    @pl.when(pl.program_id(0) == 0)
    def _():
        zbuf[...] = jnp.zeros_like(zbuf)

        def last_block_copy(e):
            end_b = eend_ref[0, e]
            begin_b = jnp.where(e == 0, 0, eend_ref[0, jnp.maximum(e - 1, 0)])
            row0 = pl.multiple_of((end_b - 1) * EXPERT_BLOCK, EXPERT_BLOCK)
            return end_b > begin_b, pltpu.make_async_copy(zbuf, xs_ref.at[pl.ds(row0, EXPERT_BLOCK), :], zsem)

        def fill(e, carry):
            owns_rows, cp = last_block_copy(e)
            pl.when(owns_rows)(cp.start)
            return carry

        def drain(e, carry):
            owns_rows, cp = last_block_copy(e)
            pl.when(owns_rows)(cp.wait)
            return carry

        lax.fori_loop(0, N_EXPERTS, fill, 0)
        lax.fori_loop(0, N_EXPERTS, drain, 0)

        def spare_block_copy(b):
            row0 = pl.multiple_of(b * EXPERT_BLOCK, EXPERT_BLOCK)
            return pltpu.make_async_copy(zbuf, xs_ref.at[pl.ds(row0, EXPERT_BLOCK), :], zsem)

        n_used = eend_ref[0, N_EXPERTS - 1]
        n_blocks = xs_ref.shape[0] // EXPERT_BLOCK
        lax.fori_loop(n_used, n_blocks, lambda b, carry: (spare_block_copy(b).start(), carry)[1], 0)
        lax.fori_loop(n_used, n_blocks, lambda b, carry: (spare_block_copy(b).wait(), carry)[1], 0)

    def start(g, carry):
        for u in range(ROW_UNROLL):
            r = g * ROW_UNROLL + u
            _row_copy(h_ref, r, xs_ref, dest_ref[0, 0, r], sem).start(priority=u % 2)
            _row_copy(h_ref, r, xs_ref, dest_ref[0, 1, r], sem).start(priority=(u + 1) % 2)
        return carry

    lax.fori_loop(0, tm // ROW_UNROLL, start, 0)
    all_rows = pltpu.make_async_copy(h_ref, xs_ref.at[pl.ds(0, tm), :], sem)
    all_rows.wait()
    all_rows.wait()


def _dispatch(dest, eend, h2, n_rows):
    nt, _, tm = dest.shape
    n, d = h2.shape
    return pl.pallas_call(
        _dispatch_kernel,
        grid=(nt,),
        in_specs=[pl.BlockSpec((1, 2, tm), lambda i: (i, 0, 0), memory_space=pltpu.SMEM),
                  pl.BlockSpec(eend.shape, lambda i: (0, 0), memory_space=pltpu.SMEM),
                  pl.BlockSpec((tm, d), lambda i: (i, 0))],
        out_specs=pl.BlockSpec(memory_space=pl.ANY),
        out_shape=jax.ShapeDtypeStruct((n_rows, d), F32),
        scratch_shapes=[pltpu.VMEM((EXPERT_BLOCK, d), F32),
                        pltpu.SemaphoreType.DMA(()), pltpu.SemaphoreType.DMA(())],
        compiler_params=_params(1),
        name="dispatch",
    )(dest, eend, h2)


def _expert_kernel(bexp_ref, eend_ref, xs_ref, wg_ref, wu_ref, wd_ref, ys_ref, wg_b, wu_b, wd_b):
    i = pl.program_id(0)
    used = i < eend_ref[N_EXPERTS - 1]
    new_expert = jnp.logical_or(i == 0, bexp_ref[i] != bexp_ref[jnp.maximum(i - 1, 0)])

    @pl.when(jnp.logical_and(used, new_expert))
    def _():
        wg_b[...] = wg_ref[0].astype(BF16)
        wu_b[...] = wu_ref[0].astype(BF16)
        wd_b[...] = wd_ref[0].astype(BF16)

    @pl.when(used)
    def _():
        x = xs_ref[...].astype(BF16)
        hg = _dot(x, wg_b[...])
        hu = _dot(x, wu_b[...])
        hid = (hg * _sigmoid(hg) * hu).astype(BF16)
        ys_ref[...] = _dot(hid, wd_b[...])

    @pl.when(jnp.logical_not(used))
    def _():
        ys_ref[...] = jnp.zeros_like(ys_ref)


def _experts(bexp, eend, xs, e_gate, e_up, e_down, n_blocks):
    n_rows, d = xs.shape
    de = e_gate.shape[2]
    blk = EXPERT_BLOCK
    last = lambda i, nu: jnp.minimum(i, nu[N_EXPERTS - 1] - 1)
    rows = lambda i, be, nu: (last(i, nu), 0)
    wsel = lambda i, be, nu: (be[last(i, nu)], 0, 0)
    return pl.pallas_call(
        _expert_kernel,
        grid_spec=pltpu.PrefetchScalarGridSpec(
            num_scalar_prefetch=2,
            grid=(n_blocks,),
            in_specs=[pl.BlockSpec((blk, d), rows),
                      pl.BlockSpec((1, d, de), wsel),
                      pl.BlockSpec((1, d, de), wsel),
                      pl.BlockSpec((1, de, d), wsel)],
            out_specs=pl.BlockSpec((blk, d), lambda i, be, nu: (i, 0)),
            scratch_shapes=[pltpu.VMEM((d, de), BF16), pltpu.VMEM((d, de), BF16),
                            pltpu.VMEM((de, d), BF16)]),
        out_shape=jax.ShapeDtypeStruct((n_rows, d), F32),
        compiler_params=_params(1),
        name="experts",
    )(bexp, eend, xs, e_gate, e_up, e_down)


def _combine_kernel(dest_ref, dnext_ref, ys_ref, wts_ref, x1_ref, g2_ref, gf_ref, scf_ref, shf_ref, o_ref,
                    buf, sems):
    tm = x1_ref.shape[0]
    i = pl.program_id(0)
    cur = lax.rem(i, 2)

    def gather(d_ref, s):
        def start(g, carry):
            for u in range(ROW_UNROLL):
                r = g * ROW_UNROLL + u
                _row_copy(ys_ref, d_ref[0, 0, r], buf.at[s, 0], r, sems.at[s]).start(priority=u % 2)
                _row_copy(ys_ref, d_ref[0, 1, r], buf.at[s, 1], r, sems.at[s]).start(priority=(u + 1) % 2)
            return carry

        lax.fori_loop(0, tm // ROW_UNROLL, start, 0)

    @pl.when(i == 0)
    def _():
        gather(dest_ref, 0)

    @pl.when(i + 1 < pl.num_programs(0))
    def _():
        gather(dnext_ref, 1 - cur)

    all_rows = pltpu.make_async_copy(ys_ref.at[pl.ds(0, tm), :], buf.at[cur, 0], sems.at[cur])
    all_rows.wait()
    all_rows.wait()
    wts = wts_ref[...]
    moe = buf[cur, 0] * wts[:, 0:1] + buf[cur, 1] * wts[:, 1:2]
    x2 = x1_ref[...] + (1.0 + g2_ref[0]) * moe
    ms = jnp.mean(x2 * x2, axis=-1, keepdims=True)
    y = x2 * lax.rsqrt(ms + NORM_EPS) * gf_ref[...]
    o_ref[...] = y * (1.0 + scf_ref[0]) + shf_ref[0]


def _combine(dest, ys, wts_col, x1, mod, modf, gf, seq):
    nt, _, tm = dest.shape
    n, d = x1.shape
    tps = seq // tm
    return pl.pallas_call(
        _combine_kernel,
        grid=(nt,),
        in_specs=[pl.BlockSpec((1, 2, tm), lambda i: (i, 0, 0), memory_space=pltpu.SMEM),
                  pl.BlockSpec((1, 2, tm), lambda i: (jnp.minimum(i + 1, nt - 1), 0, 0),
                               memory_space=pltpu.SMEM),
                  pl.BlockSpec(memory_space=pl.ANY),
                  pl.BlockSpec((tm, 2), lambda i: (i, 0)),
                  pl.BlockSpec((tm, d), lambda i: (i, 0)),
                  pl.BlockSpec((1, 1, d), lambda i: ((i // tps) * 6 + 5, 0, 0)),
                  pl.BlockSpec((1, d), lambda i: (0, 0)),
                  pl.BlockSpec((1, 1, d), lambda i: ((i // tps) * 2 + 1, 0, 0)),
                  pl.BlockSpec((1, 1, d), lambda i: ((i // tps) * 2 + 0, 0, 0))],
        out_specs=pl.BlockSpec((tm, d), lambda i: (i, 0)),
        out_shape=jax.ShapeDtypeStruct((n, d), F32),
        scratch_shapes=[pltpu.VMEM((2, 2, tm, d), F32), pltpu.SemaphoreType.DMA((2,))],
        compiler_params=_params(1),
        name="combine",
    )(dest, dest, ys, wts_col, x1, mod, gf, modf, modf)


def kernel(x, c, ada_w, ada_b, norm1_g, w_in, shift_mu, w0, w_decay_up, a0, w_aaa_up, w_gate_up,
           k_k, k_a, r_k, gn_g, gn_b, kv_norm_g, k_idx_norm_g, w_uk, w_uv, w_out, norm2_g,
           w_group, b_group, w_expert, b_expert, e_gate, e_up, e_down,
           final_ada_w, final_ada_b, final_norm_g):
    bsz, seq, d = x.shape
    n = bsz * seq
    depth = ada_w.shape[0]
    tm = min(TOKEN_TILE, seq)
    x2 = x.reshape(n, d)

    modf = _ada(c, final_ada_w, final_ada_b).reshape(bsz * 2, 1, d)
    for l in range(depth):
        mod = _ada(c, ada_w[l], ada_b[l]).reshape(bsz * 6, 1, d)

        w1 = w_in[l][:, :RWKV_COLS].astype(BF16)
        w2 = jnp.pad(w_in[l][:, RWKV_COLS:], ((0, 0), (0, DSA_COLS_PAD - DSA_COLS))).astype(BF16)
        p_rwkv, p_dsa = _proj(x2, norm1_g[l].reshape(1, d), mod, w1, w2, seq, tm)

        zeros = jnp.zeros((DECAY_LORA, RWKV_WIDTH), F32)
        wda = jnp.concatenate([jnp.concatenate([w_decay_up[l], zeros], axis=1),
                               jnp.concatenate([zeros, w_aaa_up[l]], axis=1)], axis=0).astype(BF16)
        y_rwkv = _rwkv(p_rwkv, bsz, seq, shift_mu[l], w0[l], wda, a0[l], w_gate_up[l].astype(BF16),
                       k_k[l], k_a[l], r_k[l], gn_g[l], gn_b[l])
        y_dsa = _dsa_flash(p_dsa, bsz, seq, kv_norm_g[l], k_idx_norm_g[l], w_uk[l], w_uv[l])

        wr = jnp.zeros((ROUTER_ROWS, d), F32)
        wr = wr.at[0:N_GROUPS].set(w_group[l].T).at[8:8 + N_EXPERTS].set(w_expert[l].T)
        br = jnp.zeros((ROUTER_ROWS, 1), F32)
        br = br.at[0:N_GROUPS, 0].set(b_group[l]).at[8:8 + N_EXPERTS, 0].set(b_expert[l])
        wo = w_out[l].astype(BF16)
        x1, h2, ids, wts = _mix(y_rwkv, y_dsa, x2, wo[:RWKV_WIDTH], wo[RWKV_WIDTH:], mod,
                                norm2_g[l].reshape(1, d), wr, br, seq, tm)

        n_blocks = (n * 2) // EXPERT_BLOCK + N_EXPERTS
        n_blocks_pad = -(-n_blocks // 128) * 128
        dest, bexp, eend = _sort(ids, n_blocks_pad)
        xs = _dispatch(dest, eend, h2, n_blocks * EXPERT_BLOCK)
        ys = _experts(bexp.reshape(-1), eend.reshape(-1), xs, e_gate[l], e_up[l], e_down[l], n_blocks)
        wts_col = jnp.transpose(wts, (0, 2, 1)).reshape(n, 2)
        last = l == depth - 1
        if not last:
            raise NotImplementedError("stacked layers need a residual-only combine")
        x2 = _combine(dest, ys, wts_col, x1, mod, modf, final_norm_g.reshape(1, d), seq)
    return x2.reshape(bsz, seq, d)
```

```python
import functools

import numpy as np
import jax
import jax.numpy as jnp
from jax import lax
from jax.experimental import pallas as pl
from jax.experimental.pallas import tpu as pltpu

F32 = jnp.float32
BF16 = jnp.bfloat16
I32 = jnp.int32

HEAD_DIM = 64
RWKV_WIDTH = 512
RWKV_HEADS = RWKV_WIDTH // HEAD_DIM
GROUP_HEADS = 2
DSA_WIDTH = 512
DSA_HEADS = DSA_WIDTH // HEAD_DIM
DECAY_LORA = 64
AAA_LORA = 64
GATE_LORA = 128
KV_LATENT = 128
IDX_HEADS = 4
IDX_DIM = 64
RWKV_COLS = 3 * RWKV_WIDTH + DECAY_LORA + AAA_LORA + GATE_LORA
DSA_COLS = DSA_WIDTH + KV_LATENT + IDX_HEADS * IDX_DIM + IDX_DIM + IDX_HEADS
DSA_COLS_PAD = 1024
TOPK_MAX = 256
ATTN_CHUNK_LOG2 = 6
Q_BLOCK = 256
RWKV_CHUNK = 64
RWKV_STEP_CHUNKS = 4
N_GROUPS = 4
EXPERTS_PER_GROUP = 8
N_EXPERTS = N_GROUPS * EXPERTS_PER_GROUP
ROUTER_ROWS = 40
NORM_EPS = 1e-6
GN_EPS = HEAD_DIM * 1e-5
NEG_BIG = -1e30
INT_MIN = -2 ** 31

VMEM_LIMIT_BYTES = 56 * 1024 * 1024
EXPERT_BLOCK = 512
TOKEN_TILE = 512

_NN = (((1,), (0,)), ((), ()))
_NT = (((1,), (1,)), ((), ()))
_TN = (((0,), (0,)), ((), ()))


def _dot(a, b, dims=_NN):
    return lax.dot_general(a, b, dims, preferred_element_type=F32)


def _split(x):
    hi = x.astype(BF16)
    lo = (x - hi.astype(F32)).astype(BF16)
    return hi, lo


def _dot3(a, b, dims=_NN):
    ah, al = _split(a)
    bh, bl = _split(b)
    return _dot(ah, bh, dims) + _dot(ah, bl, dims) + _dot(al, bh, dims)


def _dot_exact_rhs(a, b_bf16, dims=_NN):
    ah, al = _split(a)
    return _dot(ah, b_bf16, dims) + _dot(al, b_bf16, dims)


def _sigmoid(x):
    return 1.0 / (1.0 + jnp.exp(-x))


def _softplus(x):
    return jnp.maximum(x, 0.0) + jnp.log(1.0 + jnp.exp(-jnp.abs(x)))


def _params(n_axes):
    return pltpu.CompilerParams(dimension_semantics=("arbitrary",) * n_axes,
                                vmem_limit_bytes=VMEM_LIMIT_BYTES)


def _ada_kernel(c_ref, w_ref, b_ref, o_ref):
    c = c_ref[...]
    o_ref[...] = _dot3(c * _sigmoid(c), w_ref[...]) + b_ref[...]


def _ada(c, w, b):
    bsz, d = c.shape
    n = w.shape[1]
    tn = 1024
    return pl.pallas_call(
        _ada_kernel,
        grid=(n // tn,),
        in_specs=[pl.BlockSpec((bsz, d), lambda j: (0, 0)),
                  pl.BlockSpec((d, tn), lambda j: (0, j)),
                  pl.BlockSpec((1, tn), lambda j: (0, j))],
        out_specs=pl.BlockSpec((bsz, tn), lambda j: (0, j)),
        out_shape=jax.ShapeDtypeStruct((bsz, n), F32),
        compiler_params=_params(1),
        name="ada",
    )(c, w, b.reshape(1, n))


def _proj_kernel(x_ref, g_ref, sc_ref, sh_ref, w1_ref, w2_ref, o1_ref, o2_ref):
    x = x_ref[...]
    ms = jnp.mean(x * x, axis=-1, keepdims=True)
    y = x * lax.rsqrt(ms + NORM_EPS) * g_ref[...]
    h = (y * (1.0 + sc_ref[0]) + sh_ref[0]).astype(BF16)
    o1_ref[...] = _dot(h, w1_ref[...])
    o2_ref[...] = _dot(h, w2_ref[...])


def _proj(x2, g, mod, w1, w2, seq, tm):
    n, d = x2.shape
    tiles_per_seq = seq // tm
    return pl.pallas_call(
        _proj_kernel,
        grid=(n // tm,),
        in_specs=[pl.BlockSpec((tm, d), lambda i: (i, 0)),
                  pl.BlockSpec((1, d), lambda i: (0, 0)),
                  pl.BlockSpec((1, 1, d), lambda i: ((i // tiles_per_seq) * 6 + 1, 0, 0)),
                  pl.BlockSpec((1, 1, d), lambda i: ((i // tiles_per_seq) * 6 + 0, 0, 0)),
                  pl.BlockSpec(w1.shape, lambda i: (0, 0)),
                  pl.BlockSpec(w2.shape, lambda i: (0, 0))],
        out_specs=[pl.BlockSpec((tm, w1.shape[1]), lambda i: (i, 0)),
                   pl.BlockSpec((tm, w2.shape[1]), lambda i: (i, 0))],
        out_shape=[jax.ShapeDtypeStruct((n, w1.shape[1]), F32),
                   jax.ShapeDtypeStruct((n, w2.shape[1]), F32)],
        compiler_params=_params(1),
        name="proj",
    )(x2, g, mod, mod, w1, w2)


def _rwkv_kernel(p_ref, mu_ref, w0_ref, wda_ref, a0_ref, wg_ref, kk_ref, ka_ref, rk_ref,
                 gng_ref, gnb_ref, blk_ref, tril_ref, masks_ref, o_ref, s_ref, prev_ref):
    c = RWKV_CHUNK
    w = RWKV_WIDTH
    tt = p_ref.shape[0]
    chunks = range(tt // c)

    @pl.when(pl.program_id(1) == 0)
    def _():
        s_ref[...] = jnp.zeros_like(s_ref)
        prev_ref[...] = jnp.zeros_like(prev_ref)

    p = p_ref[...]
    row = lax.broadcasted_iota(I32, p.shape, 0)
    p_prev = jnp.where(row == 0, prev_ref[...], pltpu.roll(p, 1, 0))
    prev_ref[...] = p[tt - 1:tt, :]
    ps = p + mu_ref[...] * (p_prev - p)

    r = ps[:, 0:w]
    k = ps[:, w:2 * w]
    v = ps[:, 2 * w:3 * w]
    lora_in = ps[:, 3 * w:3 * w + DECAY_LORA + AAA_LORA]
    gate_in = ps[:, 3 * w + DECAY_LORA + AAA_LORA:]

    lane = lax.broadcasted_iota(I32, lora_in.shape, 1)
    lora_act = jnp.where(lane < DECAY_LORA, jnp.tanh(lora_in), lora_in).astype(BF16)
    da = _dot(lora_act, wda_ref[...])
    log_w = -_softplus(-(w0_ref[...] + da[:, :w])) - 0.5
    ld = -jnp.exp(log_w)
    a = _sigmoid(a0_ref[...] + da[:, w:])
    gate = _dot(_sigmoid(gate_in).astype(BF16), wg_ref[...])

    blk = blk_ref[...]

    def head_sums(z):
        nb = w // blk.shape[0]
        rows = jnp.concatenate([z[:, i * 128:(i + 1) * 128] for i in range(nb)], axis=0)
        sums = _dot(rows.astype(BF16), blk)
        return jnp.concatenate([sums[i * tt:(i + 1) * tt] for i in range(nb)], axis=1)

    kk = k * kk_ref[...]
    kk = kk / jnp.maximum(jnp.sqrt(head_sums(kk * kk)), 1e-12)
    k2 = k * (1.0 + (a - 1.0) * ka_ref[...])

    cs = _dot_exact_rhs_lhs(tril_ref[...], ld)
    cs_last = [cs[(ch + 1) * c - 1:(ch + 1) * c, :] for ch in chunks]
    cs_end = jnp.concatenate([jnp.broadcast_to(cl, (c, w)) for cl in cs_last], axis=0)
    e_neg = jnp.exp(-cs)
    e_rem = jnp.exp(cs_end - cs)
    kka = kk * a
    a_t = (-kk) * jnp.exp(cs - ld)
    b_t = kka * e_neg
    k_t = k2 * e_neg
    r_t = r * jnp.exp(cs)
    b_h = kka * e_rem
    k_h = k2 * e_rem
    w_c = [jnp.exp(cl) for cl in cs_last]

    bd = masks_ref[0]
    strict = masks_ref[1]
    incl = masks_ref[2]
    eye = masks_ref[3]
    gw = GROUP_HEADS * HEAD_DIM
    stack = lambda z: jnp.concatenate([z] * GROUP_HEADS, axis=0)

    groups = range(RWKV_HEADS // GROUP_HEADS)
    units = [(ch, g) for ch in chunks for g in groups]
    piece = lambda z, u: z[u[0] * c:(u[0] + 1) * c, u[1] * gw:(u[1] + 1) * gw]
    v_x = [(stack(piece(v, u)) * bd).astype(BF16) for u in units]
    ar_x = [jnp.concatenate([(stack(piece(a_t, u)) * bd).astype(BF16),
                             (stack(piece(r_t, u)) * bd).astype(BF16)], axis=0) for u in units]
    bk_r = [jnp.concatenate([stack(piece(b_t, u).astype(BF16)), stack(piece(k_t, u).astype(BF16))], axis=0)
            for u in units]
    bk_h = [jnp.concatenate([stack(piece(b_h, u).astype(BF16)), stack(piece(k_h, u).astype(BF16))], axis=0)
            for u in units]
    quad = [_dot(x, y, _NT) for x, y in zip(ar_x, bk_r)]
    l_ab = [q4[:gw, :gw] * strict for q4 in quad]
    l_ak = [(q4[:gw, gw:] * strict).astype(BF16) for q4 in quad]
    m_cat = [jnp.concatenate([(q4[gw:, :gw] * incl).astype(BF16), (q4[gw:, gw:] * incl).astype(BF16)], axis=1)
             for q4 in quad]

    pw = [p.astype(BF16) for p in l_ab]
    t_inv = [eye + l for l in l_ab]
    pw = [_dot(p, p) for p in pw]
    for _ in range(4):
        pb = [p.astype(BF16) for p in pw]
        both = [_dot(jnp.concatenate([t.astype(BF16), p], axis=0), p) for t, p in zip(t_inv, pb)]
        t_inv = [t + tp[:gw] for t, tp in zip(t_inv, both)]
        pw = [tp[gw:] for tp in both]
    t_inv = [(t + _dot(t.astype(BF16), p.astype(BF16))).astype(BF16) for t, p in zip(t_inv, pw)]

    state = [s_ref[g] for g in groups]
    y_rows = []
    for ch in chunks:
        us = [ch * len(groups) + g for g in groups]
        ar_s = [_dot(ar_x[u], state[g].astype(BF16), _NT) for g, u in zip(groups, us)]
        rhs = [ar_s[g][:gw] + _dot(l_ak[u], v_x[u]) for g, u in zip(groups, us)]
        ub = [_dot(t_inv[u], rhs[g].astype(BF16)).astype(BF16) for g, u in zip(groups, us)]
        uv = [jnp.concatenate([ub[g], v_x[u]], axis=0) for g, u in zip(groups, us)]
        y_x = [ar_s[g][gw:] + _dot(m_cat[u], uv[g]) for g, u in zip(groups, us)]
        y_rows.append(jnp.concatenate(
            [sum(yx[hh * c:(hh + 1) * c] for hh in range(1, GROUP_HEADS)) + yx[0:c] for yx in y_x], axis=1))
        state = [(state[g] * w_c[ch][:, g * gw:(g + 1) * gw] + _dot(uv[g], bk_h[u], _TN)) * bd
                 for g, u in zip(groups, us)]
    for g in groups:
        s_ref[g] = state[g]

    y = jnp.concatenate(y_rows, axis=0)
    inv_n = 1.0 / HEAD_DIM
    mean = head_sums(y) * inv_n
    dlt = y - mean
    var = head_sums(dlt * dlt) * inv_n
    yn = dlt * lax.rsqrt(var + GN_EPS) * gng_ref[...] + gnb_ref[...]
    bonus = head_sums(r * k2 * rk_ref[...]) * v
    o_ref[...] = ((yn + bonus) * gate).astype(o_ref.dtype)


def _dot_exact_rhs_lhs(a_bf16, b):
    bh, bl = _split(b)
    return _dot(a_bf16, bh) + _dot(a_bf16, bl)


def _rwkv(p, bsz, seq, mu, w0, wda, a0, wg, k_k, k_a, r_k, gn_g, gn_b):
    c = RWKV_CHUNK
    w = RWKV_WIDTH
    tt = c * RWKV_STEP_CHUNKS if seq % (c * RWKV_STEP_CHUNKS) == 0 else c
    n_steps = seq // tt
    head_of = np.arange(128) // HEAD_DIM
    blk = jnp.asarray(head_of[:, None] == head_of[None, :], BF16)
    ti = np.arange(tt)
    tril = jnp.asarray((ti[:, None] >= ti[None, :]) & (ti[:, None] // c == ti[None, :] // c), BF16)
    assert c == HEAD_DIM and RWKV_HEADS % GROUP_HEADS == 0
    gi = np.arange(GROUP_HEADS * c)
    same = (gi[:, None] // c) == (gi[None, :] // c)
    later = (gi[:, None] % c) > (gi[None, :] % c)
    masks = jnp.asarray(np.stack([same, same & later, same & (later | (gi[:, None] % c == gi[None, :] % c)),
                                  gi[:, None] == gi[None, :]]), F32)
    row = lambda a: a.reshape(1, -1)
    vec = lambda n: pl.BlockSpec((1, n), lambda b, t: (0, 0))
    full = lambda a: pl.BlockSpec(a.shape, lambda b, t: (0, 0))
    return pl.pallas_call(
        _rwkv_kernel,
        grid=(bsz, n_steps),
        in_specs=[pl.BlockSpec((tt, RWKV_COLS), lambda b, t: (b * n_steps + t, 0)),
                  vec(RWKV_COLS), vec(w), full(wda), vec(w), full(wg), vec(w), vec(w), vec(w),
                  vec(w), vec(w), full(blk), full(tril),
                  pl.BlockSpec(masks.shape, lambda b, t: (0, 0, 0))],
        out_specs=pl.BlockSpec((tt, w), lambda b, t: (b * n_steps + t, 0)),
        out_shape=jax.ShapeDtypeStruct((bsz * seq, w), BF16),
        scratch_shapes=[pltpu.VMEM((RWKV_HEADS // GROUP_HEADS, GROUP_HEADS * HEAD_DIM, GROUP_HEADS * HEAD_DIM), F32),
                        pltpu.VMEM((1, RWKV_COLS), F32)],
        compiler_params=_params(2),
        name="rwkv",
    )(p, row(mu), row(w0), wda, row(a0), wg, row(k_k), row(k_a), row(r_k), row(gn_g), row(gn_b),
      blk, tril, masks)


SLAB_ROWS = 64
LOW_BITS = 8
ATTN_HEAD_GROUP = 8
ONES_ROWS = 16


def _fold_rows(x, op):
    acc = x[0:SLAB_ROWS]
    for r0 in range(SLAB_ROWS, x.shape[0], SLAB_ROWS):
        acc = op(acc, x[r0:r0 + SLAB_ROWS])
    return acc


def _split3(x):
    hi = x.astype(BF16)
    rest = x - hi.astype(F32)
    mid = rest.astype(BF16)
    return hi, mid, (rest - mid.astype(F32)).astype(BF16)


def _dsa_kernel(pd_ref, kvg_ref, kig_ref, wuk_ref, wuv_ref, tril_ref, wsel_ref, o_ref,
                ckv_s, ckvt_s, kcat_s, score_s, *, nq, n_cls, topk):
    qb = Q_BLOCK
    j = pl.program_id(1)

    @pl.when(j == 0)
    def _():
        ckv_s[...] = jnp.zeros_like(ckv_s)
        ckvt_s[...] = jnp.zeros_like(ckvt_s)
        kcat_s[...] = jnp.zeros_like(kcat_s)

    pd = pd_ref[...]
    q = pd[:, :DSA_WIDTH]
    ckv = pd[:, DSA_WIDTH:DSA_WIDTH + KV_LATENT]
    qi = pd[:, DSA_WIDTH + KV_LATENT:DSA_WIDTH + KV_LATENT + IDX_HEADS * IDX_DIM]
    tail = pd[:, DSA_WIDTH + KV_LATENT + IDX_HEADS * IDX_DIM:]
    ki = tail[:, :IDX_DIM]

    ckv_n = ckv * lax.rsqrt(jnp.mean(ckv * ckv, axis=-1, keepdims=True) + NORM_EPS) * kvg_ref[...]
    ki_n = ki * lax.rsqrt(jnp.mean(ki * ki, axis=-1, keepdims=True) + NORM_EPS) * kig_ref[...]
    ki_hi, ki_lo = _split(ki_n)
    rows = pl.ds(pl.multiple_of(j * qb, qb), qb)
    ckv_s[rows, :] = ckv_n.astype(BF16)
    one_row = jnp.where(lax.broadcasted_iota(I32, (ONES_ROWS, qb), 0) == 0, 1.0, 0.0)
    ckvt_s[j] = jnp.concatenate([ckv_n.T, one_row], axis=0).astype(BF16)
    kcat_s[rows, :] = jnp.concatenate([ki_hi, ki_hi, ki_lo, jnp.zeros_like(ki_hi)], axis=1)

    w_t = sum(_dot(wsel_ref[...], part, _NT) for part in _split3(tail))

    per = nq // n_cls
    for cls in range(n_cls):
        @pl.when((j >= cls * per) & (j < (cls + 1) * per))
        def _(sk=(cls + 1) * per * qb):
            _dsa_block(j, q, qi, w_t, wuk_ref, wuv_ref, tril_ref, o_ref, ckv_s, ckvt_s, kcat_s,
                       score_s, sk=sk, topk=topk)


def _dsa_block(j, q, qi, w_t, wuk_ref, wuv_ref, tril_ref, o_ref, ckv_s, ckvt_s, kcat_s, score_s,
               *, sk, topk):
    qb = Q_BLOCK
    kcat = kcat_s[0:sk, :]
    score = jnp.zeros((sk, qb), F32)
    w_s = w_t * (IDX_HEADS ** -0.5)
    for hh in range(IDX_HEADS):
        q_hi, q_lo = _split(qi[:, hh * IDX_DIM:(hh + 1) * IDX_DIM] * (IDX_DIM ** -0.5))
        dots = _dot(kcat, jnp.concatenate([q_hi, q_lo, q_hi, jnp.zeros_like(q_hi)], axis=1), _NT)
        score = score + w_s[hh:hh + 1, :] * jnp.maximum(dots, 0.0)

    key_pos = lax.broadcasted_iota(I32, (sk, qb), 0)
    q_pos = j * qb + lax.broadcasted_iota(I32, (sk, qb), 1)
    adm = jnp.right_shift(key_pos, ATTN_CHUNK_LOG2) <= jnp.right_shift(q_pos, ATTN_CHUNK_LOG2)

    score_s[0:sk, :] = jnp.where(adm, score, NEG_BIG)

    def as_float(okey):
        return lax.bitcast_convert_type(okey ^ ((okey >> 31) & 0x7FFFFFFF), F32)

    def count_ge(cand):
        acc = jnp.zeros((SLAB_ROWS, qb), F32)
        for r0 in range(0, sk, SLAB_ROWS):
            acc = acc + jnp.where(score_s[r0:r0 + SLAB_ROWS, :] >= cand, 1.0, 0.0)
        return jnp.sum(acc, axis=0, keepdims=True)

    def descend(i, tu):
        cand_u = tu | jnp.left_shift(jnp.int32(1), 31 - i)
        return jnp.where(count_ge(as_float(cand_u ^ INT_MIN)) >= topk, cand_u, tu)

    if sk <= topk:
        thr = jnp.full((1, qb), NEG_BIG, F32)
    else:
        high = 32 - LOW_BITS
        tu = lax.fori_loop(0, high, descend, jnp.zeros((1, qb), I32))
        end = as_float((tu + (1 << LOW_BITS)) ^ INT_MIN)
        best = jnp.full((SLAB_ROWS, qb), -jnp.inf, F32)
        for r0 in range(0, sk, SLAB_ROWS):
            slab = score_s[r0:r0 + SLAB_ROWS, :]
            best = jnp.maximum(best, jnp.where(slab < end, slab, -jnp.inf))
        guess = jnp.max(best, axis=0, keepdims=True)
        settled = (count_ge(guess) >= topk) & (guess > -jnp.inf)
        missed = jnp.max(jnp.where(settled, 0, 1))
        tu = lax.fori_loop(high, high + LOW_BITS * missed, descend, tu)
        thr = jnp.where(missed == 1, as_float(tu ^ INT_MIN), guess)

    score = score_s[0:sk, :]
    gt = score > thr
    eq = score == thr
    need = topk - jnp.sum(_fold_rows(jnp.where(gt, 1.0, 0.0), jnp.add), axis=0, keepdims=True)
    eq_b = jnp.where(eq, 1.0, 0.0).astype(BF16)
    tb = tril_ref.shape[0]
    off = jnp.zeros((1, qb), F32)
    pieces = []
    for kb in range(sk // tb):
        pre = _dot(tril_ref[...], eq_b[kb * tb:(kb + 1) * tb, :])
        pieces.append(pre + off)
        off = off + pre[tb - 1:tb, :]
    prefix = jnp.concatenate(pieces, axis=0)
    tie_bias = jnp.where(prefix <= need, 0.0, NEG_BIG)
    bias = jnp.where(gt, 0.0, jnp.where(eq, tie_bias, NEG_BIG))
    bias = jnp.where(adm, bias, NEG_BIG)

    ckv_all = ckv_s[0:sk, :]
    ckv_t = jnp.concatenate([ckvt_s[b] for b in range(sk // qb)], axis=1)
    q_lat = (_dot(q.astype(BF16), wuk_ref[...]) * (HEAD_DIM ** -0.5)).astype(BF16)
    head_logits = lambda h: _dot(ckv_all, q_lat[:, h * KV_LATENT:(h + 1) * KV_LATENT], _NT)
    outs = []
    nxt = head_logits(0)
    for h in range(DSA_HEADS):
        logits = nxt + bias
        if h + 1 < DSA_HEADS:
            nxt = head_logits(h + 1)
        mx = jnp.max(_fold_rows(logits, jnp.maximum), axis=0, keepdims=True)
        pr = jnp.exp((logits - mx).astype(BF16))
        acc = _dot(ckv_t, pr)
        outs.append((acc[:KV_LATENT] / acc[KV_LATENT:KV_LATENT + 1]).astype(BF16))
    o_lat_t = jnp.concatenate(outs, axis=0)
    o_ref[...] = _dot(o_lat_t, wuv_ref[...], _TN).astype(o_ref.dtype)


def _dsa(pd, bsz, seq, kv_g, ki_g, w_uk, w_uv):
    qb = Q_BLOCK
    nq = seq // qb
    topk = min(TOPK_MAX, seq // 4)
    hc = DSA_HEADS * KV_LATENT
    head_of_q = np.arange(DSA_WIDTH) // HEAD_DIM
    head_of_l = np.arange(hc) // KV_LATENT
    mask = jnp.asarray(head_of_q[:, None] == head_of_l[None, :], F32)
    uk = jnp.transpose(w_uk, (1, 2, 0)).reshape(DSA_WIDTH, KV_LATENT)
    wuk_bd = (jnp.tile(uk, (1, DSA_HEADS)) * mask).astype(BF16)
    uv = w_uv.reshape(KV_LATENT, DSA_WIDTH)
    wuv_bd = (jnp.tile(uv, (DSA_HEADS, 1)) * mask.T).astype(BF16)
    n_cls = max(d for d in (8, 4, 2, 1) if nq % d == 0)
    tb = 256 if ((nq // n_cls) * qb) % 256 == 0 else 128
    tril = jnp.asarray(np.tril(np.ones((tb, tb), np.float32)), BF16)
    wsel = jnp.asarray(np.arange(128)[None, :] == (IDX_DIM + np.arange(8))[:, None], BF16)
    wsel = wsel * jnp.asarray(np.arange(8)[:, None] < IDX_HEADS, BF16)
    full = lambda a: pl.BlockSpec(a.shape, lambda b, j: (0, 0))
    return pl.pallas_call(
        functools.partial(_dsa_kernel, nq=nq, n_cls=n_cls, topk=topk),
        grid=(bsz, nq),
        in_specs=[pl.BlockSpec((qb, DSA_COLS_PAD), lambda b, j: (b * nq + j, 0)),
                  pl.BlockSpec((1, KV_LATENT), lambda b, j: (0, 0)),
                  pl.BlockSpec((1, IDX_DIM), lambda b, j: (0, 0)),
                  full(wuk_bd), full(wuv_bd), full(tril), full(wsel)],
        out_specs=pl.BlockSpec((qb, DSA_WIDTH), lambda b, j: (b * nq + j, 0)),
        out_shape=jax.ShapeDtypeStruct((bsz * seq, DSA_WIDTH), BF16),
        scratch_shapes=[pltpu.VMEM((seq, KV_LATENT), BF16),
                        pltpu.VMEM((nq, KV_LATENT + ONES_ROWS, qb), BF16),
                        pltpu.VMEM((seq, 4 * IDX_DIM), BF16),
                        pltpu.VMEM((seq, qb), F32)],
        compiler_params=_params(2),
        name="dsa",
    )(pd, kv_g.reshape(1, -1), ki_g.reshape(1, -1), wuk_bd, wuv_bd, tril, wsel)


def _dsa_flash_kernel(pd_ref, kvg_ref, kig_ref, wuk_ref, wuv_ref, tril_ref, wsel_ref, o_ref,
                      ckv_s, ckvt_s, kcat_s, score_s, bias_s, acc_s, m_s, *, topk):
    qb = Q_BLOCK
    kb_rows = Q_BLOCK
    j = pl.program_id(1)
    n_kb = j + 1

    pd = pd_ref[...]
    q = pd[:, :DSA_WIDTH]
    ckv = pd[:, DSA_WIDTH:DSA_WIDTH + KV_LATENT]
    qi = pd[:, DSA_WIDTH + KV_LATENT:DSA_WIDTH + KV_LATENT + IDX_HEADS * IDX_DIM]
    tail = pd[:, DSA_WIDTH + KV_LATENT + IDX_HEADS * IDX_DIM:]
    ki = tail[:, :IDX_DIM]

    ckv_n = ckv * lax.rsqrt(jnp.mean(ckv * ckv, axis=-1, keepdims=True) + NORM_EPS) * kvg_ref[...]
    ki_n = ki * lax.rsqrt(jnp.mean(ki * ki, axis=-1, keepdims=True) + NORM_EPS) * kig_ref[...]
    ki_hi, ki_lo = _split(ki_n)
    rows = pl.ds(pl.multiple_of(j * qb, qb), qb)
    ckv_s[rows, :] = ckv_n.astype(BF16)
    one_row = jnp.where(lax.broadcasted_iota(I32, (ONES_ROWS, qb), 0) == 0, 1.0, 0.0)
    ckvt_s[j] = jnp.concatenate([ckv_n.T, one_row], axis=0).astype(BF16)
    kcat_s[rows, :] = jnp.concatenate([ki_hi, ki_hi, ki_lo, jnp.zeros_like(ki_hi)], axis=1)

    w_s = sum(_dot(wsel_ref[...], part, _NT) for part in _split3(tail)) * (IDX_HEADS ** -0.5)
    q_cat = []
    for hh in range(IDX_HEADS):
        q_hi, q_lo = _split(qi[:, hh * IDX_DIM:(hh + 1) * IDX_DIM] * (IDX_DIM ** -0.5))
        q_cat.append(jnp.concatenate([q_hi, q_lo, q_hi, jnp.zeros_like(q_hi)], axis=1))

    def key_rows(kb):
        return pl.ds(pl.multiple_of(kb * kb_rows, kb_rows), kb_rows)

    def admissible(kb):
        key_pos = kb * kb_rows + lax.broadcasted_iota(I32, (kb_rows, qb), 0)
        q_pos = j * qb + lax.broadcasted_iota(I32, (kb_rows, qb), 1)
        return jnp.right_shift(key_pos, ATTN_CHUNK_LOG2) <= jnp.right_shift(q_pos, ATTN_CHUNK_LOG2)

    def score_block(kb, carry):
        kcat = kcat_s[key_rows(kb), :]
        dots = [_dot(kcat, q_cat[hh], _NT) for hh in range(IDX_HEADS)]
        score = sum(w_s[hh:hh + 1, :] * jnp.maximum(dots[hh], 0.0) for hh in range(1, IDX_HEADS)) \
            + w_s[0:1, :] * jnp.maximum(dots[0], 0.0)
        score_s[key_rows(kb), :] = jnp.where(admissible(kb), score, NEG_BIG)
        return carry

    lax.fori_loop(0, n_kb, score_block, 0)

    def as_float(okey):
        return lax.bitcast_convert_type(okey ^ ((okey >> 31) & 0x7FFFFFFF), F32)

    def count_where(pred):
        def block(kb, acc):
            base = kb * kb_rows
            for r0 in range(0, kb_rows, SLAB_ROWS):
                slab = score_s[pl.ds(pl.multiple_of(base + r0, SLAB_ROWS), SLAB_ROWS), :]
                acc = acc + jnp.where(pred(slab), 1.0, 0.0)
            return acc
        acc = lax.fori_loop(0, n_kb, block, jnp.zeros((SLAB_ROWS, qb), F32))
        return jnp.sum(acc, axis=0, keepdims=True)

    def descend(i, tu):
        cand_u = tu | jnp.left_shift(jnp.int32(1), 31 - i)
        cand = as_float(cand_u ^ INT_MIN)
        return jnp.where(count_where(lambda s: s >= cand) >= topk, cand_u, tu)

    few_keys = n_kb * kb_rows <= topk
    n_bits = jnp.where(few_keys, 0, 32)
    thr = as_float(lax.fori_loop(0, n_bits, descend, jnp.zeros((1, qb), I32)) ^ INT_MIN)
    thr = jnp.where(few_keys, NEG_BIG, thr)
    need = topk - count_where(lambda s: s > thr)

    def bias_block(kb, tie_count):
        score = score_s[key_rows(kb), :]
        eq = score == thr
        prefix = _dot(tril_ref[...], jnp.where(eq, 1.0, 0.0).astype(BF16)) + tie_count
        tie_bias = jnp.where(eq, jnp.where(prefix <= need, 0.0, NEG_BIG), NEG_BIG)
        bias_s[key_rows(kb), :] = jnp.where(admissible(kb), jnp.where(score > thr, 0.0, tie_bias), NEG_BIG)
        return prefix[kb_rows - 1:kb_rows, :]

    lax.fori_loop(0, n_kb, bias_block, jnp.zeros((1, qb), F32))

    q_lat = (_dot(q.astype(BF16), wuk_ref[...]) * (HEAD_DIM ** -0.5)).astype(BF16)
    acc_s[...] = jnp.zeros_like(acc_s)
    m_s[...] = jnp.full(m_s.shape, NEG_BIG, F32)

    def attend_block(kb, carry):
        ckv_kb = ckv_s[key_rows(kb), :]
        ckvt_kb = ckvt_s[kb]
        bias = bias_s[key_rows(kb), :]
        for h0 in range(0, DSA_HEADS, ATTN_HEAD_GROUP):
            hs = range(h0, h0 + ATTN_HEAD_GROUP)
            logits = [_dot(ckv_kb, q_lat[:, h * KV_LATENT:(h + 1) * KV_LATENT], _NT) + bias for h in hs]
            m_old = [m_s[h] for h in hs]
            m_new = [jnp.maximum(mo, jnp.max(_fold_rows(lg, jnp.maximum), axis=0, keepdims=True))
                     for mo, lg in zip(m_old, logits)]
            pr = [jnp.exp((lg - mn).astype(BF16)) for lg, mn in zip(logits, m_new)]
            for i, h in enumerate(hs):
                acc_s[h] = acc_s[h] * jnp.exp(m_old[i] - m_new[i]) + _dot(ckvt_kb, pr[i])
                m_s[h] = m_new[i]
        return carry

    lax.fori_loop(0, n_kb, attend_block, 0)

    o_lat_t = jnp.concatenate(
        [(acc_s[h][:KV_LATENT] / acc_s[h][KV_LATENT:KV_LATENT + 1]).astype(BF16) for h in range(DSA_HEADS)],
        axis=0)
    o_ref[...] = _dot(o_lat_t, wuv_ref[...], _TN).astype(o_ref.dtype)


def _dsa_flash(pd, bsz, seq, kv_g, ki_g, w_uk, w_uv):
    qb = min(Q_BLOCK, seq)
    assert qb == Q_BLOCK and seq % qb == 0
    nq = seq // qb
    topk = min(TOPK_MAX, seq // 4)
    hc = DSA_HEADS * KV_LATENT
    head_of_q = np.arange(DSA_WIDTH) // HEAD_DIM
    head_of_l = np.arange(hc) // KV_LATENT
    mask = jnp.asarray(head_of_q[:, None] == head_of_l[None, :], F32)
    uk = jnp.transpose(w_uk, (1, 2, 0)).reshape(DSA_WIDTH, KV_LATENT)
    wuk_bd = (jnp.tile(uk, (1, DSA_HEADS)) * mask).astype(BF16)
    uv = w_uv.reshape(KV_LATENT, DSA_WIDTH)
    wuv_bd = (jnp.tile(uv, (DSA_HEADS, 1)) * mask.T).astype(BF16)
    tril = jnp.asarray(np.tril(np.ones((qb, qb), np.float32)), BF16)
    wsel = jnp.asarray((np.arange(128)[None, :] == (IDX_DIM + np.arange(8))[:, None])
                       & (np.arange(8)[:, None] < IDX_HEADS), BF16)
    full = lambda a: pl.BlockSpec(a.shape, lambda b, j: (0, 0))
    return pl.pallas_call(
        functools.partial(_dsa_flash_kernel, topk=topk),
        grid=(bsz, nq),
        in_specs=[pl.BlockSpec((qb, DSA_COLS_PAD), lambda b, j: (b * nq + j, 0)),
                  pl.BlockSpec((1, KV_LATENT), lambda b, j: (0, 0)),
                  pl.BlockSpec((1, IDX_DIM), lambda b, j: (0, 0)),
                  full(wuk_bd), full(wuv_bd), full(tril), full(wsel)],
        out_specs=pl.BlockSpec((qb, DSA_WIDTH), lambda b, j: (b * nq + j, 0)),
        out_shape=jax.ShapeDtypeStruct((bsz * seq, DSA_WIDTH), BF16),
        scratch_shapes=[pltpu.VMEM((seq, KV_LATENT), BF16),
                        pltpu.VMEM((nq, KV_LATENT + ONES_ROWS, qb), BF16),
                        pltpu.VMEM((seq, 4 * IDX_DIM), BF16),
                        pltpu.VMEM((seq, qb), F32),
                        pltpu.VMEM((seq, qb), F32),
                        pltpu.VMEM((DSA_HEADS, KV_LATENT + ONES_ROWS, qb), F32),
                        pltpu.VMEM((DSA_HEADS, 1, qb), F32)],
        compiler_params=_params(2),
        name="dsa",
    )(pd, kv_g.reshape(1, -1), ki_g.reshape(1, -1), wuk_bd, wuv_bd, tril, wsel)


def _mix_kernel(yr_ref, yd_ref, x_ref, wo1_ref, wo2_ref, g1_ref, n2_ref, sc_ref, sh_ref,
                wr_ref, br_ref, x1_ref, h2_ref, ids_ref, wts_ref):
    mixed = _dot(yr_ref[...], wo1_ref[...]) + _dot(yd_ref[...], wo2_ref[...])
    x1 = x_ref[...] + (1.0 + g1_ref[0]) * mixed
    x1_ref[...] = x1
    ms = jnp.mean(x1 * x1, axis=-1, keepdims=True)
    h2 = x1 * lax.rsqrt(ms + NORM_EPS) * n2_ref[...] * (1.0 + sc_ref[0]) + sh_ref[0]
    h2_ref[...] = h2

    lt = _dot3(wr_ref[...], h2, _NT) + br_ref[...]
    gl = lt[0:N_GROUPS]
    el = lt[8:8 + N_EXPERTS]
    tm = gl.shape[1]
    gmax = jnp.max(gl, axis=0, keepdims=True)
    gidx = lax.broadcasted_iota(I32, (N_GROUPS, tm), 0)
    gsel = jnp.min(jnp.where(gl == gmax, gidx, N_GROUPS), axis=0, keepdims=True)
    p_group = 1.0 / jnp.sum(jnp.exp(gl - gmax), axis=0, keepdims=True)
    eidx = lax.broadcasted_iota(I32, (N_EXPERTS, tm), 0)
    el = jnp.where(jnp.right_shift(eidx, 3) == gsel, el, NEG_BIG)
    m1 = jnp.max(el, axis=0, keepdims=True)
    i1 = jnp.min(jnp.where(el == m1, eidx, N_EXPERTS), axis=0, keepdims=True)
    el2 = jnp.where(eidx == i1, NEG_BIG, el)
    m2 = jnp.max(el2, axis=0, keepdims=True)
    i2 = jnp.min(jnp.where(el2 == m2, eidx, N_EXPERTS), axis=0, keepdims=True)
    e2 = jnp.exp(m2 - m1)
    w1 = p_group / (1.0 + e2)
    ids_ref[0] = jnp.concatenate([i1, i2], axis=0)
    wts_ref[0] = jnp.concatenate([w1, w1 * e2], axis=0)


def _mix(yr, yd, x2, wo1, wo2, mod, n2g, wr, br, seq, tm):
    n, d = x2.shape
    nt = n // tm
    tps = seq // tm
    modspec = lambda k: pl.BlockSpec((1, 1, d), lambda i: ((i // tps) * 6 + k, 0, 0))
    full = lambda a: pl.BlockSpec(a.shape, lambda i: (0,) * a.ndim)
    return pl.pallas_call(
        _mix_kernel,
        grid=(nt,),
        in_specs=[pl.BlockSpec((tm, yr.shape[1]), lambda i: (i, 0)),
                  pl.BlockSpec((tm, yd.shape[1]), lambda i: (i, 0)),
                  pl.BlockSpec((tm, d), lambda i: (i, 0)),
                  full(wo1), full(wo2), modspec(2), full(n2g), modspec(4), modspec(3),
                  full(wr), full(br)],
        out_specs=[pl.BlockSpec((tm, d), lambda i: (i, 0)),
                   pl.BlockSpec((tm, d), lambda i: (i, 0)),
                   pl.BlockSpec((1, 2, tm), lambda i: (i, 0, 0)),
                   pl.BlockSpec((1, 2, tm), lambda i: (i, 0, 0))],
        out_shape=[jax.ShapeDtypeStruct((n, d), F32),
                   jax.ShapeDtypeStruct((n, d), F32),
                   jax.ShapeDtypeStruct((nt, 2, tm), I32),
                   jax.ShapeDtypeStruct((nt, 2, tm), F32)],
        compiler_params=_params(1),
        name="mix",
    )(yr, yd, x2, wo1, wo2, mod, n2g, mod, mod, wr, br)


def _sort_kernel(ids_ref, triu_ref, tril_ref, dest_ref, bexp_ref, eend_ref,
                 cnt_ref, run_ref, start_ref, *, n_blocks_pad):
    phase = pl.program_id(0)
    i = pl.program_id(1)
    ids = ids_ref[0]
    tm = ids.shape[1]
    eidx = lax.broadcasted_iota(I32, (N_EXPERTS, tm), 0)
    hit0 = eidx == ids[0:1]
    hit1 = eidx == ids[1:2]
    onehot = jnp.where(hit0, 1.0, 0.0) + jnp.where(hit1, 1.0, 0.0)

    @pl.when((phase == 0) & (i == 0))
    def _():
        cnt_ref[...] = jnp.zeros_like(cnt_ref)

    @pl.when(phase == 0)
    def _():
        cnt_ref[...] += jnp.sum(onehot, axis=1, keepdims=True)

    @pl.when((phase == 1) & (i == 0))
    def _():
        run_ref[...] = jnp.zeros_like(run_ref)
        nblk = jnp.floor((cnt_ref[...] + (EXPERT_BLOCK - 1)) * (1.0 / EXPERT_BLOCK))
        nblk_b = jnp.broadcast_to(nblk, (N_EXPERTS, 128))
        first_blk = _dot_exact_rhs_lhs(tril_ref[...], nblk_b)
        start_ref[...] = first_blk[:, 0:1] * EXPERT_BLOCK
        end_blk = first_blk + nblk_b
        bidx = lax.broadcasted_iota(I32, (N_EXPERTS, n_blocks_pad), 1).astype(F32)
        owner = jnp.sum(jnp.where(end_blk[:, 0:1] <= bidx, 1.0, 0.0), axis=0, keepdims=True)
        bexp_ref[...] = jnp.minimum(owner, N_EXPERTS - 1).astype(I32)
        on_diag = (lax.broadcasted_iota(I32, (N_EXPERTS, 128), 0)
                   == lax.broadcasted_iota(I32, (N_EXPERTS, 128), 1))
        eend_ref[...] = jnp.sum(jnp.where(on_diag, end_blk, 0.0), axis=0, keepdims=True).astype(I32)

    @pl.when(phase == 1)
    def _():
        before = _dot(onehot.astype(BF16), triu_ref[...])
        pos = start_ref[...] + run_ref[...] + before
        d0 = jnp.sum(jnp.where(hit0, pos, 0.0), axis=0, keepdims=True)
        d1 = jnp.sum(jnp.where(hit1, pos, 0.0), axis=0, keepdims=True)
        dest_ref[0] = jnp.concatenate([d0, d1], axis=0).astype(I32)
        run_ref[...] += jnp.sum(onehot, axis=1, keepdims=True)


def _sort(ids, n_blocks_pad):
    nt, _, tm = ids.shape
    triu = jnp.asarray(np.triu(np.ones((tm, tm), np.float32), 1), BF16)
    tril = jnp.asarray(np.tril(np.ones((N_EXPERTS, N_EXPERTS), np.float32), -1), BF16)
    return pl.pallas_call(
        functools.partial(_sort_kernel, n_blocks_pad=n_blocks_pad),
        grid=(2, nt),
        in_specs=[pl.BlockSpec((1, 2, tm), lambda p, i: (i, 0, 0)),
                  pl.BlockSpec(triu.shape, lambda p, i: (0, 0)),
                  pl.BlockSpec(tril.shape, lambda p, i: (0, 0))],
        out_specs=[pl.BlockSpec((1, 2, tm), lambda p, i: (i * p, 0, 0)),
                   pl.BlockSpec((1, n_blocks_pad), lambda p, i: (0, 0)),
                   pl.BlockSpec((1, 128), lambda p, i: (0, 0))],
        out_shape=[jax.ShapeDtypeStruct((nt, 2, tm), I32),
                   jax.ShapeDtypeStruct((1, n_blocks_pad), I32),
                   jax.ShapeDtypeStruct((1, 128), I32)],
        scratch_shapes=[pltpu.VMEM((N_EXPERTS, 1), F32),
                        pltpu.VMEM((N_EXPERTS, 1), F32),
                        pltpu.VMEM((N_EXPERTS, 1), F32)],
        compiler_params=_params(2),
        name="sort",
    )(ids, triu, tril)


def _row_copy(src_ref, src_row, dst_ref, dst_row, sem):
    return pltpu.make_async_copy(src_ref.at[pl.ds(src_row, 1), :], dst_ref.at[pl.ds(dst_row, 1), :], sem)


ROW_UNROLL = 8


def _dispatch_kernel(dest_ref, eend_ref, h_ref, xs_ref, zbuf, sem, zsem):
    tm = h_ref.shape[0]

    @pl.when(pl.program_id(0) == 0)
    def _():
        zbuf[...] = jnp.zeros_like(zbuf)

        def last_block_copy(e):
            end_b = eend_ref[0, e]
            begin_b = jnp.where(e == 0, 0, eend_ref[0, jnp.maximum(e - 1, 0)])
            row0 = pl.multiple_of((end_b - 1) * EXPERT_BLOCK, EXPERT_BLOCK)
            return end_b > begin_b, pltpu.make_async_copy(zbuf, xs_ref.at[pl.ds(row0, EXPERT_BLOCK), :], zsem)

        def fill(e, carry):
            owns_rows, cp = last_block_copy(e)
            pl.when(owns_rows)(cp.start)
            return carry

        def drain(e, carry):
            owns_rows, cp = last_block_copy(e)
            pl.when(owns_rows)(cp.wait)
            return carry

        lax.fori_loop(0, N_EXPERTS, fill, 0)
        lax.fori_loop(0, N_EXPERTS, drain, 0)

        def spare_block_copy(b):
            row0 = pl.multiple_of(b * EXPERT_BLOCK, EXPERT_BLOCK)
            return pltpu.make_async_copy(zbuf, xs_ref.at[pl.ds(row0, EXPERT_BLOCK), :], zsem)

        n_used = eend_ref[0, N_EXPERTS - 1]
        n_blocks = xs_ref.shape[0] // EXPERT_BLOCK
        lax.fori_loop(n_used, n_blocks, lambda b, carry: (spare_block_copy(b).start(), carry)[1], 0)
        lax.fori_loop(n_used, n_blocks, lambda b, carry: (spare_block_copy(b).wait(), carry)[1], 0)

    def start(g, carry):
        for u in range(ROW_UNROLL):
            r = g * ROW_UNROLL + u
            _row_copy(h_ref, r, xs_ref, dest_ref[0, 0, r], sem).start(priority=u % 2)
            _row_copy(h_ref, r, xs_ref, dest_ref[0, 1, r], sem).start(priority=(u + 1) % 2)
        return carry

    lax.fori_loop(0, tm // ROW_UNROLL, start, 0)
    all_rows = pltpu.make_async_copy(h_ref, xs_ref.at[pl.ds(0, tm), :], sem)
    all_rows.wait()
    all_rows.wait()


def _dispatch(dest, eend, h2, n_rows):
    nt, _, tm = dest.shape
    n, d = h2.shape
    return pl.pallas_call(
        _dispatch_kernel,
        grid=(nt,),
        in_specs=[pl.BlockSpec((1, 2, tm), lambda i: (i, 0, 0), memory_space=pltpu.SMEM),
                  pl.BlockSpec(eend.shape, lambda i: (0, 0), memory_space=pltpu.SMEM),
                  pl.BlockSpec((tm, d), lambda i: (i, 0))],
        out_specs=pl.BlockSpec(memory_space=pl.ANY),
        out_shape=jax.ShapeDtypeStruct((n_rows, d), F32),
        scratch_shapes=[pltpu.VMEM((EXPERT_BLOCK, d), F32),
                        pltpu.SemaphoreType.DMA(()), pltpu.SemaphoreType.DMA(())],
        compiler_params=_params(1),
        name="dispatch",
    )(dest, eend, h2)


def _expert_kernel(bexp_ref, eend_ref, xs_ref, wg_ref, wu_ref, wd_ref, ys_ref, wg_b, wu_b, wd_b):
    i = pl.program_id(0)
    used = i < eend_ref[N_EXPERTS - 1]
    new_expert = jnp.logical_or(i == 0, bexp_ref[i] != bexp_ref[jnp.maximum(i - 1, 0)])

    @pl.when(jnp.logical_and(used, new_expert))
    def _():
        wg_b[...] = wg_ref[0].astype(BF16)
        wu_b[...] = wu_ref[0].astype(BF16)
        wd_b[...] = wd_ref[0].astype(BF16)

    @pl.when(used)
    def _():
        x = xs_ref[...].astype(BF16)
        hg = _dot(x, wg_b[...])
        hu = _dot(x, wu_b[...])
        hid = (hg * _sigmoid(hg) * hu).astype(BF16)
        ys_ref[...] = _dot(hid, wd_b[...])

    @pl.when(jnp.logical_not(used))
    def _():
        ys_ref[...] = jnp.zeros_like(ys_ref)


def _experts(bexp, eend, xs, e_gate, e_up, e_down, n_blocks):
    n_rows, d = xs.shape
    de = e_gate.shape[2]
    blk = EXPERT_BLOCK
    last = lambda i, nu: jnp.minimum(i, nu[N_EXPERTS - 1] - 1)
    rows = lambda i, be, nu: (last(i, nu), 0)
    wsel = lambda i, be, nu: (be[last(i, nu)], 0, 0)
    return pl.pallas_call(
        _expert_kernel,
        grid_spec=pltpu.PrefetchScalarGridSpec(
            num_scalar_prefetch=2,
            grid=(n_blocks,),
            in_specs=[pl.BlockSpec((blk, d), rows),
                      pl.BlockSpec((1, d, de), wsel),
                      pl.BlockSpec((1, d, de), wsel),
                      pl.BlockSpec((1, de, d), wsel)],
            out_specs=pl.BlockSpec((blk, d), lambda i, be, nu: (i, 0)),
            scratch_shapes=[pltpu.VMEM((d, de), BF16), pltpu.VMEM((d, de), BF16),
                            pltpu.VMEM((de, d), BF16)]),
        out_shape=jax.ShapeDtypeStruct((n_rows, d), F32),
        compiler_params=_params(1),
        name="experts",
    )(bexp, eend, xs, e_gate, e_up, e_down)


def _combine_kernel(dest_ref, dnext_ref, ys_ref, wts_ref, x1_ref, g2_ref, gf_ref, scf_ref, shf_ref, o_ref,
                    buf, sems):
    tm = x1_ref.shape[0]
    i = pl.program_id(0)
    cur = lax.rem(i, 2)

    def gather(d_ref, s):
        def start(g, carry):
            for u in range(ROW_UNROLL):
                r = g * ROW_UNROLL + u
                _row_copy(ys_ref, d_ref[0, 0, r], buf.at[s, 0], r, sems.at[s]).start(priority=u % 2)
                _row_copy(ys_ref, d_ref[0, 1, r], buf.at[s, 1], r, sems.at[s]).start(priority=(u + 1) % 2)
            return carry

        lax.fori_loop(0, tm // ROW_UNROLL, start, 0)

    @pl.when(i == 0)
    def _():
        gather(dest_ref, 0)

    @pl.when(i + 1 < pl.num_programs(0))
    def _():
        gather(dnext_ref, 1 - cur)

    all_rows = pltpu.make_async_copy(ys_ref.at[pl.ds(0, tm), :], buf.at[cur, 0], sems.at[cur])
    all_rows.wait()
    all_rows.wait()
    wts = wts_ref[...]
    moe = buf[cur, 0] * wts[:, 0:1] + buf[cur, 1] * wts[:, 1:2]
    x2 = x1_ref[...] + (1.0 + g2_ref[0]) * moe
    ms = jnp.mean(x2 * x2, axis=-1, keepdims=True)
    y = x2 * lax.rsqrt(ms + NORM_EPS) * gf_ref[...]
    o_ref[...] = y * (1.0 + scf_ref[0]) + shf_ref[0]


def _combine(dest, ys, wts_col, x1, mod, modf, gf, seq):
    nt, _, tm = dest.shape
    n, d = x1.shape
    tps = seq // tm
    return pl.pallas_call(
        _combine_kernel,
        grid=(nt,),
        in_specs=[pl.BlockSpec((1, 2, tm), lambda i: (i, 0, 0), memory_space=pltpu.SMEM),
                  pl.BlockSpec((1, 2, tm), lambda i: (jnp.minimum(i + 1, nt - 1), 0, 0),
                               memory_space=pltpu.SMEM),
                  pl.BlockSpec(memory_space=pl.ANY),
                  pl.BlockSpec((tm, 2), lambda i: (i, 0)),
                  pl.BlockSpec((tm, d), lambda i: (i, 0)),
                  pl.BlockSpec((1, 1, d), lambda i: ((i // tps) * 6 + 5, 0, 0)),
                  pl.BlockSpec((1, d), lambda i: (0, 0)),
                  pl.BlockSpec((1, 1, d), lambda i: ((i // tps) * 2 + 1, 0, 0)),
                  pl.BlockSpec((1, 1, d), lambda i: ((i // tps) * 2 + 0, 0, 0))],
        out_specs=pl.BlockSpec((tm, d), lambda i: (i, 0)),
        out_shape=jax.ShapeDtypeStruct((n, d), F32),
        scratch_shapes=[pltpu.VMEM((2, 2, tm, d), F32), pltpu.SemaphoreType.DMA((2,))],
        compiler_params=_params(1),
        name="combine",
    )(dest, dest, ys, wts_col, x1, mod, gf, modf, modf)


def kernel(x, c, ada_w, ada_b, norm1_g, w_in, shift_mu, w0, w_decay_up, a0, w_aaa_up, w_gate_up,
           k_k, k_a, r_k, gn_g, gn_b, kv_norm_g, k_idx_norm_g, w_uk, w_uv, w_out, norm2_g,
           w_group, b_group, w_expert, b_expert, e_gate, e_up, e_down,
           final_ada_w, final_ada_b, final_norm_g):
    bsz, seq, d = x.shape
    n = bsz * seq
    depth = ada_w.shape[0]
    tm = min(TOKEN_TILE, seq)
    x2 = x.reshape(n, d)

    modf = _ada(c, final_ada_w, final_ada_b).reshape(bsz * 2, 1, d)
    for l in range(depth):
        mod = _ada(c, ada_w[l], ada_b[l]).reshape(bsz * 6, 1, d)

        w1 = w_in[l][:, :RWKV_COLS].astype(BF16)
        w2 = jnp.pad(w_in[l][:, RWKV_COLS:], ((0, 0), (0, DSA_COLS_PAD - DSA_COLS))).astype(BF16)
        p_rwkv, p_dsa = _proj(x2, norm1_g[l].reshape(1, d), mod, w1, w2, seq, tm)

        zeros = jnp.zeros((DECAY_LORA, RWKV_WIDTH), F32)
        wda = jnp.concatenate([jnp.concatenate([w_decay_up[l], zeros], axis=1),
                               jnp.concatenate([zeros, w_aaa_up[l]], axis=1)], axis=0).astype(BF16)
        y_rwkv = _rwkv(p_rwkv, bsz, seq, shift_mu[l], w0[l], wda, a0[l], w_gate_up[l].astype(BF16),
                       k_k[l], k_a[l], r_k[l], gn_g[l], gn_b[l])
        y_dsa = _dsa(p_dsa, bsz, seq, kv_norm_g[l], k_idx_norm_g[l], w_uk[l], w_uv[l])

        wr = jnp.zeros((ROUTER_ROWS, d), F32)
        wr = wr.at[0:N_GROUPS].set(w_group[l].T).at[8:8 + N_EXPERTS].set(w_expert[l].T)
        br = jnp.zeros((ROUTER_ROWS, 1), F32)
        br = br.at[0:N_GROUPS, 0].set(b_group[l]).at[8:8 + N_EXPERTS, 0].set(b_expert[l])
        wo = w_out[l].astype(BF16)
        x1, h2, ids, wts = _mix(y_rwkv, y_dsa, x2, wo[:RWKV_WIDTH], wo[RWKV_WIDTH:], mod,
                                norm2_g[l].reshape(1, d), wr, br, seq, tm)

        n_blocks = (n * 2) // EXPERT_BLOCK + N_EXPERTS
        n_blocks_pad = -(-n_blocks // 128) * 128
        dest, bexp, eend = _sort(ids, n_blocks_pad)
        xs = _dispatch(dest, eend, h2, n_blocks * EXPERT_BLOCK)
        ys = _experts(bexp.reshape(-1), eend.reshape(-1), xs, e_gate[l], e_up[l], e_down[l], n_blocks)
        wts_col = jnp.transpose(wts, (0, 2, 1)).reshape(n, 2)
        last = l == depth - 1
        if not last:
            raise NotImplementedError("stacked layers need a residual-only combine")
        x2 = _combine(dest, ys, wts_col, x1, mod, modf, final_norm_g.reshape(1, d), seq)
    return x2.reshape(bsz, seq, d)
```

```python
import functools

import numpy as np
import jax
import jax.numpy as jnp
from jax import lax
from jax.experimental import pallas as pl
from jax.experimental.pallas import tpu as pltpu

F32 = jnp.float32
BF16 = jnp.bfloat16
I32 = jnp.int32

HEAD_DIM = 64
RWKV_WIDTH = 512
RWKV_HEADS = RWKV_WIDTH // HEAD_DIM
GROUP_HEADS = 2
DSA_WIDTH = 512
DSA_HEADS = DSA_WIDTH // HEAD_DIM
DECAY_LORA = 64
AAA_LORA = 64
GATE_LORA = 128
KV_LATENT = 128
IDX_HEADS = 4
IDX_DIM = 64
RWKV_COLS = 3 * RWKV_WIDTH + DECAY_LORA + AAA_LORA + GATE_LORA
DSA_COLS = DSA_WIDTH + KV_LATENT + IDX_HEADS * IDX_DIM + IDX_DIM + IDX_HEADS
DSA_COLS_PAD = 1024
TOPK_MAX = 256
ATTN_CHUNK_LOG2 = 6
Q_BLOCK = 256
RWKV_CHUNK = 64
RWKV_STEP_CHUNKS = 4
N_GROUPS = 4
EXPERTS_PER_GROUP = 8
N_EXPERTS = N_GROUPS * EXPERTS_PER_GROUP
ROUTER_ROWS = 40
NORM_EPS = 1e-6
GN_EPS = HEAD_DIM * 1e-5
NEG_BIG = -1e30
INT_MIN = -2 ** 31

VMEM_LIMIT_BYTES = 56 * 1024 * 1024
EXPERT_BLOCK = 512
TOKEN_TILE = 512

_NN = (((1,), (0,)), ((), ()))
_NT = (((1,), (1,)), ((), ()))
_TN = (((0,), (0,)), ((), ()))


def _dot(a, b, dims=_NN):
    return lax.dot_general(a, b, dims, preferred_element_type=F32)


def _split(x):
    hi = x.astype(BF16)
    lo = (x - hi.astype(F32)).astype(BF16)
    return hi, lo


def _dot3(a, b, dims=_NN):
    ah, al = _split(a)
    bh, bl = _split(b)
    return _dot(ah, bh, dims) + _dot(ah, bl, dims) + _dot(al, bh, dims)


def _dot_exact_rhs(a, b_bf16, dims=_NN):
    ah, al = _split(a)
    return _dot(ah, b_bf16, dims) + _dot(al, b_bf16, dims)


def _sigmoid(x):
    return 1.0 / (1.0 + jnp.exp(-x))


def _softplus(x):
    return jnp.maximum(x, 0.0) + jnp.log(1.0 + jnp.exp(-jnp.abs(x)))


def _params(n_axes):
    return pltpu.CompilerParams(dimension_semantics=("arbitrary",) * n_axes,
                                vmem_limit_bytes=VMEM_LIMIT_BYTES)


def _ada_kernel(c_ref, w_ref, b_ref, o_ref):
    c = c_ref[...]
    o_ref[...] = _dot3(c * _sigmoid(c), w_ref[...]) + b_ref[...]


def _ada(c, w, b):
    bsz, d = c.shape
    n = w.shape[1]
    tn = 1024
    return pl.pallas_call(
        _ada_kernel,
        grid=(n // tn,),
        in_specs=[pl.BlockSpec((bsz, d), lambda j: (0, 0)),
                  pl.BlockSpec((d, tn), lambda j: (0, j)),
                  pl.BlockSpec((1, tn), lambda j: (0, j))],
        out_specs=pl.BlockSpec((bsz, tn), lambda j: (0, j)),
        out_shape=jax.ShapeDtypeStruct((bsz, n), F32),
        compiler_params=_params(1),
        name="ada",
    )(c, w, b.reshape(1, n))


def _proj_kernel(x_ref, g_ref, sc_ref, sh_ref, w1_ref, w2_ref, o1_ref, o2_ref):
    x = x_ref[...]
    ms = jnp.mean(x * x, axis=-1, keepdims=True)
    y = x * lax.rsqrt(ms + NORM_EPS) * g_ref[...]
    h = (y * (1.0 + sc_ref[0]) + sh_ref[0]).astype(BF16)
    o1_ref[...] = _dot(h, w1_ref[...])
    o2_ref[...] = _dot(h, w2_ref[...])


def _proj(x2, g, mod, w1, w2, seq, tm):
    n, d = x2.shape
    tiles_per_seq = seq // tm
    return pl.pallas_call(
        _proj_kernel,
        grid=(n // tm,),
        in_specs=[pl.BlockSpec((tm, d), lambda i: (i, 0)),
                  pl.BlockSpec((1, d), lambda i: (0, 0)),
                  pl.BlockSpec((1, 1, d), lambda i: ((i // tiles_per_seq) * 6 + 1, 0, 0)),
                  pl.BlockSpec((1, 1, d), lambda i: ((i // tiles_per_seq) * 6 + 0, 0, 0)),
                  pl.BlockSpec(w1.shape, lambda i: (0, 0)),
                  pl.BlockSpec(w2.shape, lambda i: (0, 0))],
        out_specs=[pl.BlockSpec((tm, w1.shape[1]), lambda i: (i, 0)),
                   pl.BlockSpec((tm, w2.shape[1]), lambda i: (i, 0))],
        out_shape=[jax.ShapeDtypeStruct((n, w1.shape[1]), F32),
                   jax.ShapeDtypeStruct((n, w2.shape[1]), F32)],
        compiler_params=_params(1),
        name="proj",
    )(x2, g, mod, mod, w1, w2)


def _rwkv_kernel(p_ref, mu_ref, w0_ref, wda_ref, a0_ref, wg_ref, kk_ref, ka_ref, rk_ref,
                 gng_ref, gnb_ref, blk_ref, tril_ref, masks_ref, o_ref, s_ref, prev_ref):
    c = RWKV_CHUNK
    w = RWKV_WIDTH
    tt = p_ref.shape[0]
    chunks = range(tt // c)

    @pl.when(pl.program_id(1) == 0)
    def _():
        s_ref[...] = jnp.zeros_like(s_ref)
        prev_ref[...] = jnp.zeros_like(prev_ref)

    p = p_ref[...]
    row = lax.broadcasted_iota(I32, p.shape, 0)
    p_prev = jnp.where(row == 0, prev_ref[...], pltpu.roll(p, 1, 0))
    prev_ref[...] = p[tt - 1:tt, :]
    ps = p + mu_ref[...] * (p_prev - p)

    r = ps[:, 0:w]
    k = ps[:, w:2 * w]
    v = ps[:, 2 * w:3 * w]
    lora_in = ps[:, 3 * w:3 * w + DECAY_LORA + AAA_LORA]
    gate_in = ps[:, 3 * w + DECAY_LORA + AAA_LORA:]

    lane = lax.broadcasted_iota(I32, lora_in.shape, 1)
    lora_act = jnp.where(lane < DECAY_LORA, jnp.tanh(lora_in), lora_in).astype(BF16)
    da = _dot(lora_act, wda_ref[...])
    log_w = -_softplus(-(w0_ref[...] + da[:, :w])) - 0.5
    ld = -jnp.exp(log_w)
    a = _sigmoid(a0_ref[...] + da[:, w:])
    gate = _dot(_sigmoid(gate_in).astype(BF16), wg_ref[...])

    blk = blk_ref[...]

    def head_sums(z):
        nb = w // blk.shape[0]
        rows = jnp.concatenate([z[:, i * 128:(i + 1) * 128] for i in range(nb)], axis=0)
        sums = _dot(rows.astype(BF16), blk)
        return jnp.concatenate([sums[i * tt:(i + 1) * tt] for i in range(nb)], axis=1)

    kk = k * kk_ref[...]
    kk = kk / jnp.maximum(jnp.sqrt(head_sums(kk * kk)), 1e-12)
    k2 = k * (1.0 + (a - 1.0) * ka_ref[...])

    cs = _dot_exact_rhs_lhs(tril_ref[...], ld)
    cs_last = [cs[(ch + 1) * c - 1:(ch + 1) * c, :] for ch in chunks]
    cs_end = jnp.concatenate([jnp.broadcast_to(cl, (c, w)) for cl in cs_last], axis=0)
    e_neg = jnp.exp(-cs)
    e_rem = jnp.exp(cs_end - cs)
    kka = kk * a
    a_t = (-kk) * jnp.exp(cs - ld)
    b_t = kka * e_neg
    k_t = k2 * e_neg
    r_t = r * jnp.exp(cs)
    b_h = kka * e_rem
    k_h = k2 * e_rem
    w_c = [jnp.exp(cl) for cl in cs_last]

    bd = masks_ref[0]
    strict = masks_ref[1]
    incl = masks_ref[2]
    eye = masks_ref[3]
    gw = GROUP_HEADS * HEAD_DIM
    stack = lambda z: jnp.concatenate([z] * GROUP_HEADS, axis=0)

    groups = range(RWKV_HEADS // GROUP_HEADS)
    units = [(ch, g) for ch in chunks for g in groups]
    piece = lambda z, u: z[u[0] * c:(u[0] + 1) * c, u[1] * gw:(u[1] + 1) * gw]
    v_x = [(stack(piece(v, u)) * bd).astype(BF16) for u in units]
    ar_x = [jnp.concatenate([(stack(piece(a_t, u)) * bd).astype(BF16),
                             (stack(piece(r_t, u)) * bd).astype(BF16)], axis=0) for u in units]
    bk_r = [jnp.concatenate([stack(piece(b_t, u).astype(BF16)), stack(piece(k_t, u).astype(BF16))], axis=0)
            for u in units]
    bk_h = [jnp.concatenate([stack(piece(b_h, u).astype(BF16)), stack(piece(k_h, u).astype(BF16))], axis=0)
            for u in units]
    quad = [_dot(x, y, _NT) for x, y in zip(ar_x, bk_r)]
    l_ab = [q4[:gw, :gw] * strict for q4 in quad]
    l_ak = [(q4[:gw, gw:] * strict).astype(BF16) for q4 in quad]
    m_cat = [jnp.concatenate([(q4[gw:, :gw] * incl).astype(BF16), (q4[gw:, gw:] * incl).astype(BF16)], axis=1)
             for q4 in quad]

    pw = [p.astype(BF16) for p in l_ab]
    t_inv = [eye + l for l in l_ab]
    pw = [_dot(p, p) for p in pw]
    for _ in range(4):
        pb = [p.astype(BF16) for p in pw]
        both = [_dot(jnp.concatenate([t.astype(BF16), p], axis=0), p) for t, p in zip(t_inv, pb)]
        t_inv = [t + tp[:gw] for t, tp in zip(t_inv, both)]
        pw = [tp[gw:] for tp in both]
    t_inv = [(t + _dot(t.astype(BF16), p.astype(BF16))).astype(BF16) for t, p in zip(t_inv, pw)]

    state = [s_ref[g] for g in groups]
    y_rows = []
    for ch in chunks:
        us = [ch * len(groups) + g for g in groups]
        ar_s = [_dot(ar_x[u], state[g].astype(BF16), _NT) for g, u in zip(groups, us)]
        rhs = [ar_s[g][:gw] + _dot(l_ak[u], v_x[u]) for g, u in zip(groups, us)]
        ub = [_dot(t_inv[u], rhs[g].astype(BF16)).astype(BF16) for g, u in zip(groups, us)]
        uv = [jnp.concatenate([ub[g], v_x[u]], axis=0) for g, u in zip(groups, us)]
        y_x = [ar_s[g][gw:] + _dot(m_cat[u], uv[g]) for g, u in zip(groups, us)]
        y_rows.append(jnp.concatenate(
            [sum(yx[hh * c:(hh + 1) * c] for hh in range(1, GROUP_HEADS)) + yx[0:c] for yx in y_x], axis=1))
        state = [(state[g] * w_c[ch][:, g * gw:(g + 1) * gw] + _dot(uv[g], bk_h[u], _TN)) * bd
                 for g, u in zip(groups, us)]
    for g in groups:
        s_ref[g] = state[g]

    y = jnp.concatenate(y_rows, axis=0)
    inv_n = 1.0 / HEAD_DIM
    mean = head_sums(y) * inv_n
    dlt = y - mean
    var = head_sums(dlt * dlt) * inv_n
    yn = dlt * lax.rsqrt(var + GN_EPS) * gng_ref[...] + gnb_ref[...]
    bonus = head_sums(r * k2 * rk_ref[...]) * v
    o_ref[...] = ((yn + bonus) * gate).astype(o_ref.dtype)


def _dot_exact_rhs_lhs(a_bf16, b):
    bh, bl = _split(b)
    return _dot(a_bf16, bh) + _dot(a_bf16, bl)


def _rwkv(p, bsz, seq, mu, w0, wda, a0, wg, k_k, k_a, r_k, gn_g, gn_b):
    c = RWKV_CHUNK
    w = RWKV_WIDTH
    tt = c * RWKV_STEP_CHUNKS if seq % (c * RWKV_STEP_CHUNKS) == 0 else c
    n_steps = seq // tt
    head_of = np.arange(128) // HEAD_DIM
    blk = jnp.asarray(head_of[:, None] == head_of[None, :], BF16)
    ti = np.arange(tt)
    tril = jnp.asarray((ti[:, None] >= ti[None, :]) & (ti[:, None] // c == ti[None, :] // c), BF16)
    assert c == HEAD_DIM and RWKV_HEADS % GROUP_HEADS == 0
    gi = np.arange(GROUP_HEADS * c)
    same = (gi[:, None] // c) == (gi[None, :] // c)
    later = (gi[:, None] % c) > (gi[None, :] % c)
    masks = jnp.asarray(np.stack([same, same & later, same & (later | (gi[:, None] % c == gi[None, :] % c)),
                                  gi[:, None] == gi[None, :]]), F32)
    row = lambda a: a.reshape(1, -1)
    vec = lambda n: pl.BlockSpec((1, n), lambda b, t: (0, 0))
    full = lambda a: pl.BlockSpec(a.shape, lambda b, t: (0, 0))
    return pl.pallas_call(
        _rwkv_kernel,
        grid=(bsz, n_steps),
        in_specs=[pl.BlockSpec((tt, RWKV_COLS), lambda b, t: (b * n_steps + t, 0)),
                  vec(RWKV_COLS), vec(w), full(wda), vec(w), full(wg), vec(w), vec(w), vec(w),
                  vec(w), vec(w), full(blk), full(tril),
                  pl.BlockSpec(masks.shape, lambda b, t: (0, 0, 0))],
        out_specs=pl.BlockSpec((tt, w), lambda b, t: (b * n_steps + t, 0)),
        out_shape=jax.ShapeDtypeStruct((bsz * seq, w), BF16),
        scratch_shapes=[pltpu.VMEM((RWKV_HEADS // GROUP_HEADS, GROUP_HEADS * HEAD_DIM, GROUP_HEADS * HEAD_DIM), F32),
                        pltpu.VMEM((1, RWKV_COLS), F32)],
        compiler_params=_params(2),
        name="rwkv",
    )(p, row(mu), row(w0), wda, row(a0), wg, row(k_k), row(k_a), row(r_k), row(gn_g), row(gn_b),
      blk, tril, masks)


SLAB_ROWS = 64
LOW_BITS = 8
ATTN_HEAD_GROUP = 8
ONES_ROWS = 16


def _fold_rows(x, op):
    acc = x[0:SLAB_ROWS]
    for r0 in range(SLAB_ROWS, x.shape[0], SLAB_ROWS):
        acc = op(acc, x[r0:r0 + SLAB_ROWS])
    return acc


def _split3(x):
    hi = x.astype(BF16)
    rest = x - hi.astype(F32)
    mid = rest.astype(BF16)
    return hi, mid, (rest - mid.astype(F32)).astype(BF16)


def _dsa_kernel(pd_ref, kvg_ref, kig_ref, wuk_ref, wuv_ref, tril_ref, wsel_ref, o_ref,
                ckv_s, ckvt_s, kcat_s, score_s, *, nq, n_cls, topk):
    qb = Q_BLOCK
    j = pl.program_id(1)

    @pl.when(j == 0)
    def _():
        ckv_s[...] = jnp.zeros_like(ckv_s)
        ckvt_s[...] = jnp.zeros_like(ckvt_s)
        kcat_s[...] = jnp.zeros_like(kcat_s)

    pd = pd_ref[...]
    q = pd[:, :DSA_WIDTH]
    ckv = pd[:, DSA_WIDTH:DSA_WIDTH + KV_LATENT]
    qi = pd[:, DSA_WIDTH + KV_LATENT:DSA_WIDTH + KV_LATENT + IDX_HEADS * IDX_DIM]
    tail = pd[:, DSA_WIDTH + KV_LATENT + IDX_HEADS * IDX_DIM:]
    ki = tail[:, :IDX_DIM]

    ckv_n = ckv * lax.rsqrt(jnp.mean(ckv * ckv, axis=-1, keepdims=True) + NORM_EPS) * kvg_ref[...]
    ki_n = ki * lax.rsqrt(jnp.mean(ki * ki, axis=-1, keepdims=True) + NORM_EPS) * kig_ref[...]
    ki_hi, ki_lo = _split(ki_n)
    rows = pl.ds(pl.multiple_of(j * qb, qb), qb)
    ckv_s[rows, :] = ckv_n.astype(BF16)
    one_row = jnp.where(lax.broadcasted_iota(I32, (ONES_ROWS, qb), 0) == 0, 1.0, 0.0)
    ckvt_s[j] = jnp.concatenate([ckv_n.T, one_row], axis=0).astype(BF16)
    kcat_s[rows, :] = jnp.concatenate([ki_hi, ki_hi, ki_lo, jnp.zeros_like(ki_hi)], axis=1)

    w_t = sum(_dot(wsel_ref[...], part, _NT) for part in _split3(tail))

    per = nq // n_cls
    for cls in range(n_cls):
        @pl.when((j >= cls * per) & (j < (cls + 1) * per))
        def _(sk=(cls + 1) * per * qb):
            _dsa_block(j, q, qi, w_t, wuk_ref, wuv_ref, tril_ref, o_ref, ckv_s, ckvt_s, kcat_s,
                       score_s, sk=sk, topk=topk)


def _dsa_block(j, q, qi, w_t, wuk_ref, wuv_ref, tril_ref, o_ref, ckv_s, ckvt_s, kcat_s, score_s,
               *, sk, topk):
    qb = Q_BLOCK
    kcat = kcat_s[0:sk, :]
    score = jnp.zeros((sk, qb), F32)
    w_s = w_t * (IDX_HEADS ** -0.5)
    for hh in range(IDX_HEADS):
        q_hi, q_lo = _split(qi[:, hh * IDX_DIM:(hh + 1) * IDX_DIM] * (IDX_DIM ** -0.5))
        dots = _dot(kcat, jnp.concatenate([q_hi, q_lo, q_hi, jnp.zeros_like(q_hi)], axis=1), _NT)
        score = score + w_s[hh:hh + 1, :] * jnp.maximum(dots, 0.0)

    key_pos = lax.broadcasted_iota(I32, (sk, qb), 0)
    q_pos = j * qb + lax.broadcasted_iota(I32, (sk, qb), 1)
    adm = jnp.right_shift(key_pos, ATTN_CHUNK_LOG2) <= jnp.right_shift(q_pos, ATTN_CHUNK_LOG2)

    score_s[0:sk, :] = jnp.where(adm, score, NEG_BIG)

    def as_float(okey):
        return lax.bitcast_convert_type(okey ^ ((okey >> 31) & 0x7FFFFFFF), F32)

    def count_ge(cand):
        acc = jnp.zeros((SLAB_ROWS, qb), F32)
        for r0 in range(0, sk, SLAB_ROWS):
            acc = acc + jnp.where(score_s[r0:r0 + SLAB_ROWS, :] >= cand, 1.0, 0.0)
        return jnp.sum(acc, axis=0, keepdims=True)

    def descend(i, tu):
        cand_u = tu | jnp.left_shift(jnp.int32(1), 31 - i)
        return jnp.where(count_ge(as_float(cand_u ^ INT_MIN)) >= topk, cand_u, tu)

    if sk <= topk:
        thr = jnp.full((1, qb), NEG_BIG, F32)
    else:
        high = 32 - LOW_BITS
        tu = lax.fori_loop(0, high, descend, jnp.zeros((1, qb), I32))
        end = as_float((tu + (1 << LOW_BITS)) ^ INT_MIN)
        best = jnp.full((SLAB_ROWS, qb), -jnp.inf, F32)
        for r0 in range(0, sk, SLAB_ROWS):
            slab = score_s[r0:r0 + SLAB_ROWS, :]
            best = jnp.maximum(best, jnp.where(slab < end, slab, -jnp.inf))
        guess = jnp.max(best, axis=0, keepdims=True)
        settled = (count_ge(guess) >= topk) & (guess > -jnp.inf)
        missed = jnp.max(jnp.where(settled, 0, 1))
        tu = lax.fori_loop(high, high + LOW_BITS * missed, descend, tu)
        thr = jnp.where(missed == 1, as_float(tu ^ INT_MIN), guess)

    score = score_s[0:sk, :]
    gt = score > thr
    eq = score == thr
    need = topk - jnp.sum(_fold_rows(jnp.where(gt, 1.0, 0.0), jnp.add), axis=0, keepdims=True)
    eq_b = jnp.where(eq, 1.0, 0.0).astype(BF16)
    tb = tril_ref.shape[0]
    off = jnp.zeros((1, qb), F32)
    pieces = []
    for kb in range(sk // tb):
        pre = _dot(tril_ref[...], eq_b[kb * tb:(kb + 1) * tb, :])
        pieces.append(pre + off)
        off = off + pre[tb - 1:tb, :]
    prefix = jnp.concatenate(pieces, axis=0)
    tie_bias = jnp.where(prefix <= need, 0.0, NEG_BIG)
    bias = jnp.where(gt, 0.0, jnp.where(eq, tie_bias, NEG_BIG))
    bias = jnp.where(adm, bias, NEG_BIG)

    ckv_all = ckv_s[0:sk, :]
    ckv_t = jnp.concatenate([ckvt_s[b] for b in range(sk // qb)], axis=1)
    q_lat = (_dot(q.astype(BF16), wuk_ref[...]) * (HEAD_DIM ** -0.5)).astype(BF16)
    head_logits = lambda h: _dot(ckv_all, q_lat[:, h * KV_LATENT:(h + 1) * KV_LATENT], _NT)
    outs = []
    nxt = head_logits(0)
    for h in range(DSA_HEADS):
        logits = nxt + bias
        if h + 1 < DSA_HEADS:
            nxt = head_logits(h + 1)
        mx = jnp.max(_fold_rows(logits, jnp.maximum), axis=0, keepdims=True)
        pr = jnp.exp((logits - mx).astype(BF16))
        acc = _dot(ckv_t, pr)
        outs.append((acc[:KV_LATENT] / acc[KV_LATENT:KV_LATENT + 1]).astype(BF16))
    o_lat_t = jnp.concatenate(outs, axis=0)
    o_ref[...] = _dot(o_lat_t, wuv_ref[...], _TN).astype(o_ref.dtype)


def _dsa(pd, bsz, seq, kv_g, ki_g, w_uk, w_uv):
    qb = Q_BLOCK
    nq = seq // qb
    topk = min(TOPK_MAX, seq // 4)
    hc = DSA_HEADS * KV_LATENT
    head_of_q = np.arange(DSA_WIDTH) // HEAD_DIM
    head_of_l = np.arange(hc) // KV_LATENT
    mask = jnp.asarray(head_of_q[:, None] == head_of_l[None, :], F32)
    uk = jnp.transpose(w_uk, (1, 2, 0)).reshape(DSA_WIDTH, KV_LATENT)
    wuk_bd = (jnp.tile(uk, (1, DSA_HEADS)) * mask).astype(BF16)
    uv = w_uv.reshape(KV_LATENT, DSA_WIDTH)
    wuv_bd = (jnp.tile(uv, (DSA_HEADS, 1)) * mask.T).astype(BF16)
    n_cls = max(d for d in (8, 4, 2, 1) if nq % d == 0)
    tb = 256 if ((nq // n_cls) * qb) % 256 == 0 else 128
    tril = jnp.asarray(np.tril(np.ones((tb, tb), np.float32)), BF16)
    wsel = jnp.asarray(np.arange(128)[None, :] == (IDX_DIM + np.arange(8))[:, None], BF16)
    wsel = wsel * jnp.asarray(np.arange(8)[:, None] < IDX_HEADS, BF16)
    full = lambda a: pl.BlockSpec(a.shape, lambda b, j: (0, 0))
    return pl.pallas_call(
        functools.partial(_dsa_kernel, nq=nq, n_cls=n_cls, topk=topk),
        grid=(bsz, nq),
        in_specs=[pl.BlockSpec((qb, DSA_COLS_PAD), lambda b, j: (b * nq + j, 0)),
                  pl.BlockSpec((1, KV_LATENT), lambda b, j: (0, 0)),
                  pl.BlockSpec((1, IDX_DIM), lambda b, j: (0, 0)),
                  full(wuk_bd), full(wuv_bd), full(tril), full(wsel)],
        out_specs=pl.BlockSpec((qb, DSA_WIDTH), lambda b, j: (b * nq + j, 0)),
        out_shape=jax.ShapeDtypeStruct((bsz * seq, DSA_WIDTH), BF16),
        scratch_shapes=[pltpu.VMEM((seq, KV_LATENT), BF16),
                        pltpu.VMEM((nq, KV_LATENT + ONES_ROWS, qb), BF16),
                        pltpu.VMEM((seq, 4 * IDX_DIM), BF16),
                        pltpu.VMEM((seq, qb), F32)],
        compiler_params=_params(2),
        name="dsa",
    )(pd, kv_g.reshape(1, -1), ki_g.reshape(1, -1), wuk_bd, wuv_bd, tril, wsel)


def _dsa_flash_kernel(pd_ref, kvg_ref, kig_ref, wuk_ref, wuv_ref, tril_ref, wsel_ref, o_ref,
                      ckv_s, ckvt_s, kcat_s, score_s, bias_s, acc_s, m_s, *, topk):
    qb = Q_BLOCK
    kb_rows = Q_BLOCK
    j = pl.program_id(1)
    n_kb = j + 1

    pd = pd_ref[...]
    q = pd[:, :DSA_WIDTH]
    ckv = pd[:, DSA_WIDTH:DSA_WIDTH + KV_LATENT]
    qi = pd[:, DSA_WIDTH + KV_LATENT:DSA_WIDTH + KV_LATENT + IDX_HEADS * IDX_DIM]
    tail = pd[:, DSA_WIDTH + KV_LATENT + IDX_HEADS * IDX_DIM:]
    ki = tail[:, :IDX_DIM]

    ckv_n = ckv * lax.rsqrt(jnp.mean(ckv * ckv, axis=-1, keepdims=True) + NORM_EPS) * kvg_ref[...]
    ki_n = ki * lax.rsqrt(jnp.mean(ki * ki, axis=-1, keepdims=True) + NORM_EPS) * kig_ref[...]
    ki_hi, ki_lo = _split(ki_n)
    rows = pl.ds(pl.multiple_of(j * qb, qb), qb)
    ckv_s[rows, :] = ckv_n.astype(BF16)
    one_row = jnp.where(lax.broadcasted_iota(I32, (ONES_ROWS, qb), 0) == 0, 1.0, 0.0)
    ckvt_s[j] = jnp.concatenate([ckv_n.T, one_row], axis=0).astype(BF16)
    kcat_s[rows, :] = jnp.concatenate([ki_hi, ki_hi, ki_lo, jnp.zeros_like(ki_hi)], axis=1)

    w_s = sum(_dot(wsel_ref[...], part, _NT) for part in _split3(tail)) * (IDX_HEADS ** -0.5)
    q_cat = []
    for hh in range(IDX_HEADS):
        q_hi, q_lo = _split(qi[:, hh * IDX_DIM:(hh + 1) * IDX_DIM] * (IDX_DIM ** -0.5))
        q_cat.append(jnp.concatenate([q_hi, q_lo, q_hi, jnp.zeros_like(q_hi)], axis=1))

    def key_rows(kb):
        return pl.ds(pl.multiple_of(kb * kb_rows, kb_rows), kb_rows)

    def admissible(kb):
        key_pos = kb * kb_rows + lax.broadcasted_iota(I32, (kb_rows, qb), 0)
        q_pos = j * qb + lax.broadcasted_iota(I32, (kb_rows, qb), 1)
        return jnp.right_shift(key_pos, ATTN_CHUNK_LOG2) <= jnp.right_shift(q_pos, ATTN_CHUNK_LOG2)

    def score_block(kb, carry):
        kcat = kcat_s[key_rows(kb), :]
        dots = [_dot(kcat, q_cat[hh], _NT) for hh in range(IDX_HEADS)]
        score = sum(w_s[hh:hh + 1, :] * jnp.maximum(dots[hh], 0.0) for hh in range(1, IDX_HEADS)) \
            + w_s[0:1, :] * jnp.maximum(dots[0], 0.0)
        score_s[key_rows(kb), :] = jnp.where(admissible(kb), score, NEG_BIG)
        return carry

    lax.fori_loop(0, n_kb, score_block, 0)

    def as_float(okey):
        return lax.bitcast_convert_type(okey ^ ((okey >> 31) & 0x7FFFFFFF), F32)

    def count_where(pred):
        def block(kb, acc):
            base = kb * kb_rows
            for r0 in range(0, kb_rows, SLAB_ROWS):
                slab = score_s[pl.ds(pl.multiple_of(base + r0, SLAB_ROWS), SLAB_ROWS), :]
                acc = acc + jnp.where(pred(slab), 1.0, 0.0)
            return acc
        acc = lax.fori_loop(0, n_kb, block, jnp.zeros((SLAB_ROWS, qb), F32))
        return jnp.sum(acc, axis=0, keepdims=True)

    def descend(i, tu):
        cand_u = tu | jnp.left_shift(jnp.int32(1), 31 - i)
        cand = as_float(cand_u ^ INT_MIN)
        return jnp.where(count_where(lambda s: s >= cand) >= topk, cand_u, tu)

    few_keys = n_kb * kb_rows <= topk
    high = 32 - LOW_BITS
    tu = lax.fori_loop(0, jnp.where(few_keys, 0, high), descend, jnp.zeros((1, qb), I32))
    end = as_float((tu + (1 << LOW_BITS)) ^ INT_MIN)

    def best_below(kb, best):
        base = kb * kb_rows
        for r0 in range(0, kb_rows, SLAB_ROWS):
            slab = score_s[pl.ds(pl.multiple_of(base + r0, SLAB_ROWS), SLAB_ROWS), :]
            best = jnp.maximum(best, jnp.where(slab < end, slab, -jnp.inf))
        return best

    guess = jnp.max(lax.fori_loop(0, n_kb, best_below, jnp.full((SLAB_ROWS, qb), -jnp.inf, F32)),
                    axis=0, keepdims=True)
    settled = (count_where(lambda s: s >= guess) >= topk) & (guess > -jnp.inf)
    missed = jnp.max(jnp.where(settled, 0, 1))
    tu = lax.fori_loop(high, jnp.where(few_keys, high, high + LOW_BITS * missed), descend, tu)
    thr = jnp.where(few_keys, NEG_BIG, jnp.where(missed == 1, as_float(tu ^ INT_MIN), guess))
    need = topk - count_where(lambda s: s > thr)

    def bias_block(kb, tie_count):
        score = score_s[key_rows(kb), :]
        eq = score == thr
        prefix = _dot(tril_ref[...], jnp.where(eq, 1.0, 0.0).astype(BF16)) + tie_count
        tie_bias = jnp.where(eq, jnp.where(prefix <= need, 0.0, NEG_BIG), NEG_BIG)
        bias_s[key_rows(kb), :] = jnp.where(admissible(kb), jnp.where(score > thr, 0.0, tie_bias), NEG_BIG)
        return prefix[kb_rows - 1:kb_rows, :]

    lax.fori_loop(0, n_kb, bias_block, jnp.zeros((1, qb), F32))

    q_lat = (_dot(q.astype(BF16), wuk_ref[...]) * (HEAD_DIM ** -0.5)).astype(BF16)
    acc_s[...] = jnp.zeros_like(acc_s)
    m_s[...] = jnp.full(m_s.shape, NEG_BIG, F32)

    def attend_block(kb, carry):
        ckv_kb = ckv_s[key_rows(kb), :]
        ckvt_kb = ckvt_s[kb]
        bias = bias_s[key_rows(kb), :]
        for h0 in range(0, DSA_HEADS, ATTN_HEAD_GROUP):
            hs = range(h0, h0 + ATTN_HEAD_GROUP)
            logits = [_dot(ckv_kb, q_lat[:, h * KV_LATENT:(h + 1) * KV_LATENT], _NT) + bias for h in hs]
            m_old = [m_s[h] for h in hs]
            m_new = [jnp.maximum(mo, jnp.max(_fold_rows(lg, jnp.maximum), axis=0, keepdims=True))
                     for mo, lg in zip(m_old, logits)]
            pr = [jnp.exp((lg - mn).astype(BF16)) for lg, mn in zip(logits, m_new)]
            for i, h in enumerate(hs):
                acc_s[h] = acc_s[h] * jnp.exp(m_old[i] - m_new[i]) + _dot(ckvt_kb, pr[i])
                m_s[h] = m_new[i]
        return carry

    lax.fori_loop(0, n_kb, attend_block, 0)

    o_lat_t = jnp.concatenate(
        [(acc_s[h][:KV_LATENT] / acc_s[h][KV_LATENT:KV_LATENT + 1]).astype(BF16) for h in range(DSA_HEADS)],
        axis=0)
    o_ref[...] = _dot(o_lat_t, wuv_ref[...], _TN).astype(o_ref.dtype)


def _dsa_flash(pd, bsz, seq, kv_g, ki_g, w_uk, w_uv):
    qb = min(Q_BLOCK, seq)
    assert qb == Q_BLOCK and seq % qb == 0
    nq = seq // qb
    topk = min(TOPK_MAX, seq // 4)
    hc = DSA_HEADS * KV_LATENT
    head_of_q = np.arange(DSA_WIDTH) // HEAD_DIM
    head_of_l = np.arange(hc) // KV_LATENT
    mask = jnp.asarray(head_of_q[:, None] == head_of_l[None, :], F32)
    uk = jnp.transpose(w_uk, (1, 2, 0)).reshape(DSA_WIDTH, KV_LATENT)
    wuk_bd = (jnp.tile(uk, (1, DSA_HEADS)) * mask).astype(BF16)
    uv = w_uv.reshape(KV_LATENT, DSA_WIDTH)
    wuv_bd = (jnp.tile(uv, (DSA_HEADS, 1)) * mask.T).astype(BF16)
    tril = jnp.asarray(np.tril(np.ones((qb, qb), np.float32)), BF16)
    wsel = jnp.asarray((np.arange(128)[None, :] == (IDX_DIM + np.arange(8))[:, None])
                       & (np.arange(8)[:, None] < IDX_HEADS), BF16)
    full = lambda a: pl.BlockSpec(a.shape, lambda b, j: (0, 0))
    return pl.pallas_call(
        functools.partial(_dsa_flash_kernel, topk=topk),
        grid=(bsz, nq),
        in_specs=[pl.BlockSpec((qb, DSA_COLS_PAD), lambda b, j: (b * nq + j, 0)),
                  pl.BlockSpec((1, KV_LATENT), lambda b, j: (0, 0)),
                  pl.BlockSpec((1, IDX_DIM), lambda b, j: (0, 0)),
                  full(wuk_bd), full(wuv_bd), full(tril), full(wsel)],
        out_specs=pl.BlockSpec((qb, DSA_WIDTH), lambda b, j: (b * nq + j, 0)),
        out_shape=jax.ShapeDtypeStruct((bsz * seq, DSA_WIDTH), BF16),
        scratch_shapes=[pltpu.VMEM((seq, KV_LATENT), BF16),
                        pltpu.VMEM((nq, KV_LATENT + ONES_ROWS, qb), BF16),
                        pltpu.VMEM((seq, 4 * IDX_DIM), BF16),
                        pltpu.VMEM((seq, qb), F32),
                        pltpu.VMEM((seq, qb), F32),
                        pltpu.VMEM((DSA_HEADS, KV_LATENT + ONES_ROWS, qb), F32),
                        pltpu.VMEM((DSA_HEADS, 1, qb), F32)],
        compiler_params=_params(2),
        name="dsa",
    )(pd, kv_g.reshape(1, -1), ki_g.reshape(1, -1), wuk_bd, wuv_bd, tril, wsel)


def _mix_kernel(yr_ref, yd_ref, x_ref, wo1_ref, wo2_ref, g1_ref, n2_ref, sc_ref, sh_ref,
                wr_ref, br_ref, x1_ref, h2_ref, ids_ref, wts_ref):
    mixed = _dot(yr_ref[...], wo1_ref[...]) + _dot(yd_ref[...], wo2_ref[...])
    x1 = x_ref[...] + (1.0 + g1_ref[0]) * mixed
    x1_ref[...] = x1
    ms = jnp.mean(x1 * x1, axis=-1, keepdims=True)
    h2 = x1 * lax.rsqrt(ms + NORM_EPS) * n2_ref[...] * (1.0 + sc_ref[0]) + sh_ref[0]
    h2_ref[...] = h2

    lt = _dot3(wr_ref[...], h2, _NT) + br_ref[...]
    gl = lt[0:N_GROUPS]
    el = lt[8:8 + N_EXPERTS]
    tm = gl.shape[1]
    gmax = jnp.max(gl, axis=0, keepdims=True)
    gidx = lax.broadcasted_iota(I32, (N_GROUPS, tm), 0)
    gsel = jnp.min(jnp.where(gl == gmax, gidx, N_GROUPS), axis=0, keepdims=True)
    p_group = 1.0 / jnp.sum(jnp.exp(gl - gmax), axis=0, keepdims=True)
    eidx = lax.broadcasted_iota(I32, (N_EXPERTS, tm), 0)
    el = jnp.where(jnp.right_shift(eidx, 3) == gsel, el, NEG_BIG)
    m1 = jnp.max(el, axis=0, keepdims=True)
    i1 = jnp.min(jnp.where(el == m1, eidx, N_EXPERTS), axis=0, keepdims=True)
    el2 = jnp.where(eidx == i1, NEG_BIG, el)
    m2 = jnp.max(el2, axis=0, keepdims=True)
    i2 = jnp.min(jnp.where(el2 == m2, eidx, N_EXPERTS), axis=0, keepdims=True)
    e2 = jnp.exp(m2 - m1)
    w1 = p_group / (1.0 + e2)
    ids_ref[0] = jnp.concatenate([i1, i2], axis=0)
    wts_ref[0] = jnp.concatenate([w1, w1 * e2], axis=0)


def _mix(yr, yd, x2, wo1, wo2, mod, n2g, wr, br, seq, tm):
    n, d = x2.shape
    nt = n // tm
    tps = seq // tm
    modspec = lambda k: pl.BlockSpec((1, 1, d), lambda i: ((i // tps) * 6 + k, 0, 0))
    full = lambda a: pl.BlockSpec(a.shape, lambda i: (0,) * a.ndim)
    return pl.pallas_call(
        _mix_kernel,
        grid=(nt,),
        in_specs=[pl.BlockSpec((tm, yr.shape[1]), lambda i: (i, 0)),
                  pl.BlockSpec((tm, yd.shape[1]), lambda i: (i, 0)),
                  pl.BlockSpec((tm, d), lambda i: (i, 0)),
                  full(wo1), full(wo2), modspec(2), full(n2g), modspec(4), modspec(3),
                  full(wr), full(br)],
        out_specs=[pl.BlockSpec((tm, d), lambda i: (i, 0)),
                   pl.BlockSpec((tm, d), lambda i: (i, 0)),
                   pl.BlockSpec((1, 2, tm), lambda i: (i, 0, 0)),
                   pl.BlockSpec((1, 2, tm), lambda i: (i, 0, 0))],
        out_shape=[jax.ShapeDtypeStruct((n, d), F32),
                   jax.ShapeDtypeStruct((n, d), F32),
                   jax.ShapeDtypeStruct((nt, 2, tm), I32),
                   jax.ShapeDtypeStruct((nt, 2, tm), F32)],
        compiler_params=_params(1),
        name="mix",
    )(yr, yd, x2, wo1, wo2, mod, n2g, mod, mod, wr, br)


def _sort_kernel(ids_ref, triu_ref, tril_ref, dest_ref, bexp_ref, eend_ref,
                 cnt_ref, run_ref, start_ref, *, n_blocks_pad):
    phase = pl.program_id(0)
    i = pl.program_id(1)
    ids = ids_ref[0]
    tm = ids.shape[1]
    eidx = lax.broadcasted_iota(I32, (N_EXPERTS, tm), 0)
    hit0 = eidx == ids[0:1]
    hit1 = eidx == ids[1:2]
    onehot = jnp.where(hit0, 1.0, 0.0) + jnp.where(hit1, 1.0, 0.0)

    @pl.when((phase == 0) & (i == 0))
    def _():
        cnt_ref[...] = jnp.zeros_like(cnt_ref)

    @pl.when(phase == 0)
    def _():
        cnt_ref[...] += jnp.sum(onehot, axis=1, keepdims=True)

    @pl.when((phase == 1) & (i == 0))
    def _():
        run_ref[...] = jnp.zeros_like(run_ref)
        nblk = jnp.floor((cnt_ref[...] + (EXPERT_BLOCK - 1)) * (1.0 / EXPERT_BLOCK))
        nblk_b = jnp.broadcast_to(nblk, (N_EXPERTS, 128))
        first_blk = _dot_exact_rhs_lhs(tril_ref[...], nblk_b)
        start_ref[...] = first_blk[:, 0:1] * EXPERT_BLOCK
        end_blk = first_blk + nblk_b
        bidx = lax.broadcasted_iota(I32, (N_EXPERTS, n_blocks_pad), 1).astype(F32)
        owner = jnp.sum(jnp.where(end_blk[:, 0:1] <= bidx, 1.0, 0.0), axis=0, keepdims=True)
        bexp_ref[...] = jnp.minimum(owner, N_EXPERTS - 1).astype(I32)
        on_diag = (lax.broadcasted_iota(I32, (N_EXPERTS, 128), 0)
                   == lax.broadcasted_iota(I32, (N_EXPERTS, 128), 1))
        eend_ref[...] = jnp.sum(jnp.where(on_diag, end_blk, 0.0), axis=0, keepdims=True).astype(I32)

    @pl.when(phase == 1)
    def _():
        before = _dot(onehot.astype(BF16), triu_ref[...])
        pos = start_ref[...] + run_ref[...] + before
        d0 = jnp.sum(jnp.where(hit0, pos, 0.0), axis=0, keepdims=True)
        d1 = jnp.sum(jnp.where(hit1, pos, 0.0), axis=0, keepdims=True)
        dest_ref[0] = jnp.concatenate([d0, d1], axis=0).astype(I32)
        run_ref[...] += jnp.sum(onehot, axis=1, keepdims=True)


def _sort(ids, n_blocks_pad):
    nt, _, tm = ids.shape
    triu = jnp.asarray(np.triu(np.ones((tm, tm), np.float32), 1), BF16)
    tril = jnp.asarray(np.tril(np.ones((N_EXPERTS, N_EXPERTS), np.float32), -1), BF16)
    return pl.pallas_call(
        functools.partial(_sort_kernel, n_blocks_pad=n_blocks_pad),
        grid=(2, nt),
        in_specs=[pl.BlockSpec((1, 2, tm), lambda p, i: (i, 0, 0)),
                  pl.BlockSpec(triu.shape, lambda p, i: (0, 0)),
                  pl.BlockSpec(tril.shape, lambda p, i: (0, 0))],
        out_specs=[pl.BlockSpec((1, 2, tm), lambda p, i: (i * p, 0, 0)),
                   pl.BlockSpec((1, n_blocks_pad), lambda p, i: (0, 0)),
                   pl.BlockSpec((1, 128), lambda p, i: (0, 0))],
        out_shape=[jax.ShapeDtypeStruct((nt, 2, tm), I32),
                   jax.ShapeDtypeStruct((1, n_blocks_pad), I32),
                   jax.ShapeDtypeStruct((1, 128), I32)],
        scratch_shapes=[pltpu.VMEM((N_EXPERTS, 1), F32),
                        pltpu.VMEM((N_EXPERTS, 1), F32),
                        pltpu.VMEM((N_EXPERTS, 1), F32)],
        compiler_params=_params(2),
        name="sort",
    )(ids, triu, tril)


def _row_copy(src_ref, src_row, dst_ref, dst_row, sem):
    return pltpu.make_async_copy(src_ref.at[pl.ds(src_row, 1), :], dst_ref.at[pl.ds(dst_row, 1), :], sem)


ROW_UNROLL = 8


def _dispatch_kernel(dest_ref, eend_ref, h_ref, xs_ref, zbuf, sem, zsem):
    tm = h_ref.shape[0]

    @pl.when(pl.program_id(0) == 0)
    def _():
        zbuf[...] = jnp.zeros_like(zbuf)

        def last_block_copy(e):
            end_b = eend_ref[0, e]
            begin_b = jnp.where(e == 0, 0, eend_ref[0, jnp.maximum(e - 1, 0)])
            row0 = pl.multiple_of((end_b - 1) * EXPERT_BLOCK, EXPERT_BLOCK)
            return end_b > begin_b, pltpu.make_async_copy(zbuf, xs_ref.at[pl.ds(row0, EXPERT_BLOCK), :], zsem)

        def fill(e, carry):
            owns_rows, cp = last_block_copy(e)
            pl.when(owns_rows)(cp.start)
            return carry

        def drain(e, carry):
            owns_rows, cp = last_block_copy(e)
            pl.when(owns_rows)(cp.wait)
            return carry

        lax.fori_loop(0, N_EXPERTS, fill, 0)
        lax.fori_loop(0, N_EXPERTS, drain, 0)

        def spare_block_copy(b):
            row0 = pl.multiple_of(b * EXPERT_BLOCK, EXPERT_BLOCK)
            return pltpu.make_async_copy(zbuf, xs_ref.at[pl.ds(row0, EXPERT_BLOCK), :], zsem)

        n_used = eend_ref[0, N_EXPERTS - 1]
        n_blocks = xs_ref.shape[0] // EXPERT_BLOCK
        lax.fori_loop(n_used, n_blocks, lambda b, carry: (spare_block_copy(b).start(), carry)[1], 0)
        lax.fori_loop(n_used, n_blocks, lambda b, carry: (spare_block_copy(b).wait(), carry)[1], 0)

    def start(g, carry):
        for u in range(ROW_UNROLL):
            r = g * ROW_UNROLL + u
            _row_copy(h_ref, r, xs_ref, dest_ref[0, 0, r], sem).start(priority=u % 2)
            _row_copy(h_ref, r, xs_ref, dest_ref[0, 1, r], sem).start(priority=(u + 1) % 2)
        return carry

    lax.fori_loop(0, tm // ROW_UNROLL, start, 0)
    all_rows = pltpu.make_async_copy(h_ref, xs_ref.at[pl.ds(0, tm), :], sem)
    all_rows.wait()
    all_rows.wait()


def _dispatch(dest, eend, h2, n_rows):
    nt, _, tm = dest.shape
    n, d = h2.shape
    return pl.pallas_call(
        _dispatch_kernel,
        grid=(nt,),
        in_specs=[pl.BlockSpec((1, 2, tm), lambda i: (i, 0, 0), memory_space=pltpu.SMEM),
                  pl.BlockSpec(eend.shape, lambda i: (0, 0), memory_space=pltpu.SMEM),
                  pl.BlockSpec((tm, d), lambda i: (i, 0))],
        out_specs=pl.BlockSpec(memory_space=pl.ANY),
        out_shape=jax.ShapeDtypeStruct((n_rows, d), F32),
        scratch_shapes=[pltpu.VMEM((EXPERT_BLOCK, d), F32),
                        pltpu.SemaphoreType.DMA(()), pltpu.SemaphoreType.DMA(())],
        compiler_params=_params(1),
        name="dispatch",
    )(dest, eend, h2)


def _expert_kernel(bexp_ref, eend_ref, xs_ref, wg_ref, wu_ref, wd_ref, ys_ref, wg_b, wu_b, wd_b):
    i = pl.program_id(0)
    used = i < eend_ref[N_EXPERTS - 1]
    new_expert = jnp.logical_or(i == 0, bexp_ref[i] != bexp_ref[jnp.maximum(i - 1, 0)])

    @pl.when(jnp.logical_and(used, new_expert))
    def _():
        wg_b[...] = wg_ref[0].astype(BF16)
        wu_b[...] = wu_ref[0].astype(BF16)
        wd_b[...] = wd_ref[0].astype(BF16)

    @pl.when(used)
    def _():
        x = xs_ref[...].astype(BF16)
        hg = _dot(x, wg_b[...])
        hu = _dot(x, wu_b[...])
        hid = (hg * _sigmoid(hg) * hu).astype(BF16)
        ys_ref[...] = _dot(hid, wd_b[...])

    @pl.when(jnp.logical_not(used))
    def _():
        ys_ref[...] = jnp.zeros_like(ys_ref)


def _experts(bexp, eend, xs, e_gate, e_up, e_down, n_blocks):
    n_rows, d = xs.shape
    de = e_gate.shape[2]
    blk = EXPERT_BLOCK
    last = lambda i, nu: jnp.minimum(i, nu[N_EXPERTS - 1] - 1)
    rows = lambda i, be, nu: (last(i, nu), 0)
    wsel = lambda i, be, nu: (be[last(i, nu)], 0, 0)
    return pl.pallas_call(
        _expert_kernel,
        grid_spec=pltpu.PrefetchScalarGridSpec(
            num_scalar_prefetch=2,
            grid=(n_blocks,),
            in_specs=[pl.BlockSpec((blk, d), rows),
                      pl.BlockSpec((1, d, de), wsel),
                      pl.BlockSpec((1, d, de), wsel),
                      pl.BlockSpec((1, de, d), wsel)],
            out_specs=pl.BlockSpec((blk, d), lambda i, be, nu: (i, 0)),
            scratch_shapes=[pltpu.VMEM((d, de), BF16), pltpu.VMEM((d, de), BF16),
                            pltpu.VMEM((de, d), BF16)]),
        out_shape=jax.ShapeDtypeStruct((n_rows, d), F32),
        compiler_params=_params(1),
        name="experts",
    )(bexp, eend, xs, e_gate, e_up, e_down)


def _combine_kernel(dest_ref, dnext_ref, ys_ref, wts_ref, x1_ref, g2_ref, gf_ref, scf_ref, shf_ref, o_ref,
                    buf, sems):
    tm = x1_ref.shape[0]
    i = pl.program_id(0)
    cur = lax.rem(i, 2)

    def gather(d_ref, s):
        def start(g, carry):
            for u in range(ROW_UNROLL):
                r = g * ROW_UNROLL + u
                _row_copy(ys_ref, d_ref[0, 0, r], buf.at[s, 0], r, sems.at[s]).start(priority=u % 2)
                _row_copy(ys_ref, d_ref[0, 1, r], buf.at[s, 1], r, sems.at[s]).start(priority=(u + 1) % 2)
            return carry

        lax.fori_loop(0, tm // ROW_UNROLL, start, 0)

    @pl.when(i == 0)
    def _():
        gather(dest_ref, 0)

    @pl.when(i + 1 < pl.num_programs(0))
    def _():
        gather(dnext_ref, 1 - cur)

    all_rows = pltpu.make_async_copy(ys_ref.at[pl.ds(0, tm), :], buf.at[cur, 0], sems.at[cur])
    all_rows.wait()
    all_rows.wait()
    wts = wts_ref[...]
    moe = buf[cur, 0] * wts[:, 0:1] + buf[cur, 1] * wts[:, 1:2]
    x2 = x1_ref[...] + (1.0 + g2_ref[0]) * moe
    ms = jnp.mean(x2 * x2, axis=-1, keepdims=True)
    y = x2 * lax.rsqrt(ms + NORM_EPS) * gf_ref[...]
    o_ref[...] = y * (1.0 + scf_ref[0]) + shf_ref[0]


def _combine(dest, ys, wts_col, x1, mod, modf, gf, seq):
    nt, _, tm = dest.shape
    n, d = x1.shape
    tps = seq // tm
    return pl.pallas_call(
        _combine_kernel,
        grid=(nt,),
        in_specs=[pl.BlockSpec((1, 2, tm), lambda i: (i, 0, 0), memory_space=pltpu.SMEM),
                  pl.BlockSpec((1, 2, tm), lambda i: (jnp.minimum(i + 1, nt - 1), 0, 0),
                               memory_space=pltpu.SMEM),
                  pl.BlockSpec(memory_space=pl.ANY),
                  pl.BlockSpec((tm, 2), lambda i: (i, 0)),
                  pl.BlockSpec((tm, d), lambda i: (i, 0)),
                  pl.BlockSpec((1, 1, d), lambda i: ((i // tps) * 6 + 5, 0, 0)),
                  pl.BlockSpec((1, d), lambda i: (0, 0)),
                  pl.BlockSpec((1, 1, d), lambda i: ((i // tps) * 2 + 1, 0, 0)),
                  pl.BlockSpec((1, 1, d), lambda i: ((i // tps) * 2 + 0, 0, 0))],
        out_specs=pl.BlockSpec((tm, d), lambda i: (i, 0)),
        out_shape=jax.ShapeDtypeStruct((n, d), F32),
        scratch_shapes=[pltpu.VMEM((2, 2, tm, d), F32), pltpu.SemaphoreType.DMA((2,))],
        compiler_params=_params(1),
        name="combine",
    )(dest, dest, ys, wts_col, x1, mod, gf, modf, modf)


def kernel(x, c, ada_w, ada_b, norm1_g, w_in, shift_mu, w0, w_decay_up, a0, w_aaa_up, w_gate_up,
           k_k, k_a, r_k, gn_g, gn_b, kv_norm_g, k_idx_norm_g, w_uk, w_uv, w_out, norm2_g,
           w_group, b_group, w_expert, b_expert, e_gate, e_up, e_down,
           final_ada_w, final_ada_b, final_norm_g):
    bsz, seq, d = x.shape
    n = bsz * seq
    depth = ada_w.shape[0]
    tm = min(TOKEN_TILE, seq)
    x2 = x.reshape(n, d)

    modf = _ada(c, final_ada_w, final_ada_b).reshape(bsz * 2, 1, d)
    for l in range(depth):
        mod = _ada(c, ada_w[l], ada_b[l]).reshape(bsz * 6, 1, d)

        w1 = w_in[l][:, :RWKV_COLS].astype(BF16)
        w2 = jnp.pad(w_in[l][:, RWKV_COLS:], ((0, 0), (0, DSA_COLS_PAD - DSA_COLS))).astype(BF16)
        p_rwkv, p_dsa = _proj(x2, norm1_g[l].reshape(1, d), mod, w1, w2, seq, tm)

        zeros = jnp.zeros((DECAY_LORA, RWKV_WIDTH), F32)
        wda = jnp.concatenate([jnp.concatenate([w_decay_up[l], zeros], axis=1),
                               jnp.concatenate([zeros, w_aaa_up[l]], axis=1)], axis=0).astype(BF16)
        y_rwkv = _rwkv(p_rwkv, bsz, seq, shift_mu[l], w0[l], wda, a0[l], w_gate_up[l].astype(BF16),
                       k_k[l], k_a[l], r_k[l], gn_g[l], gn_b[l])
        y_dsa = _dsa_flash(p_dsa, bsz, seq, kv_norm_g[l], k_idx_norm_g[l], w_uk[l], w_uv[l])

        wr = jnp.zeros((ROUTER_ROWS, d), F32)
        wr = wr.at[0:N_GROUPS].set(w_group[l].T).at[8:8 + N_EXPERTS].set(w_expert[l].T)
        br = jnp.zeros((ROUTER_ROWS, 1), F32)
        br = br.at[0:N_GROUPS, 0].set(b_group[l]).at[8:8 + N_EXPERTS, 0].set(b_expert[l])
        wo = w_out[l].astype(BF16)
        x1, h2, ids, wts = _mix(y_rwkv, y_dsa, x2, wo[:RWKV_WIDTH], wo[RWKV_WIDTH:], mod,
                                norm2_g[l].reshape(1, d), wr, br, seq, tm)

        n_blocks = (n * 2) // EXPERT_BLOCK + N_EXPERTS
        n_blocks_pad = -(-n_blocks // 128) * 128
        dest, bexp, eend = _sort(ids, n_blocks_pad)
        xs = _dispatch(dest, eend, h2, n_blocks * EXPERT_BLOCK)
        ys = _experts(bexp.reshape(-1), eend.reshape(-1), xs, e_gate[l], e_up[l], e_down[l], n_blocks)
        wts_col = jnp.transpose(wts, (0, 2, 1)).reshape(n, 2)
        last = l == depth - 1
        if not last:
            raise NotImplementedError("stacked layers need a residual-only combine")
        x2 = _combine(dest, ys, wts_col, x1, mod, modf, final_norm_g.reshape(1, d), seq)
    return x2.reshape(bsz, seq, d)
```

```python
import functools

import numpy as np
import jax
import jax.numpy as jnp
from jax import lax
from jax.experimental import pallas as pl
from jax.experimental.pallas import tpu as pltpu

F32 = jnp.float32
BF16 = jnp.bfloat16
I32 = jnp.int32

HEAD_DIM = 64
RWKV_WIDTH = 512
RWKV_HEADS = RWKV_WIDTH // HEAD_DIM
GROUP_HEADS = 2
DSA_WIDTH = 512
DSA_HEADS = DSA_WIDTH // HEAD_DIM
DECAY_LORA = 64
AAA_LORA = 64
GATE_LORA = 128
KV_LATENT = 128
IDX_HEADS = 4
IDX_DIM = 64
RWKV_COLS = 3 * RWKV_WIDTH + DECAY_LORA + AAA_LORA + GATE_LORA
DSA_COLS = DSA_WIDTH + KV_LATENT + IDX_HEADS * IDX_DIM + IDX_DIM + IDX_HEADS
DSA_COLS_PAD = 1024
TOPK_MAX = 256
ATTN_CHUNK_LOG2 = 6
Q_BLOCK = 256
RWKV_CHUNK = 64
RWKV_STEP_CHUNKS = 4
N_GROUPS = 4
EXPERTS_PER_GROUP = 8
N_EXPERTS = N_GROUPS * EXPERTS_PER_GROUP
ROUTER_ROWS = 40
NORM_EPS = 1e-6
GN_EPS = HEAD_DIM * 1e-5
NEG_BIG = -1e30
INT_MIN = -2 ** 31

VMEM_LIMIT_BYTES = 56 * 1024 * 1024
EXPERT_BLOCK = 512
PROJ_TILE = 512
TOKEN_TILE = 1024

_NN = (((1,), (0,)), ((), ()))
_NT = (((1,), (1,)), ((), ()))
_TN = (((0,), (0,)), ((), ()))


def _dot(a, b, dims=_NN):
    return lax.dot_general(a, b, dims, preferred_element_type=F32)


def _split(x):
    hi = x.astype(BF16)
    lo = (x - hi.astype(F32)).astype(BF16)
    return hi, lo


def _dot3(a, b, dims=_NN):
    ah, al = _split(a)
    bh, bl = _split(b)
    return _dot(ah, bh, dims) + _dot(ah, bl, dims) + _dot(al, bh, dims)


def _dot_exact_rhs(a, b_bf16, dims=_NN):
    ah, al = _split(a)
    return _dot(ah, b_bf16, dims) + _dot(al, b_bf16, dims)


def _sigmoid(x):
    return 1.0 / (1.0 + jnp.exp(-x))


def _softplus(x):
    return jnp.maximum(x, 0.0) + jnp.log(1.0 + jnp.exp(-jnp.abs(x)))


def _params(n_axes):
    return pltpu.CompilerParams(dimension_semantics=("arbitrary",) * n_axes,
                                vmem_limit_bytes=VMEM_LIMIT_BYTES)


def _ada_kernel(c_ref, w_ref, b_ref, o_ref):
    c = c_ref[...]
    o_ref[...] = _dot3(c * _sigmoid(c), w_ref[...]) + b_ref[...]


def _ada(c, w, b):
    bsz, d = c.shape
    n = w.shape[1]
    tn = 1024
    return pl.pallas_call(
        _ada_kernel,
        grid=(n // tn,),
        in_specs=[pl.BlockSpec((bsz, d), lambda j: (0, 0)),
                  pl.BlockSpec((d, tn), lambda j: (0, j)),
                  pl.BlockSpec((1, tn), lambda j: (0, j))],
        out_specs=pl.BlockSpec((bsz, tn), lambda j: (0, j)),
        out_shape=jax.ShapeDtypeStruct((bsz, n), F32),
        compiler_params=_params(1),
        name="ada",
    )(c, w, b.reshape(1, n))


def _proj_kernel(x_ref, g_ref, sc_ref, sh_ref, w1_ref, w2_ref, o1_ref, o2_ref):
    x = x_ref[...]
    ms = jnp.mean(x * x, axis=-1, keepdims=True)
    y = x * lax.rsqrt(ms + NORM_EPS) * g_ref[...]
    h = (y * (1.0 + sc_ref[0]) + sh_ref[0]).astype(BF16)
    o1_ref[...] = _dot(h, w1_ref[...])
    o2_ref[...] = _dot(h, w2_ref[...])


def _proj(x2, g, mod, w1, w2, seq, tm):
    n, d = x2.shape
    tiles_per_seq = seq // tm
    return pl.pallas_call(
        _proj_kernel,
        grid=(n // tm,),
        in_specs=[pl.BlockSpec((tm, d), lambda i: (i, 0)),
                  pl.BlockSpec((1, d), lambda i: (0, 0)),
                  pl.BlockSpec((1, 1, d), lambda i: ((i // tiles_per_seq) * 6 + 1, 0, 0)),
                  pl.BlockSpec((1, 1, d), lambda i: ((i // tiles_per_seq) * 6 + 0, 0, 0)),
                  pl.BlockSpec(w1.shape, lambda i: (0, 0)),
                  pl.BlockSpec(w2.shape, lambda i: (0, 0))],
        out_specs=[pl.BlockSpec((tm, w1.shape[1]), lambda i: (i, 0)),
                   pl.BlockSpec((tm, w2.shape[1]), lambda i: (i, 0))],
        out_shape=[jax.ShapeDtypeStruct((n, w1.shape[1]), F32),
                   jax.ShapeDtypeStruct((n, w2.shape[1]), F32)],
        compiler_params=_params(1),
        name="proj",
    )(x2, g, mod, mod, w1, w2)


def _rwkv_kernel(p_ref, mu_ref, w0_ref, wda_ref, a0_ref, wg_ref, kk_ref, ka_ref, rk_ref,
                 gng_ref, gnb_ref, blk_ref, tril_ref, masks_ref, o_ref, s_ref, prev_ref):
    c = RWKV_CHUNK
    w = RWKV_WIDTH
    tt = p_ref.shape[0]
    chunks = range(tt // c)

    @pl.when(pl.program_id(1) == 0)
    def _():
        s_ref[...] = jnp.zeros_like(s_ref)
        prev_ref[...] = jnp.zeros_like(prev_ref)

    p = p_ref[...]
    row = lax.broadcasted_iota(I32, p.shape, 0)
    p_prev = jnp.where(row == 0, prev_ref[...], pltpu.roll(p, 1, 0))
    prev_ref[...] = p[tt - 1:tt, :]
    ps = p + mu_ref[...] * (p_prev - p)

    r = ps[:, 0:w]
    k = ps[:, w:2 * w]
    v = ps[:, 2 * w:3 * w]
    lora_in = ps[:, 3 * w:3 * w + DECAY_LORA + AAA_LORA]
    gate_in = ps[:, 3 * w + DECAY_LORA + AAA_LORA:]

    lane = lax.broadcasted_iota(I32, lora_in.shape, 1)
    lora_act = jnp.where(lane < DECAY_LORA, jnp.tanh(lora_in), lora_in).astype(BF16)
    da = _dot(lora_act, wda_ref[...])
    log_w = -_softplus(-(w0_ref[...] + da[:, :w])) - 0.5
    ld = -jnp.exp(log_w)
    a = _sigmoid(a0_ref[...] + da[:, w:])
    gate = _dot(_sigmoid(gate_in).astype(BF16), wg_ref[...])

    blk = blk_ref[...]

    def head_sums(z):
        nb = w // blk.shape[0]
        rows = jnp.concatenate([z[:, i * 128:(i + 1) * 128] for i in range(nb)], axis=0)
        sums = _dot(rows.astype(BF16), blk)
        return jnp.concatenate([sums[i * tt:(i + 1) * tt] for i in range(nb)], axis=1)

    kk = k * kk_ref[...]
    kk = kk / jnp.maximum(jnp.sqrt(head_sums(kk * kk)), 1e-12)
    k2 = k * (1.0 + (a - 1.0) * ka_ref[...])

    cs = _dot_exact_rhs_lhs(tril_ref[...], ld)
    cs_last = [cs[(ch + 1) * c - 1:(ch + 1) * c, :] for ch in chunks]
    cs_end = jnp.concatenate([jnp.broadcast_to(cl, (c, w)) for cl in cs_last], axis=0)
    e_neg = jnp.exp(-cs)
    e_rem = jnp.exp(cs_end - cs)
    kka = kk * a
    a_t = (-kk) * jnp.exp(cs - ld)
    b_t = kka * e_neg
    k_t = k2 * e_neg
    r_t = r * jnp.exp(cs)
    b_h = kka * e_rem
    k_h = k2 * e_rem
    w_c = [jnp.exp(cl) for cl in cs_last]

    bd = masks_ref[0]
    strict = masks_ref[1]
    incl = masks_ref[2]
    eye = masks_ref[3]
    gw = GROUP_HEADS * HEAD_DIM
    stack = lambda z: jnp.concatenate([z] * GROUP_HEADS, axis=0)

    groups = range(RWKV_HEADS // GROUP_HEADS)
    units = [(ch, g) for ch in chunks for g in groups]
    piece = lambda z, u: z[u[0] * c:(u[0] + 1) * c, u[1] * gw:(u[1] + 1) * gw]
    v_x = [(stack(piece(v, u)) * bd).astype(BF16) for u in units]
    ar_x = [jnp.concatenate([(stack(piece(a_t, u)) * bd).astype(BF16),
                             (stack(piece(r_t, u)) * bd).astype(BF16)], axis=0) for u in units]
    bk_r = [jnp.concatenate([stack(piece(b_t, u).astype(BF16)), stack(piece(k_t, u).astype(BF16))], axis=0)
            for u in units]
    bk_h = [jnp.concatenate([stack(piece(b_h, u).astype(BF16)), stack(piece(k_h, u).astype(BF16))], axis=0)
            for u in units]
    quad = [_dot(x, y, _NT) for x, y in zip(ar_x, bk_r)]
    l_ab = [q4[:gw, :gw] * strict for q4 in quad]
    l_ak = [(q4[:gw, gw:] * strict).astype(BF16) for q4 in quad]
    m_cat = [jnp.concatenate([(q4[gw:, :gw] * incl).astype(BF16), (q4[gw:, gw:] * incl).astype(BF16)], axis=1)
             for q4 in quad]

    pw = [p.astype(BF16) for p in l_ab]
    t_inv = [eye + l for l in l_ab]
    pw = [_dot(p, p) for p in pw]
    for _ in range(4):
        pb = [p.astype(BF16) for p in pw]
        both = [_dot(jnp.concatenate([t.astype(BF16), p], axis=0), p) for t, p in zip(t_inv, pb)]
        t_inv = [t + tp[:gw] for t, tp in zip(t_inv, both)]
        pw = [tp[gw:] for tp in both]
    t_inv = [(t + _dot(t.astype(BF16), p.astype(BF16))).astype(BF16) for t, p in zip(t_inv, pw)]

    state = [s_ref[g] for g in groups]
    y_rows = []
    for ch in chunks:
        us = [ch * len(groups) + g for g in groups]
        ar_s = [_dot(ar_x[u], state[g].astype(BF16), _NT) for g, u in zip(groups, us)]
        rhs = [ar_s[g][:gw] + _dot(l_ak[u], v_x[u]) for g, u in zip(groups, us)]
        ub = [_dot(t_inv[u], rhs[g].astype(BF16)).astype(BF16) for g, u in zip(groups, us)]
        uv = [jnp.concatenate([ub[g], v_x[u]], axis=0) for g, u in zip(groups, us)]
        y_x = [ar_s[g][gw:] + _dot(m_cat[u], uv[g]) for g, u in zip(groups, us)]
        y_rows.append(jnp.concatenate(
            [sum(yx[hh * c:(hh + 1) * c] for hh in range(1, GROUP_HEADS)) + yx[0:c] for yx in y_x], axis=1))
        state = [(state[g] * w_c[ch][:, g * gw:(g + 1) * gw] + _dot(uv[g], bk_h[u], _TN)) * bd
                 for g, u in zip(groups, us)]
    for g in groups:
        s_ref[g] = state[g]

    y = jnp.concatenate(y_rows, axis=0)
    inv_n = 1.0 / HEAD_DIM
    mean = head_sums(y) * inv_n
    dlt = y - mean
    var = head_sums(dlt * dlt) * inv_n
    yn = dlt * lax.rsqrt(var + GN_EPS) * gng_ref[...] + gnb_ref[...]
    bonus = head_sums(r * k2 * rk_ref[...]) * v
    o_ref[...] = ((yn + bonus) * gate).astype(o_ref.dtype)


def _dot_exact_rhs_lhs(a_bf16, b):
    bh, bl = _split(b)
    return _dot(a_bf16, bh) + _dot(a_bf16, bl)


def _rwkv(p, bsz, seq, mu, w0, wda, a0, wg, k_k, k_a, r_k, gn_g, gn_b):
    c = RWKV_CHUNK
    w = RWKV_WIDTH
    tt = c * RWKV_STEP_CHUNKS if seq % (c * RWKV_STEP_CHUNKS) == 0 else c
    n_steps = seq // tt
    head_of = np.arange(128) // HEAD_DIM
    blk = jnp.asarray(head_of[:, None] == head_of[None, :], BF16)
    ti = np.arange(tt)
    tril = jnp.asarray((ti[:, None] >= ti[None, :]) & (ti[:, None] // c == ti[None, :] // c), BF16)
    assert c == HEAD_DIM and RWKV_HEADS % GROUP_HEADS == 0
    gi = np.arange(GROUP_HEADS * c)
    same = (gi[:, None] // c) == (gi[None, :] // c)
    later = (gi[:, None] % c) > (gi[None, :] % c)
    masks = jnp.asarray(np.stack([same, same & later, same & (later | (gi[:, None] % c == gi[None, :] % c)),
                                  gi[:, None] == gi[None, :]]), F32)
    row = lambda a: a.reshape(1, -1)
    vec = lambda n: pl.BlockSpec((1, n), lambda b, t: (0, 0))
    full = lambda a: pl.BlockSpec(a.shape, lambda b, t: (0, 0))
    return pl.pallas_call(
        _rwkv_kernel,
        grid=(bsz, n_steps),
        in_specs=[pl.BlockSpec((tt, RWKV_COLS), lambda b, t: (b * n_steps + t, 0)),
                  vec(RWKV_COLS), vec(w), full(wda), vec(w), full(wg), vec(w), vec(w), vec(w),
                  vec(w), vec(w), full(blk), full(tril),
                  pl.BlockSpec(masks.shape, lambda b, t: (0, 0, 0))],
        out_specs=pl.BlockSpec((tt, w), lambda b, t: (b * n_steps + t, 0)),
        out_shape=jax.ShapeDtypeStruct((bsz * seq, w), BF16),
        scratch_shapes=[pltpu.VMEM((RWKV_HEADS // GROUP_HEADS, GROUP_HEADS * HEAD_DIM, GROUP_HEADS * HEAD_DIM), F32),
                        pltpu.VMEM((1, RWKV_COLS), F32)],
        compiler_params=_params(2),
        name="rwkv",
    )(p, row(mu), row(w0), wda, row(a0), wg, row(k_k), row(k_a), row(r_k), row(gn_g), row(gn_b),
      blk, tril, masks)


SLAB_ROWS = 64
ONES_ROWS = 16


def _fold_rows(x, op):
    acc = x[0:SLAB_ROWS]
    for r0 in range(SLAB_ROWS, x.shape[0], SLAB_ROWS):
        acc = op(acc, x[r0:r0 + SLAB_ROWS])
    return acc


def _split3(x):
    hi = x.astype(BF16)
    rest = x - hi.astype(F32)
    mid = rest.astype(BF16)
    return hi, mid, (rest - mid.astype(F32)).astype(BF16)


def _dsa_kernel(pd_ref, kvg_ref, kig_ref, wuk_ref, wuv_ref, tril_ref, wsel_ref, o_ref,
                ckv_s, ckvt_s, kcat_s, score_s, *, nq, n_cls, topk):
    qb = Q_BLOCK
    j = pl.program_id(1)

    @pl.when(j == 0)
    def _():
        ckv_s[...] = jnp.zeros_like(ckv_s)
        ckvt_s[...] = jnp.zeros_like(ckvt_s)
        kcat_s[...] = jnp.zeros_like(kcat_s)

    pd = pd_ref[...]
    q = pd[:, :DSA_WIDTH]
    ckv = pd[:, DSA_WIDTH:DSA_WIDTH + KV_LATENT]
    qi = pd[:, DSA_WIDTH + KV_LATENT:DSA_WIDTH + KV_LATENT + IDX_HEADS * IDX_DIM]
    tail = pd[:, DSA_WIDTH + KV_LATENT + IDX_HEADS * IDX_DIM:]
    ki = tail[:, :IDX_DIM]

    ckv_n = ckv * lax.rsqrt(jnp.mean(ckv * ckv, axis=-1, keepdims=True) + NORM_EPS) * kvg_ref[...]
    ki_n = ki * lax.rsqrt(jnp.mean(ki * ki, axis=-1, keepdims=True) + NORM_EPS) * kig_ref[...]
    ki_hi, ki_lo = _split(ki_n)
    rows = pl.ds(pl.multiple_of(j * qb, qb), qb)
    ckv_s[rows, :] = ckv_n.astype(BF16)
    one_row = jnp.where(lax.broadcasted_iota(I32, (ONES_ROWS, qb), 0) == 0, 1.0, 0.0)
    ckvt_s[j] = jnp.concatenate([ckv_n.T, one_row], axis=0).astype(BF16)
    kcat_s[rows, :] = jnp.concatenate([ki_hi, ki_hi, ki_lo, jnp.zeros_like(ki_hi)], axis=1)

    w_t = sum(_dot(wsel_ref[...], part, _NT) for part in _split3(tail))

    per = nq // n_cls
    for cls in range(n_cls):
        @pl.when((j >= cls * per) & (j < (cls + 1) * per))
        def _(sk=(cls + 1) * per * qb):
            _dsa_block(j, q, qi, w_t, wuk_ref, wuv_ref, tril_ref, o_ref, ckv_s, ckvt_s, kcat_s,
                       score_s, sk=sk, topk=topk)


def _dsa_block(j, q, qi, w_t, wuk_ref, wuv_ref, tril_ref, o_ref, ckv_s, ckvt_s, kcat_s, score_s,
               *, sk, topk):
    qb = Q_BLOCK
    kcat = kcat_s[0:sk, :]
    score = jnp.zeros((sk, qb), F32)
    w_s = w_t * (IDX_HEADS ** -0.5)
    for hh in range(IDX_HEADS):
        q_hi, q_lo = _split(qi[:, hh * IDX_DIM:(hh + 1) * IDX_DIM] * (IDX_DIM ** -0.5))
        dots = _dot(kcat, jnp.concatenate([q_hi, q_lo, q_hi, jnp.zeros_like(q_hi)], axis=1), _NT)
        score = score + w_s[hh:hh + 1, :] * jnp.maximum(dots, 0.0)

    key_pos = lax.broadcasted_iota(I32, (sk, qb), 0)
    q_pos = j * qb + lax.broadcasted_iota(I32, (sk, qb), 1)
    adm = jnp.right_shift(key_pos, ATTN_CHUNK_LOG2) <= jnp.right_shift(q_pos, ATTN_CHUNK_LOG2)

    score_s[0:sk, :] = jnp.where(adm, score, NEG_BIG)

    def as_float(okey):
        return lax.bitcast_convert_type(okey ^ ((okey >> 31) & 0x7FFFFFFF), F32)

    def count_ge(cand):
        acc = jnp.zeros((SLAB_ROWS, qb), F32)
        for r0 in range(0, sk, SLAB_ROWS):
            acc = acc + jnp.where(score_s[r0:r0 + SLAB_ROWS, :] >= cand, 1.0, 0.0)
        return jnp.sum(acc, axis=0, keepdims=True)

    def descend(i, tu):
        cand_u = tu | jnp.left_shift(jnp.int32(1), 31 - i)
        return jnp.where(count_ge(as_float(cand_u ^ INT_MIN)) >= topk, cand_u, tu)

    few_keys = (j + 1) * qb <= topk
    tu = lax.fori_loop(0, jnp.where(few_keys, 0, 32), descend, jnp.zeros((1, qb), I32))
    thr = jnp.where(few_keys, NEG_BIG, as_float(tu ^ INT_MIN))

    score = score_s[0:sk, :]
    gt = score > thr
    eq = score == thr
    need = topk - jnp.sum(_fold_rows(jnp.where(gt, 1.0, 0.0), jnp.add), axis=0, keepdims=True)
    eq_b = jnp.where(eq, 1.0, 0.0).astype(BF16)
    tb = tril_ref.shape[0]
    off = jnp.zeros((1, qb), F32)
    pieces = []
    for kb in range(sk // tb):
        pre = _dot(tril_ref[...], eq_b[kb * tb:(kb + 1) * tb, :])
        pieces.append(pre + off)
        off = off + pre[tb - 1:tb, :]
    prefix = jnp.concatenate(pieces, axis=0)
    tie_bias = jnp.where(prefix <= need, 0.0, NEG_BIG)
    bias = jnp.where(gt, 0.0, jnp.where(eq, tie_bias, NEG_BIG))
    bias = jnp.where(adm, bias, NEG_BIG)

    ckv_all = ckv_s[0:sk, :]
    ckv_t = jnp.concatenate([ckvt_s[b] for b in range(sk // qb)], axis=1)
    q_lat = (_dot(q.astype(BF16), wuk_ref[...]) * (HEAD_DIM ** -0.5)).astype(BF16)
    head_logits = lambda h: _dot(ckv_all, q_lat[:, h * KV_LATENT:(h + 1) * KV_LATENT], _NT)
    outs = []
    nxt = head_logits(0)
    for h in range(DSA_HEADS):
        logits = nxt + bias
        if h + 1 < DSA_HEADS:
            nxt = head_logits(h + 1)
        mx = jnp.max(_fold_rows(logits, jnp.maximum), axis=0, keepdims=True)
        pr = jnp.exp((logits - mx).astype(BF16))
        acc = _dot(ckv_t, pr)
        outs.append((acc[:KV_LATENT] / acc[KV_LATENT:KV_LATENT + 1]).astype(BF16))
    o_lat_t = jnp.concatenate(outs, axis=0)
    o_ref[...] = _dot(o_lat_t, wuv_ref[...], _TN).astype(o_ref.dtype)


def _dsa(pd, bsz, seq, kv_g, ki_g, w_uk, w_uv):
    qb = Q_BLOCK
    nq = seq // qb
    topk = min(TOPK_MAX, seq // 4)
    hc = DSA_HEADS * KV_LATENT
    head_of_q = np.arange(DSA_WIDTH) // HEAD_DIM
    head_of_l = np.arange(hc) // KV_LATENT
    mask = jnp.asarray(head_of_q[:, None] == head_of_l[None, :], F32)
    uk = jnp.transpose(w_uk, (1, 2, 0)).reshape(DSA_WIDTH, KV_LATENT)
    wuk_bd = (jnp.tile(uk, (1, DSA_HEADS)) * mask).astype(BF16)
    uv = w_uv.reshape(KV_LATENT, DSA_WIDTH)
    wuv_bd = (jnp.tile(uv, (DSA_HEADS, 1)) * mask.T).astype(BF16)
    n_cls = max(d for d in (4, 2, 1) if nq % d == 0)
    tb = 256 if ((nq // n_cls) * qb) % 256 == 0 else 128
    tril = jnp.asarray(np.tril(np.ones((tb, tb), np.float32)), BF16)
    wsel = jnp.asarray(np.arange(128)[None, :] == (IDX_DIM + np.arange(8))[:, None], BF16)
    wsel = wsel * jnp.asarray(np.arange(8)[:, None] < IDX_HEADS, BF16)
    full = lambda a: pl.BlockSpec(a.shape, lambda b, j: (0, 0))
    return pl.pallas_call(
        functools.partial(_dsa_kernel, nq=nq, n_cls=n_cls, topk=topk),
        grid=(bsz, nq),
        in_specs=[pl.BlockSpec((qb, DSA_COLS_PAD), lambda b, j: (b * nq + j, 0)),
                  pl.BlockSpec((1, KV_LATENT), lambda b, j: (0, 0)),
                  pl.BlockSpec((1, IDX_DIM), lambda b, j: (0, 0)),
                  full(wuk_bd), full(wuv_bd), full(tril), full(wsel)],
        out_specs=pl.BlockSpec((qb, DSA_WIDTH), lambda b, j: (b * nq + j, 0)),
        out_shape=jax.ShapeDtypeStruct((bsz * seq, DSA_WIDTH), BF16),
        scratch_shapes=[pltpu.VMEM((seq, KV_LATENT), BF16),
                        pltpu.VMEM((nq, KV_LATENT + ONES_ROWS, qb), BF16),
                        pltpu.VMEM((seq, 4 * IDX_DIM), BF16),
                        pltpu.VMEM((seq, qb), F32)],
        compiler_params=_params(2),
        name="dsa",
    )(pd, kv_g.reshape(1, -1), ki_g.reshape(1, -1), wuk_bd, wuv_bd, tril, wsel)


def _mix_kernel(yr_ref, yd_ref, x_ref, wo1_ref, wo2_ref, g1_ref, n2_ref, sc_ref, sh_ref,
                wr_ref, br_ref, x1_ref, h2_ref, ids_ref, wts_ref):
    mixed = _dot(yr_ref[...], wo1_ref[...]) + _dot(yd_ref[...], wo2_ref[...])
    x1 = x_ref[...] + (1.0 + g1_ref[0]) * mixed
    x1_ref[...] = x1
    ms = jnp.mean(x1 * x1, axis=-1, keepdims=True)
    h2 = x1 * lax.rsqrt(ms + NORM_EPS) * n2_ref[...] * (1.0 + sc_ref[0]) + sh_ref[0]
    h2_ref[...] = h2

    lt = _dot3(wr_ref[...], h2, _NT) + br_ref[...]
    gl = lt[0:N_GROUPS]
    el = lt[8:8 + N_EXPERTS]
    tm = gl.shape[1]
    gmax = jnp.max(gl, axis=0, keepdims=True)
    gidx = lax.broadcasted_iota(I32, (N_GROUPS, tm), 0)
    gsel = jnp.min(jnp.where(gl == gmax, gidx, N_GROUPS), axis=0, keepdims=True)
    p_group = 1.0 / jnp.sum(jnp.exp(gl - gmax), axis=0, keepdims=True)
    eidx = lax.broadcasted_iota(I32, (N_EXPERTS, tm), 0)
    el = jnp.where(jnp.right_shift(eidx, 3) == gsel, el, NEG_BIG)
    m1 = jnp.max(el, axis=0, keepdims=True)
    i1 = jnp.min(jnp.where(el == m1, eidx, N_EXPERTS), axis=0, keepdims=True)
    el2 = jnp.where(eidx == i1, NEG_BIG, el)
    m2 = jnp.max(el2, axis=0, keepdims=True)
    i2 = jnp.min(jnp.where(el2 == m2, eidx, N_EXPERTS), axis=0, keepdims=True)
    e2 = jnp.exp(m2 - m1)
    w1 = p_group / (1.0 + e2)
    ids_ref[0] = jnp.concatenate([i1, i2], axis=0)
    wts_ref[0] = jnp.concatenate([w1, w1 * e2], axis=0)


def _mix(yr, yd, x2, wo1, wo2, mod, n2g, wr, br, seq, tm):
    n, d = x2.shape
    nt = n // tm
    tps = seq // tm
    modspec = lambda k: pl.BlockSpec((1, 1, d), lambda i: ((i // tps) * 6 + k, 0, 0))
    full = lambda a: pl.BlockSpec(a.shape, lambda i: (0,) * a.ndim)
    return pl.pallas_call(
        _mix_kernel,
        grid=(nt,),
        in_specs=[pl.BlockSpec((tm, yr.shape[1]), lambda i: (i, 0)),
                  pl.BlockSpec((tm, yd.shape[1]), lambda i: (i, 0)),
                  pl.BlockSpec((tm, d), lambda i: (i, 0)),
                  full(wo1), full(wo2), modspec(2), full(n2g), modspec(4), modspec(3),
                  full(wr), full(br)],
        out_specs=[pl.BlockSpec((tm, d), lambda i: (i, 0)),
                   pl.BlockSpec((tm, d), lambda i: (i, 0)),
                   pl.BlockSpec((1, 2, tm), lambda i: (i, 0, 0)),
                   pl.BlockSpec((1, 2, tm), lambda i: (i, 0, 0))],
        out_shape=[jax.ShapeDtypeStruct((n, d), F32),
                   jax.ShapeDtypeStruct((n, d), F32),
                   jax.ShapeDtypeStruct((nt, 2, tm), I32),
                   jax.ShapeDtypeStruct((nt, 2, tm), F32)],
        compiler_params=_params(1),
        name="mix",
    )(yr, yd, x2, wo1, wo2, mod, n2g, mod, mod, wr, br)


def _sort_kernel(ids_ref, triu_ref, tril_ref, dest_ref, bexp_ref, eend_ref,
                 cnt_ref, run_ref, start_ref, *, n_blocks_pad):
    phase = pl.program_id(0)
    i = pl.program_id(1)
    ids = ids_ref[0]
    tm = ids.shape[1]
    eidx = lax.broadcasted_iota(I32, (N_EXPERTS, tm), 0)
    hit0 = eidx == ids[0:1]
    hit1 = eidx == ids[1:2]
    onehot = jnp.where(hit0, 1.0, 0.0) + jnp.where(hit1, 1.0, 0.0)

    @pl.when((phase == 0) & (i == 0))
    def _():
        cnt_ref[...] = jnp.zeros_like(cnt_ref)

    @pl.when(phase == 0)
    def _():
        cnt_ref[...] += jnp.sum(onehot, axis=1, keepdims=True)

    @pl.when((phase == 1) & (i == 0))
    def _():
        run_ref[...] = jnp.zeros_like(run_ref)
        nblk = jnp.floor((cnt_ref[...] + (EXPERT_BLOCK - 1)) * (1.0 / EXPERT_BLOCK))
        nblk_b = jnp.broadcast_to(nblk, (N_EXPERTS, 128))
        first_blk = _dot_exact_rhs_lhs(tril_ref[...], nblk_b)
        start_ref[...] = first_blk[:, 0:1] * EXPERT_BLOCK
        end_blk = first_blk + nblk_b
        bidx = lax.broadcasted_iota(I32, (N_EXPERTS, n_blocks_pad), 1).astype(F32)
        owner = jnp.sum(jnp.where(end_blk[:, 0:1] <= bidx, 1.0, 0.0), axis=0, keepdims=True)
        bexp_ref[...] = jnp.minimum(owner, N_EXPERTS - 1).astype(I32)
        on_diag = (lax.broadcasted_iota(I32, (N_EXPERTS, 128), 0)
                   == lax.broadcasted_iota(I32, (N_EXPERTS, 128), 1))
        eend_ref[...] = jnp.sum(jnp.where(on_diag, end_blk, 0.0), axis=0, keepdims=True).astype(I32)

    @pl.when(phase == 1)
    def _():
        before = _dot(onehot.astype(BF16), triu_ref[...])
        pos = start_ref[...] + run_ref[...] + before
        d0 = jnp.sum(jnp.where(hit0, pos, 0.0), axis=0, keepdims=True)
        d1 = jnp.sum(jnp.where(hit1, pos, 0.0), axis=0, keepdims=True)
        dest_ref[0] = jnp.concatenate([d0, d1], axis=0).astype(I32)
        run_ref[...] += jnp.sum(onehot, axis=1, keepdims=True)


def _sort(ids, n_blocks_pad):
    nt, _, tm = ids.shape
    triu = jnp.asarray(np.triu(np.ones((tm, tm), np.float32), 1), BF16)
    tril = jnp.asarray(np.tril(np.ones((N_EXPERTS, N_EXPERTS), np.float32), -1), BF16)
    return pl.pallas_call(
        functools.partial(_sort_kernel, n_blocks_pad=n_blocks_pad),
        grid=(2, nt),
        in_specs=[pl.BlockSpec((1, 2, tm), lambda p, i: (i, 0, 0)),
                  pl.BlockSpec(triu.shape, lambda p, i: (0, 0)),
                  pl.BlockSpec(tril.shape, lambda p, i: (0, 0))],
        out_specs=[pl.BlockSpec((1, 2, tm), lambda p, i: (i * p, 0, 0)),
                   pl.BlockSpec((1, n_blocks_pad), lambda p, i: (0, 0)),
                   pl.BlockSpec((1, 128), lambda p, i: (0, 0))],
        out_shape=[jax.ShapeDtypeStruct((nt, 2, tm), I32),
                   jax.ShapeDtypeStruct((1, n_blocks_pad), I32),
                   jax.ShapeDtypeStruct((1, 128), I32)],
        scratch_shapes=[pltpu.VMEM((N_EXPERTS, 1), F32),
                        pltpu.VMEM((N_EXPERTS, 1), F32),
                        pltpu.VMEM((N_EXPERTS, 1), F32)],
        compiler_params=_params(2),
        name="sort",
    )(ids, triu, tril)


def _row_copy(src_ref, src_row, dst_ref, dst_row, sem):
    return pltpu.make_async_copy(src_ref.at[pl.ds(src_row, 1), :], dst_ref.at[pl.ds(dst_row, 1), :], sem)


ROW_UNROLL = 8


def _dispatch_kernel(dest_ref, eend_ref, h_ref, xs_ref, zbuf, sem, zsem):
    tm = h_ref.shape[0]

    @pl.when(pl.program_id(0) == 0)
    def _():
        zbuf[...] = jnp.zeros_like(zbuf)

        def last_block_copy(e):
            end_b = eend_ref[0, e]
            begin_b = jnp.where(e == 0, 0, eend_ref[0, jnp.maximum(e - 1, 0)])
            row0 = pl.multiple_of((end_b - 1) * EXPERT_BLOCK, EXPERT_BLOCK)
            return end_b > begin_b, pltpu.make_async_copy(zbuf, xs_ref.at[pl.ds(row0, EXPERT_BLOCK), :], zsem)

        def fill(e, carry):
            owns_rows, cp = last_block_copy(e)
            pl.when(owns_rows)(cp.start)
            return carry

        def drain(e, carry):
            owns_rows, cp = last_block_copy(e)
            pl.when(owns_rows)(cp.wait)
            return carry

        lax.fori_loop(0, N_EXPERTS, fill, 0)
        lax.fori_loop(0, N_EXPERTS, drain, 0)

        def spare_block_copy(b):
            row0 = pl.multiple_of(b * EXPERT_BLOCK, EXPERT_BLOCK)
            return pltpu.make_async_copy(zbuf, xs_ref.at[pl.ds(row0, EXPERT_BLOCK), :], zsem)

        n_used = eend_ref[0, N_EXPERTS - 1]
        n_blocks = xs_ref.shape[0] // EXPERT_BLOCK
        lax.fori_loop(n_used, n_blocks, lambda b, carry: (spare_block_copy(b).start(), carry)[1], 0)
        lax.fori_loop(n_used, n_blocks, lambda b, carry: (spare_block_copy(b).wait(), carry)[1], 0)

    def start(g, carry):
        for u in range(ROW_UNROLL):
            r = g * ROW_UNROLL + u
            _row_copy(h_ref, r, xs_ref, dest_ref[0, 0, r], sem).start(priority=u % 2)
            _row_copy(h_ref, r, xs_ref, dest_ref[0, 1, r], sem).start(priority=(u + 1) % 2)
        return carry

    lax.fori_loop(0, tm // ROW_UNROLL, start, 0)
    all_rows = pltpu.make_async_copy(h_ref, xs_ref.at[pl.ds(0, tm), :], sem)
    all_rows.wait()
    all_rows.wait()


def _dispatch(dest, eend, h2, n_rows):
    nt, _, tm = dest.shape
    n, d = h2.shape
    return pl.pallas_call(
        _dispatch_kernel,
        grid=(nt,),
        in_specs=[pl.BlockSpec((1, 2, tm), lambda i: (i, 0, 0), memory_space=pltpu.SMEM),
                  pl.BlockSpec(eend.shape, lambda i: (0, 0), memory_space=pltpu.SMEM),
                  pl.BlockSpec((tm, d), lambda i: (i, 0))],
        out_specs=pl.BlockSpec(memory_space=pl.ANY),
        out_shape=jax.ShapeDtypeStruct((n_rows, d), F32),
        scratch_shapes=[pltpu.VMEM((EXPERT_BLOCK, d), F32),
                        pltpu.SemaphoreType.DMA(()), pltpu.SemaphoreType.DMA(())],
        compiler_params=_params(1),
        name="dispatch",
    )(dest, eend, h2)


def _expert_kernel(bexp_ref, eend_ref, xs_ref, wg_ref, wu_ref, wd_ref, ys_ref, wg_b, wu_b, wd_b):
    i = pl.program_id(0)
    used = i < eend_ref[N_EXPERTS - 1]
    new_expert = jnp.logical_or(i == 0, bexp_ref[i] != bexp_ref[jnp.maximum(i - 1, 0)])

    @pl.when(jnp.logical_and(used, new_expert))
    def _():
        wg_b[...] = wg_ref[0].astype(BF16)
        wu_b[...] = wu_ref[0].astype(BF16)
        wd_b[...] = wd_ref[0].astype(BF16)

    @pl.when(used)
    def _():
        x = xs_ref[...].astype(BF16)
        hg = _dot(x, wg_b[...])
        hu = _dot(x, wu_b[...])
        hid = (hg * _sigmoid(hg) * hu).astype(BF16)
        ys_ref[...] = _dot(hid, wd_b[...])

    @pl.when(jnp.logical_not(used))
    def _():
        ys_ref[...] = jnp.zeros_like(ys_ref)


def _experts(bexp, eend, xs, e_gate, e_up, e_down, n_blocks):
    n_rows, d = xs.shape
    de = e_gate.shape[2]
    blk = EXPERT_BLOCK
    last = lambda i, nu: jnp.minimum(i, nu[N_EXPERTS - 1] - 1)
    rows = lambda i, be, nu: (last(i, nu), 0)
    wsel = lambda i, be, nu: (be[last(i, nu)], 0, 0)
    return pl.pallas_call(
        _expert_kernel,
        grid_spec=pltpu.PrefetchScalarGridSpec(
            num_scalar_prefetch=2,
            grid=(n_blocks,),
            in_specs=[pl.BlockSpec((blk, d), rows),
                      pl.BlockSpec((1, d, de), wsel),
                      pl.BlockSpec((1, d, de), wsel),
                      pl.BlockSpec((1, de, d), wsel)],
            out_specs=pl.BlockSpec((blk, d), lambda i, be, nu: (i, 0)),
            scratch_shapes=[pltpu.VMEM((d, de), BF16), pltpu.VMEM((d, de), BF16),
                            pltpu.VMEM((de, d), BF16)]),
        out_shape=jax.ShapeDtypeStruct((n_rows, d), F32),
        compiler_params=_params(1),
        name="experts",
    )(bexp, eend, xs, e_gate, e_up, e_down)


def _combine_kernel(dest_ref, dnext_ref, ys_ref, wts_ref, x1_ref, g2_ref, gf_ref, scf_ref, shf_ref, o_ref,
                    buf, sems):
    tm = x1_ref.shape[0]
    i = pl.program_id(0)
    cur = lax.rem(i, 2)

    def gather(d_ref, s):
        def start(g, carry):
            for u in range(ROW_UNROLL):
                r = g * ROW_UNROLL + u
                _row_copy(ys_ref, d_ref[0, 0, r], buf.at[s, 0], r, sems.at[s]).start(priority=u % 2)
                _row_copy(ys_ref, d_ref[0, 1, r], buf.at[s, 1], r, sems.at[s]).start(priority=(u + 1) % 2)
            return carry

        lax.fori_loop(0, tm // ROW_UNROLL, start, 0)

    @pl.when(i == 0)
    def _():
        gather(dest_ref, 0)

    @pl.when(i + 1 < pl.num_programs(0))
    def _():
        gather(dnext_ref, 1 - cur)

    all_rows = pltpu.make_async_copy(ys_ref.at[pl.ds(0, tm), :], buf.at[cur, 0], sems.at[cur])
    all_rows.wait()
    all_rows.wait()
    wts = wts_ref[...]
    moe = buf[cur, 0] * wts[:, 0:1] + buf[cur, 1] * wts[:, 1:2]
    x2 = x1_ref[...] + (1.0 + g2_ref[0]) * moe
    ms = jnp.mean(x2 * x2, axis=-1, keepdims=True)
    y = x2 * lax.rsqrt(ms + NORM_EPS) * gf_ref[...]
    o_ref[...] = y * (1.0 + scf_ref[0]) + shf_ref[0]


def _combine(dest, ys, wts_col, x1, mod, modf, gf, seq):
    nt, _, tm = dest.shape
    n, d = x1.shape
    tps = seq // tm
    return pl.pallas_call(
        _combine_kernel,
        grid=(nt,),
        in_specs=[pl.BlockSpec((1, 2, tm), lambda i: (i, 0, 0), memory_space=pltpu.SMEM),
                  pl.BlockSpec((1, 2, tm), lambda i: (jnp.minimum(i + 1, nt - 1), 0, 0),
                               memory_space=pltpu.SMEM),
                  pl.BlockSpec(memory_space=pl.ANY),
                  pl.BlockSpec((tm, 2), lambda i: (i, 0)),
                  pl.BlockSpec((tm, d), lambda i: (i, 0)),
                  pl.BlockSpec((1, 1, d), lambda i: ((i // tps) * 6 + 5, 0, 0)),
                  pl.BlockSpec((1, d), lambda i: (0, 0)),
                  pl.BlockSpec((1, 1, d), lambda i: ((i // tps) * 2 + 1, 0, 0)),
                  pl.BlockSpec((1, 1, d), lambda i: ((i // tps) * 2 + 0, 0, 0))],
        out_specs=pl.BlockSpec((tm, d), lambda i: (i, 0)),
        out_shape=jax.ShapeDtypeStruct((n, d), F32),
        scratch_shapes=[pltpu.VMEM((2, 2, tm, d), F32), pltpu.SemaphoreType.DMA((2,))],
        compiler_params=_params(1),
        name="combine",
    )(dest, dest, ys, wts_col, x1, mod, gf, modf, modf)


def kernel(x, c, ada_w, ada_b, norm1_g, w_in, shift_mu, w0, w_decay_up, a0, w_aaa_up, w_gate_up,
           k_k, k_a, r_k, gn_g, gn_b, kv_norm_g, k_idx_norm_g, w_uk, w_uv, w_out, norm2_g,
           w_group, b_group, w_expert, b_expert, e_gate, e_up, e_down,
           final_ada_w, final_ada_b, final_norm_g):
    bsz, seq, d = x.shape
    n = bsz * seq
    depth = ada_w.shape[0]
    tm = min(TOKEN_TILE, seq)
    x2 = x.reshape(n, d)

    modf = _ada(c, final_ada_w, final_ada_b).reshape(bsz * 2, 1, d)
    for l in range(depth):
        mod = _ada(c, ada_w[l], ada_b[l]).reshape(bsz * 6, 1, d)

        w1 = w_in[l][:, :RWKV_COLS].astype(BF16)
        w2 = jnp.pad(w_in[l][:, RWKV_COLS:], ((0, 0), (0, DSA_COLS_PAD - DSA_COLS))).astype(BF16)
        p_rwkv, p_dsa = _proj(x2, norm1_g[l].reshape(1, d), mod, w1, w2, seq, min(PROJ_TILE, seq))

        zeros = jnp.zeros((DECAY_LORA, RWKV_WIDTH), F32)
        wda = jnp.concatenate([jnp.concatenate([w_decay_up[l], zeros], axis=1),
                               jnp.concatenate([zeros, w_aaa_up[l]], axis=1)], axis=0).astype(BF16)
        y_rwkv = _rwkv(p_rwkv, bsz, seq, shift_mu[l], w0[l], wda, a0[l], w_gate_up[l].astype(BF16),
                       k_k[l], k_a[l], r_k[l], gn_g[l], gn_b[l])
        y_dsa = _dsa(p_dsa, bsz, seq, kv_norm_g[l], k_idx_norm_g[l], w_uk[l], w_uv[l])

        wr = jnp.zeros((ROUTER_ROWS, d), F32)
        wr = wr.at[0:N_GROUPS].set(w_group[l].T).at[8:8 + N_EXPERTS].set(w_expert[l].T)
        br = jnp.zeros((ROUTER_ROWS, 1), F32)
        br = br.at[0:N_GROUPS, 0].set(b_group[l]).at[8:8 + N_EXPERTS, 0].set(b_expert[l])
        wo = w_out[l].astype(BF16)
        x1, h2, ids, wts = _mix(y_rwkv, y_dsa, x2, wo[:RWKV_WIDTH], wo[RWKV_WIDTH:], mod,
                                norm2_g[l].reshape(1, d), wr, br, seq, tm)

        n_blocks = (n * 2) // EXPERT_BLOCK + N_EXPERTS
        n_blocks_pad = -(-n_blocks // 128) * 128
        dest, bexp, eend = _sort(ids, n_blocks_pad)
        xs = _dispatch(dest, eend, h2, n_blocks * EXPERT_BLOCK)
        ys = _experts(bexp.reshape(-1), eend.reshape(-1), xs, e_gate[l], e_up[l], e_down[l], n_blocks)
        wts_col = jnp.transpose(wts, (0, 2, 1)).reshape(n, 2)
        last = l == depth - 1
        if not last:
            raise NotImplementedError("stacked layers need a residual-only combine")
        x2 = _combine(dest, ys, wts_col, x1, mod, modf, final_norm_g.reshape(1, d), seq)
    return x2.reshape(bsz, seq, d)
```

```python
import functools

import numpy as np
import jax
import jax.numpy as jnp
from jax import lax
from jax.experimental import pallas as pl
from jax.experimental.pallas import tpu as pltpu

F32 = jnp.float32
BF16 = jnp.bfloat16
I32 = jnp.int32

HEAD_DIM = 64
RWKV_WIDTH = 512
RWKV_HEADS = RWKV_WIDTH // HEAD_DIM
GROUP_HEADS = 2
DSA_WIDTH = 512
DSA_HEADS = DSA_WIDTH // HEAD_DIM
DECAY_LORA = 64
AAA_LORA = 64
GATE_LORA = 128
KV_LATENT = 128
IDX_HEADS = 4
IDX_DIM = 64
RWKV_COLS = 3 * RWKV_WIDTH + DECAY_LORA + AAA_LORA + GATE_LORA
DSA_COLS = DSA_WIDTH + KV_LATENT + IDX_HEADS * IDX_DIM + IDX_DIM + IDX_HEADS
DSA_COLS_PAD = 1024
TOPK_MAX = 256
ATTN_CHUNK_LOG2 = 6
Q_BLOCK = 256
RWKV_CHUNK = 64
RWKV_STEP_CHUNKS = 4
N_GROUPS = 4
EXPERTS_PER_GROUP = 8
N_EXPERTS = N_GROUPS * EXPERTS_PER_GROUP
ROUTER_ROWS = 40
NORM_EPS = 1e-6
GN_EPS = HEAD_DIM * 1e-5
NEG_BIG = -1e30
INT_MIN = -2 ** 31

VMEM_LIMIT_BYTES = 56 * 1024 * 1024
EXPERT_BLOCK = 512
PROJ_TILE = 512
TOKEN_TILE = 1024

_NN = (((1,), (0,)), ((), ()))
_NT = (((1,), (1,)), ((), ()))
_TN = (((0,), (0,)), ((), ()))


def _dot(a, b, dims=_NN):
    return lax.dot_general(a, b, dims, preferred_element_type=F32)


def _split(x):
    hi = x.astype(BF16)
    lo = (x - hi.astype(F32)).astype(BF16)
    return hi, lo


def _dot3(a, b, dims=_NN):
    ah, al = _split(a)
    bh, bl = _split(b)
    return _dot(ah, bh, dims) + _dot(ah, bl, dims) + _dot(al, bh, dims)


def _dot_exact_rhs(a, b_bf16, dims=_NN):
    ah, al = _split(a)
    return _dot(ah, b_bf16, dims) + _dot(al, b_bf16, dims)


def _sigmoid(x):
    return 1.0 / (1.0 + jnp.exp(-x))


def _softplus(x):
    return jnp.maximum(x, 0.0) + jnp.log(1.0 + jnp.exp(-jnp.abs(x)))


def _params(n_axes):
    return pltpu.CompilerParams(dimension_semantics=("arbitrary",) * n_axes,
                                vmem_limit_bytes=VMEM_LIMIT_BYTES)


def _ada_kernel(c_ref, w_ref, b_ref, o_ref):
    c = c_ref[...]
    o_ref[...] = _dot3(c * _sigmoid(c), w_ref[...]) + b_ref[...]


def _ada(c, w, b):
    bsz, d = c.shape
    n = w.shape[1]
    tn = 1024
    return pl.pallas_call(
        _ada_kernel,
        grid=(n // tn,),
        in_specs=[pl.BlockSpec((bsz, d), lambda j: (0, 0)),
                  pl.BlockSpec((d, tn), lambda j: (0, j)),
                  pl.BlockSpec((1, tn), lambda j: (0, j))],
        out_specs=pl.BlockSpec((bsz, tn), lambda j: (0, j)),
        out_shape=jax.ShapeDtypeStruct((bsz, n), F32),
        compiler_params=_params(1),
        name="ada",
    )(c, w, b.reshape(1, n))


def _proj_kernel(x_ref, g_ref, sc_ref, sh_ref, w1_ref, w2_ref, o1_ref, o2_ref):
    x = x_ref[...]
    ms = jnp.mean(x * x, axis=-1, keepdims=True)
    y = x * lax.rsqrt(ms + NORM_EPS) * g_ref[...]
    h = (y * (1.0 + sc_ref[0]) + sh_ref[0]).astype(BF16)
    o1_ref[...] = _dot(h, w1_ref[...])
    o2_ref[...] = _dot(h, w2_ref[...])


def _proj(x2, g, mod, w1, w2, seq, tm):
    n, d = x2.shape
    tiles_per_seq = seq // tm
    return pl.pallas_call(
        _proj_kernel,
        grid=(n // tm,),
        in_specs=[pl.BlockSpec((tm, d), lambda i: (i, 0)),
                  pl.BlockSpec((1, d), lambda i: (0, 0)),
                  pl.BlockSpec((1, 1, d), lambda i: ((i // tiles_per_seq) * 6 + 1, 0, 0)),
                  pl.BlockSpec((1, 1, d), lambda i: ((i // tiles_per_seq) * 6 + 0, 0, 0)),
                  pl.BlockSpec(w1.shape, lambda i: (0, 0)),
                  pl.BlockSpec(w2.shape, lambda i: (0, 0))],
        out_specs=[pl.BlockSpec((tm, w1.shape[1]), lambda i: (i, 0)),
                   pl.BlockSpec((tm, w2.shape[1]), lambda i: (i, 0))],
        out_shape=[jax.ShapeDtypeStruct((n, w1.shape[1]), F32),
                   jax.ShapeDtypeStruct((n, w2.shape[1]), F32)],
        compiler_params=_params(1),
        name="proj",
    )(x2, g, mod, mod, w1, w2)


def _rwkv_kernel(p_ref, mu_ref, w0_ref, wda_ref, a0_ref, wg_ref, kk_ref, ka_ref, rk_ref,
                 gng_ref, gnb_ref, blk_ref, tril_ref, masks_ref, o_ref, s_ref, prev_ref):
    c = RWKV_CHUNK
    w = RWKV_WIDTH
    tt = p_ref.shape[0]
    chunks = range(tt // c)

    @pl.when(pl.program_id(1) == 0)
    def _():
        s_ref[...] = jnp.zeros_like(s_ref)
        prev_ref[...] = jnp.zeros_like(prev_ref)

    p = p_ref[...]
    row = lax.broadcasted_iota(I32, p.shape, 0)
    p_prev = jnp.where(row == 0, prev_ref[...], pltpu.roll(p, 1, 0))
    prev_ref[...] = p[tt - 1:tt, :]
    ps = p + mu_ref[...] * (p_prev - p)

    r = ps[:, 0:w]
    k = ps[:, w:2 * w]
    v = ps[:, 2 * w:3 * w]
    lora_in = ps[:, 3 * w:3 * w + DECAY_LORA + AAA_LORA]
    gate_in = ps[:, 3 * w + DECAY_LORA + AAA_LORA:]

    lane = lax.broadcasted_iota(I32, lora_in.shape, 1)
    lora_act = jnp.where(lane < DECAY_LORA, jnp.tanh(lora_in), lora_in).astype(BF16)
    da = _dot(lora_act, wda_ref[...])
    log_w = -_softplus(-(w0_ref[...] + da[:, :w])) - 0.5
    ld = -jnp.exp(log_w)
    a = _sigmoid(a0_ref[...] + da[:, w:])
    gate = _dot(_sigmoid(gate_in).astype(BF16), wg_ref[...])

    blk = blk_ref[...]

    def head_sums(z):
        nb = w // blk.shape[0]
        rows = jnp.concatenate([z[:, i * 128:(i + 1) * 128] for i in range(nb)], axis=0)
        sums = _dot(rows.astype(BF16), blk)
        return jnp.concatenate([sums[i * tt:(i + 1) * tt] for i in range(nb)], axis=1)

    kk = k * kk_ref[...]
    kk = kk / jnp.maximum(jnp.sqrt(head_sums(kk * kk)), 1e-12)
    k2 = k * (1.0 + (a - 1.0) * ka_ref[...])

    cs = _dot_exact_rhs_lhs(tril_ref[...], ld)
    cs_last = [cs[(ch + 1) * c - 1:(ch + 1) * c, :] for ch in chunks]
    cs_end = jnp.concatenate([jnp.broadcast_to(cl, (c, w)) for cl in cs_last], axis=0)
    e_neg = jnp.exp(-cs)
    e_rem = jnp.exp(cs_end - cs)
    kka = kk * a
    a_t = (-kk) * jnp.exp(cs - ld)
    b_t = kka * e_neg
    k_t = k2 * e_neg
    r_t = r * jnp.exp(cs)
    b_h = kka * e_rem
    k_h = k2 * e_rem
    w_c = [jnp.exp(cl) for cl in cs_last]

    bd = masks_ref[0]
    strict = masks_ref[1]
    incl = masks_ref[2]
    eye = masks_ref[3]
    gw = GROUP_HEADS * HEAD_DIM
    stack = lambda z: jnp.concatenate([z] * GROUP_HEADS, axis=0)

    groups = range(RWKV_HEADS // GROUP_HEADS)
    units = [(ch, g) for ch in chunks for g in groups]
    piece = lambda z, u: z[u[0] * c:(u[0] + 1) * c, u[1] * gw:(u[1] + 1) * gw]
    v_x = [(stack(piece(v, u)) * bd).astype(BF16) for u in units]
    ar_x = [jnp.concatenate([(stack(piece(a_t, u)) * bd).astype(BF16),
                             (stack(piece(r_t, u)) * bd).astype(BF16)], axis=0) for u in units]
    bk_r = [jnp.concatenate([stack(piece(b_t, u).astype(BF16)), stack(piece(k_t, u).astype(BF16))], axis=0)
            for u in units]
    bk_h = [jnp.concatenate([stack(piece(b_h, u).astype(BF16)), stack(piece(k_h, u).astype(BF16))], axis=0)
            for u in units]
    quad = [_dot(x, y, _NT) for x, y in zip(ar_x, bk_r)]
    l_ab = [q4[:gw, :gw] * strict for q4 in quad]
    l_ak = [(q4[:gw, gw:] * strict).astype(BF16) for q4 in quad]
    m_cat = [jnp.concatenate([(q4[gw:, :gw] * incl).astype(BF16), (q4[gw:, gw:] * incl).astype(BF16)], axis=1)
             for q4 in quad]

    pw = [p.astype(BF16) for p in l_ab]
    t_inv = [eye + l for l in l_ab]
    pw = [_dot(p, p) for p in pw]
    for _ in range(4):
        pb = [p.astype(BF16) for p in pw]
        both = [_dot(jnp.concatenate([t.astype(BF16), p], axis=0), p) for t, p in zip(t_inv, pb)]
        t_inv = [t + tp[:gw] for t, tp in zip(t_inv, both)]
        pw = [tp[gw:] for tp in both]
    t_inv = [(t + _dot(t.astype(BF16), p.astype(BF16))).astype(BF16) for t, p in zip(t_inv, pw)]

    state = [s_ref[g] for g in groups]
    y_rows = []
    for ch in chunks:
        us = [ch * len(groups) + g for g in groups]
        ar_s = [_dot(ar_x[u], state[g].astype(BF16), _NT) for g, u in zip(groups, us)]
        rhs = [ar_s[g][:gw] + _dot(l_ak[u], v_x[u]) for g, u in zip(groups, us)]
        ub = [_dot(t_inv[u], rhs[g].astype(BF16)).astype(BF16) for g, u in zip(groups, us)]
        uv = [jnp.concatenate([ub[g], v_x[u]], axis=0) for g, u in zip(groups, us)]
        y_x = [ar_s[g][gw:] + _dot(m_cat[u], uv[g]) for g, u in zip(groups, us)]
        y_rows.append(jnp.concatenate(
            [sum(yx[hh * c:(hh + 1) * c] for hh in range(1, GROUP_HEADS)) + yx[0:c] for yx in y_x], axis=1))
        state = [(state[g] * w_c[ch][:, g * gw:(g + 1) * gw] + _dot(uv[g], bk_h[u], _TN)) * bd
                 for g, u in zip(groups, us)]
    for g in groups:
        s_ref[g] = state[g]

    y = jnp.concatenate(y_rows, axis=0)
    inv_n = 1.0 / HEAD_DIM
    mean = head_sums(y) * inv_n
    dlt = y - mean
    var = head_sums(dlt * dlt) * inv_n
    yn = dlt * lax.rsqrt(var + GN_EPS) * gng_ref[...] + gnb_ref[...]
    bonus = head_sums(r * k2 * rk_ref[...]) * v
    o_ref[...] = ((yn + bonus) * gate).astype(o_ref.dtype)


def _dot_exact_rhs_lhs(a_bf16, b):
    bh, bl = _split(b)
    return _dot(a_bf16, bh) + _dot(a_bf16, bl)


def _rwkv(p, bsz, seq, mu, w0, wda, a0, wg, k_k, k_a, r_k, gn_g, gn_b):
    c = RWKV_CHUNK
    w = RWKV_WIDTH
    tt = c * RWKV_STEP_CHUNKS if seq % (c * RWKV_STEP_CHUNKS) == 0 else c
    n_steps = seq // tt
    head_of = np.arange(128) // HEAD_DIM
    blk = jnp.asarray(head_of[:, None] == head_of[None, :], BF16)
    ti = np.arange(tt)
    tril = jnp.asarray((ti[:, None] >= ti[None, :]) & (ti[:, None] // c == ti[None, :] // c), BF16)
    assert c == HEAD_DIM and RWKV_HEADS % GROUP_HEADS == 0
    gi = np.arange(GROUP_HEADS * c)
    same = (gi[:, None] // c) == (gi[None, :] // c)
    later = (gi[:, None] % c) > (gi[None, :] % c)
    masks = jnp.asarray(np.stack([same, same & later, same & (later | (gi[:, None] % c == gi[None, :] % c)),
                                  gi[:, None] == gi[None, :]]), F32)
    row = lambda a: a.reshape(1, -1)
    vec = lambda n: pl.BlockSpec((1, n), lambda b, t: (0, 0))
    full = lambda a: pl.BlockSpec(a.shape, lambda b, t: (0, 0))
    return pl.pallas_call(
        _rwkv_kernel,
        grid=(bsz, n_steps),
        in_specs=[pl.BlockSpec((tt, RWKV_COLS), lambda b, t: (b * n_steps + t, 0)),
                  vec(RWKV_COLS), vec(w), full(wda), vec(w), full(wg), vec(w), vec(w), vec(w),
                  vec(w), vec(w), full(blk), full(tril),
                  pl.BlockSpec(masks.shape, lambda b, t: (0, 0, 0))],
        out_specs=pl.BlockSpec((tt, w), lambda b, t: (b * n_steps + t, 0)),
        out_shape=jax.ShapeDtypeStruct((bsz * seq, w), BF16),
        scratch_shapes=[pltpu.VMEM((RWKV_HEADS // GROUP_HEADS, GROUP_HEADS * HEAD_DIM, GROUP_HEADS * HEAD_DIM), F32),
                        pltpu.VMEM((1, RWKV_COLS), F32)],
        compiler_params=_params(2),
        name="rwkv",
    )(p, row(mu), row(w0), wda, row(a0), wg, row(k_k), row(k_a), row(r_k), row(gn_g), row(gn_b),
      blk, tril, masks)


SLAB_ROWS = 64
ATTN_HEAD_GROUP = 4
ONES_ROWS = 16


def _fold_rows(x, op):
    acc = x[0:SLAB_ROWS]
    for r0 in range(SLAB_ROWS, x.shape[0], SLAB_ROWS):
        acc = op(acc, x[r0:r0 + SLAB_ROWS])
    return acc


def _split3(x):
    hi = x.astype(BF16)
    rest = x - hi.astype(F32)
    mid = rest.astype(BF16)
    return hi, mid, (rest - mid.astype(F32)).astype(BF16)


def _dsa_kernel(pd_ref, kvg_ref, kig_ref, wuk_ref, wuv_ref, tril_ref, wsel_ref, o_ref,
                ckv_s, ckvt_s, kcat_s, score_s, *, nq, n_cls, topk):
    qb = Q_BLOCK
    j = pl.program_id(1)

    @pl.when(j == 0)
    def _():
        ckv_s[...] = jnp.zeros_like(ckv_s)
        ckvt_s[...] = jnp.zeros_like(ckvt_s)
        kcat_s[...] = jnp.zeros_like(kcat_s)

    pd = pd_ref[...]
    q = pd[:, :DSA_WIDTH]
    ckv = pd[:, DSA_WIDTH:DSA_WIDTH + KV_LATENT]
    qi = pd[:, DSA_WIDTH + KV_LATENT:DSA_WIDTH + KV_LATENT + IDX_HEADS * IDX_DIM]
    tail = pd[:, DSA_WIDTH + KV_LATENT + IDX_HEADS * IDX_DIM:]
    ki = tail[:, :IDX_DIM]

    ckv_n = ckv * lax.rsqrt(jnp.mean(ckv * ckv, axis=-1, keepdims=True) + NORM_EPS) * kvg_ref[...]
    ki_n = ki * lax.rsqrt(jnp.mean(ki * ki, axis=-1, keepdims=True) + NORM_EPS) * kig_ref[...]
    ki_hi, ki_lo = _split(ki_n)
    rows = pl.ds(pl.multiple_of(j * qb, qb), qb)
    ckv_s[rows, :] = ckv_n.astype(BF16)
    one_row = jnp.where(lax.broadcasted_iota(I32, (ONES_ROWS, qb), 0) == 0, 1.0, 0.0)
    ckvt_s[j] = jnp.concatenate([ckv_n.T, one_row], axis=0).astype(BF16)
    kcat_s[rows, :] = jnp.concatenate([ki_hi, ki_hi, ki_lo, jnp.zeros_like(ki_hi)], axis=1)

    w_t = sum(_dot(wsel_ref[...], part, _NT) for part in _split3(tail))

    per = nq // n_cls
    for cls in range(n_cls):
        @pl.when((j >= cls * per) & (j < (cls + 1) * per))
        def _(sk=(cls + 1) * per * qb):
            _dsa_block(j, q, qi, w_t, wuk_ref, wuv_ref, tril_ref, o_ref, ckv_s, ckvt_s, kcat_s,
                       score_s, sk=sk, topk=topk)


def _dsa_block(j, q, qi, w_t, wuk_ref, wuv_ref, tril_ref, o_ref, ckv_s, ckvt_s, kcat_s, score_s,
               *, sk, topk):
    qb = Q_BLOCK
    kcat = kcat_s[0:sk, :]
    w_s = w_t * (IDX_HEADS ** -0.5)
    q_cat = []
    for hh in range(IDX_HEADS):
        q_hi, q_lo = _split(qi[:, hh * IDX_DIM:(hh + 1) * IDX_DIM] * (IDX_DIM ** -0.5))
        q_cat.append(jnp.concatenate([q_hi, q_lo, q_hi, jnp.zeros_like(q_hi)], axis=1))
    dots = [_dot(kcat, qc, _NT) for qc in q_cat]
    terms = [w_s[hh:hh + 1, :] * jnp.maximum(dots[hh], 0.0) for hh in range(IDX_HEADS)]
    score = sum(terms[1:], terms[0])

    key_pos = lax.broadcasted_iota(I32, (sk, qb), 0)
    q_pos = j * qb + lax.broadcasted_iota(I32, (sk, qb), 1)
    adm = jnp.right_shift(key_pos, ATTN_CHUNK_LOG2) <= jnp.right_shift(q_pos, ATTN_CHUNK_LOG2)

    score_s[0:sk, :] = jnp.where(adm, score, NEG_BIG)

    def as_float(okey):
        return lax.bitcast_convert_type(okey ^ ((okey >> 31) & 0x7FFFFFFF), F32)

    def count_ge(cand):
        acc = jnp.zeros((SLAB_ROWS, qb), F32)
        for r0 in range(0, sk, SLAB_ROWS):
            acc = acc + jnp.where(score_s[r0:r0 + SLAB_ROWS, :] >= cand, 1.0, 0.0)
        return jnp.sum(acc, axis=0, keepdims=True)

    def descend(i, tu):
        cand_u = tu | jnp.left_shift(jnp.int32(1), 31 - i)
        return jnp.where(count_ge(as_float(cand_u ^ INT_MIN)) >= topk, cand_u, tu)

    few_keys = (j + 1) * qb <= topk
    tu = lax.fori_loop(0, jnp.where(few_keys, 0, 32), descend, jnp.zeros((1, qb), I32))
    thr = jnp.where(few_keys, NEG_BIG, as_float(tu ^ INT_MIN))

    score = score_s[0:sk, :]
    gt = score > thr
    eq = score == thr
    need = topk - jnp.sum(_fold_rows(jnp.where(gt, 1.0, 0.0), jnp.add), axis=0, keepdims=True)
    eq_b = jnp.where(eq, 1.0, 0.0).astype(BF16)
    tb = tril_ref.shape[0]
    off = jnp.zeros((1, qb), F32)
    pieces = []
    for kb in range(sk // tb):
        pre = _dot(tril_ref[...], eq_b[kb * tb:(kb + 1) * tb, :])
        pieces.append(pre + off)
        off = off + pre[tb - 1:tb, :]
    prefix = jnp.concatenate(pieces, axis=0)
    tie_bias = jnp.where(prefix <= need, 0.0, NEG_BIG)
    bias = jnp.where(gt, 0.0, jnp.where(eq, tie_bias, NEG_BIG))
    bias = jnp.where(adm, bias, NEG_BIG)

    ckv_all = ckv_s[0:sk, :]
    ckv_t = jnp.concatenate([ckvt_s[b] for b in range(sk // qb)], axis=1)
    q_lat = (_dot(q.astype(BF16), wuk_ref[...]) * (HEAD_DIM ** -0.5)).astype(BF16)
    head_logits = lambda h: _dot(ckv_all, q_lat[:, h * KV_LATENT:(h + 1) * KV_LATENT], _NT)
    outs = []
    for h0 in range(0, DSA_HEADS, ATTN_HEAD_GROUP):
        logits = [head_logits(h) + bias for h in range(h0, h0 + ATTN_HEAD_GROUP)]
        mx = [jnp.max(_fold_rows(lg, jnp.maximum), axis=0, keepdims=True) for lg in logits]
        pr = [jnp.exp((lg - m).astype(BF16)) for lg, m in zip(logits, mx)]
        acc = [_dot(ckv_t, p) for p in pr]
        outs.extend((a[:KV_LATENT] / a[KV_LATENT:KV_LATENT + 1]).astype(BF16) for a in acc)
    o_lat_t = jnp.concatenate(outs, axis=0)
    o_ref[...] = _dot(o_lat_t, wuv_ref[...], _TN).astype(o_ref.dtype)


def _dsa(pd, bsz, seq, kv_g, ki_g, w_uk, w_uv):
    qb = Q_BLOCK
    nq = seq // qb
    topk = min(TOPK_MAX, seq // 4)
    hc = DSA_HEADS * KV_LATENT
    head_of_q = np.arange(DSA_WIDTH) // HEAD_DIM
    head_of_l = np.arange(hc) // KV_LATENT
    mask = jnp.asarray(head_of_q[:, None] == head_of_l[None, :], F32)
    uk = jnp.transpose(w_uk, (1, 2, 0)).reshape(DSA_WIDTH, KV_LATENT)
    wuk_bd = (jnp.tile(uk, (1, DSA_HEADS)) * mask).astype(BF16)
    uv = w_uv.reshape(KV_LATENT, DSA_WIDTH)
    wuv_bd = (jnp.tile(uv, (DSA_HEADS, 1)) * mask.T).astype(BF16)
    n_cls = max(d for d in (4, 2, 1) if nq % d == 0)
    tb = 256 if ((nq // n_cls) * qb) % 256 == 0 else 128
    tril = jnp.asarray(np.tril(np.ones((tb, tb), np.float32)), BF16)
    wsel = jnp.asarray(np.arange(128)[None, :] == (IDX_DIM + np.arange(8))[:, None], BF16)
    wsel = wsel * jnp.asarray(np.arange(8)[:, None] < IDX_HEADS, BF16)
    full = lambda a: pl.BlockSpec(a.shape, lambda b, j: (0, 0))
    return pl.pallas_call(
        functools.partial(_dsa_kernel, nq=nq, n_cls=n_cls, topk=topk),
        grid=(bsz, nq),
        in_specs=[pl.BlockSpec((qb, DSA_COLS_PAD), lambda b, j: (b * nq + j, 0)),
                  pl.BlockSpec((1, KV_LATENT), lambda b, j: (0, 0)),
                  pl.BlockSpec((1, IDX_DIM), lambda b, j: (0, 0)),
                  full(wuk_bd), full(wuv_bd), full(tril), full(wsel)],
        out_specs=pl.BlockSpec((qb, DSA_WIDTH), lambda b, j: (b * nq + j, 0)),
        out_shape=jax.ShapeDtypeStruct((bsz * seq, DSA_WIDTH), BF16),
        scratch_shapes=[pltpu.VMEM((seq, KV_LATENT), BF16),
                        pltpu.VMEM((nq, KV_LATENT + ONES_ROWS, qb), BF16),
                        pltpu.VMEM((seq, 4 * IDX_DIM), BF16),
                        pltpu.VMEM((seq, qb), F32)],
        compiler_params=_params(2),
        name="dsa",
    )(pd, kv_g.reshape(1, -1), ki_g.reshape(1, -1), wuk_bd, wuv_bd, tril, wsel)


def _mix_kernel(yr_ref, yd_ref, x_ref, wo1_ref, wo2_ref, g1_ref, n2_ref, sc_ref, sh_ref,
                wr_ref, br_ref, x1_ref, h2_ref, ids_ref, wts_ref):
    mixed = _dot(yr_ref[...], wo1_ref[...]) + _dot(yd_ref[...], wo2_ref[...])
    x1 = x_ref[...] + (1.0 + g1_ref[0]) * mixed
    x1_ref[...] = x1
    ms = jnp.mean(x1 * x1, axis=-1, keepdims=True)
    h2 = x1 * lax.rsqrt(ms + NORM_EPS) * n2_ref[...] * (1.0 + sc_ref[0]) + sh_ref[0]
    h2_ref[...] = h2

    lt = _dot3(wr_ref[...], h2, _NT) + br_ref[...]
    gl = lt[0:N_GROUPS]
    el = lt[8:8 + N_EXPERTS]
    tm = gl.shape[1]
    gmax = jnp.max(gl, axis=0, keepdims=True)
    gidx = lax.broadcasted_iota(I32, (N_GROUPS, tm), 0)
    gsel = jnp.min(jnp.where(gl == gmax, gidx, N_GROUPS), axis=0, keepdims=True)
    p_group = 1.0 / jnp.sum(jnp.exp(gl - gmax), axis=0, keepdims=True)
    eidx = lax.broadcasted_iota(I32, (N_EXPERTS, tm), 0)
    el = jnp.where(jnp.right_shift(eidx, 3) == gsel, el, NEG_BIG)
    m1 = jnp.max(el, axis=0, keepdims=True)
    i1 = jnp.min(jnp.where(el == m1, eidx, N_EXPERTS), axis=0, keepdims=True)
    el2 = jnp.where(eidx == i1, NEG_BIG, el)
    m2 = jnp.max(el2, axis=0, keepdims=True)
    i2 = jnp.min(jnp.where(el2 == m2, eidx, N_EXPERTS), axis=0, keepdims=True)
    e2 = jnp.exp(m2 - m1)
    w1 = p_group / (1.0 + e2)
    ids_ref[0] = jnp.concatenate([i1, i2], axis=0)
    wts_ref[0] = jnp.concatenate([w1, w1 * e2], axis=0)


def _mix(yr, yd, x2, wo1, wo2, mod, n2g, wr, br, seq, tm):
    n, d = x2.shape
    nt = n // tm
    tps = seq // tm
    modspec = lambda k: pl.BlockSpec((1, 1, d), lambda i: ((i // tps) * 6 + k, 0, 0))
    full = lambda a: pl.BlockSpec(a.shape, lambda i: (0,) * a.ndim)
    return pl.pallas_call(
        _mix_kernel,
        grid=(nt,),
        in_specs=[pl.BlockSpec((tm, yr.shape[1]), lambda i: (i, 0)),
                  pl.BlockSpec((tm, yd.shape[1]), lambda i: (i, 0)),
                  pl.BlockSpec((tm, d), lambda i: (i, 0)),
                  full(wo1), full(wo2), modspec(2), full(n2g), modspec(4), modspec(3),
                  full(wr), full(br)],
        out_specs=[pl.BlockSpec((tm, d), lambda i: (i, 0)),
                   pl.BlockSpec((tm, d), lambda i: (i, 0)),
                   pl.BlockSpec((1, 2, tm), lambda i: (i, 0, 0)),
                   pl.BlockSpec((1, 2, tm), lambda i: (i, 0, 0))],
        out_shape=[jax.ShapeDtypeStruct((n, d), F32),
                   jax.ShapeDtypeStruct((n, d), F32),
                   jax.ShapeDtypeStruct((nt, 2, tm), I32),
                   jax.ShapeDtypeStruct((nt, 2, tm), F32)],
        compiler_params=_params(1),
        name="mix",
    )(yr, yd, x2, wo1, wo2, mod, n2g, mod, mod, wr, br)


def _sort_kernel(ids_ref, triu_ref, tril_ref, dest_ref, bexp_ref, eend_ref,
                 cnt_ref, run_ref, start_ref, *, n_blocks_pad):
    phase = pl.program_id(0)
    i = pl.program_id(1)
    ids = ids_ref[0]
    tm = ids.shape[1]
    eidx = lax.broadcasted_iota(I32, (N_EXPERTS, tm), 0)
    hit0 = eidx == ids[0:1]
    hit1 = eidx == ids[1:2]
    onehot = jnp.where(hit0, 1.0, 0.0) + jnp.where(hit1, 1.0, 0.0)

    @pl.when((phase == 0) & (i == 0))
    def _():
        cnt_ref[...] = jnp.zeros_like(cnt_ref)

    @pl.when(phase == 0)
    def _():
        cnt_ref[...] += jnp.sum(onehot, axis=1, keepdims=True)

    @pl.when((phase == 1) & (i == 0))
    def _():
        run_ref[...] = jnp.zeros_like(run_ref)
        nblk = jnp.floor((cnt_ref[...] + (EXPERT_BLOCK - 1)) * (1.0 / EXPERT_BLOCK))
        nblk_b = jnp.broadcast_to(nblk, (N_EXPERTS, 128))
        first_blk = _dot_exact_rhs_lhs(tril_ref[...], nblk_b)
        start_ref[...] = first_blk[:, 0:1] * EXPERT_BLOCK
        end_blk = first_blk + nblk_b
        bidx = lax.broadcasted_iota(I32, (N_EXPERTS, n_blocks_pad), 1).astype(F32)
        owner = jnp.sum(jnp.where(end_blk[:, 0:1] <= bidx, 1.0, 0.0), axis=0, keepdims=True)
        bexp_ref[...] = jnp.minimum(owner, N_EXPERTS - 1).astype(I32)
        on_diag = (lax.broadcasted_iota(I32, (N_EXPERTS, 128), 0)
                   == lax.broadcasted_iota(I32, (N_EXPERTS, 128), 1))
        eend_ref[...] = jnp.sum(jnp.where(on_diag, end_blk, 0.0), axis=0, keepdims=True).astype(I32)

    @pl.when(phase == 1)
    def _():
        before = _dot(onehot.astype(BF16), triu_ref[...])
        pos = start_ref[...] + run_ref[...] + before
        d0 = jnp.sum(jnp.where(hit0, pos, 0.0), axis=0, keepdims=True)
        d1 = jnp.sum(jnp.where(hit1, pos, 0.0), axis=0, keepdims=True)
        dest_ref[0] = jnp.concatenate([d0, d1], axis=0).astype(I32)
        run_ref[...] += jnp.sum(onehot, axis=1, keepdims=True)


def _sort(ids, n_blocks_pad):
    nt, _, tm = ids.shape
    triu = jnp.asarray(np.triu(np.ones((tm, tm), np.float32), 1), BF16)
    tril = jnp.asarray(np.tril(np.ones((N_EXPERTS, N_EXPERTS), np.float32), -1), BF16)
    return pl.pallas_call(
        functools.partial(_sort_kernel, n_blocks_pad=n_blocks_pad),
        grid=(2, nt),
        in_specs=[pl.BlockSpec((1, 2, tm), lambda p, i: (i, 0, 0)),
                  pl.BlockSpec(triu.shape, lambda p, i: (0, 0)),
                  pl.BlockSpec(tril.shape, lambda p, i: (0, 0))],
        out_specs=[pl.BlockSpec((1, 2, tm), lambda p, i: (i * p, 0, 0)),
                   pl.BlockSpec((1, n_blocks_pad), lambda p, i: (0, 0)),
                   pl.BlockSpec((1, 128), lambda p, i: (0, 0))],
        out_shape=[jax.ShapeDtypeStruct((nt, 2, tm), I32),
                   jax.ShapeDtypeStruct((1, n_blocks_pad), I32),
                   jax.ShapeDtypeStruct((1, 128), I32)],
        scratch_shapes=[pltpu.VMEM((N_EXPERTS, 1), F32),
                        pltpu.VMEM((N_EXPERTS, 1), F32),
                        pltpu.VMEM((N_EXPERTS, 1), F32)],
        compiler_params=_params(2),
        name="sort",
    )(ids, triu, tril)


def _row_copy(src_ref, src_row, dst_ref, dst_row, sem):
    return pltpu.make_async_copy(src_ref.at[pl.ds(src_row, 1), :], dst_ref.at[pl.ds(dst_row, 1), :], sem)


ROW_UNROLL = 8


def _dispatch_kernel(dest_ref, eend_ref, h_ref, xs_ref, zbuf, sem, zsem):
    tm = h_ref.shape[0]

    @pl.when(pl.program_id(0) == 0)
    def _():
        zbuf[...] = jnp.zeros_like(zbuf)

        def last_block_copy(e):
            end_b = eend_ref[0, e]
            begin_b = jnp.where(e == 0, 0, eend_ref[0, jnp.maximum(e - 1, 0)])
            row0 = pl.multiple_of((end_b - 1) * EXPERT_BLOCK, EXPERT_BLOCK)
            return end_b > begin_b, pltpu.make_async_copy(zbuf, xs_ref.at[pl.ds(row0, EXPERT_BLOCK), :], zsem)

        def fill(e, carry):
            owns_rows, cp = last_block_copy(e)
            pl.when(owns_rows)(cp.start)
            return carry

        def drain(e, carry):
            owns_rows, cp = last_block_copy(e)
            pl.when(owns_rows)(cp.wait)
            return carry

        lax.fori_loop(0, N_EXPERTS, fill, 0)
        lax.fori_loop(0, N_EXPERTS, drain, 0)

        def spare_block_copy(b):
            row0 = pl.multiple_of(b * EXPERT_BLOCK, EXPERT_BLOCK)
            return pltpu.make_async_copy(zbuf, xs_ref.at[pl.ds(row0, EXPERT_BLOCK), :], zsem)

        n_used = eend_ref[0, N_EXPERTS - 1]
        n_blocks = xs_ref.shape[0] // EXPERT_BLOCK
        lax.fori_loop(n_used, n_blocks, lambda b, carry: (spare_block_copy(b).start(), carry)[1], 0)
        lax.fori_loop(n_used, n_blocks, lambda b, carry: (spare_block_copy(b).wait(), carry)[1], 0)

    def start(g, carry):
        for u in range(ROW_UNROLL):
            r = g * ROW_UNROLL + u
            _row_copy(h_ref, r, xs_ref, dest_ref[0, 0, r], sem).start(priority=u % 2)
            _row_copy(h_ref, r, xs_ref, dest_ref[0, 1, r], sem).start(priority=(u + 1) % 2)
        return carry

    lax.fori_loop(0, tm // ROW_UNROLL, start, 0)
    all_rows = pltpu.make_async_copy(h_ref, xs_ref.at[pl.ds(0, tm), :], sem)
    all_rows.wait()
    all_rows.wait()


def _dispatch(dest, eend, h2, n_rows):
    nt, _, tm = dest.shape
    n, d = h2.shape
    return pl.pallas_call(
        _dispatch_kernel,
        grid=(nt,),
        in_specs=[pl.BlockSpec((1, 2, tm), lambda i: (i, 0, 0), memory_space=pltpu.SMEM),
                  pl.BlockSpec(eend.shape, lambda i: (0, 0), memory_space=pltpu.SMEM),
                  pl.BlockSpec((tm, d), lambda i: (i, 0))],
        out_specs=pl.BlockSpec(memory_space=pl.ANY),
        out_shape=jax.ShapeDtypeStruct((n_rows, d), F32),
        scratch_shapes=[pltpu.VMEM((EXPERT_BLOCK, d), F32),
                        pltpu.SemaphoreType.DMA(()), pltpu.SemaphoreType.DMA(())],
        compiler_params=_params(1),
        name="dispatch",
    )(dest, eend, h2)


def _expert_kernel(bexp_ref, eend_ref, xs_ref, wg_ref, wu_ref, wd_ref, ys_ref, wg_b, wu_b, wd_b):
    i = pl.program_id(0)
    used = i < eend_ref[N_EXPERTS - 1]
    new_expert = jnp.logical_or(i == 0, bexp_ref[i] != bexp_ref[jnp.maximum(i - 1, 0)])

    @pl.when(jnp.logical_and(used, new_expert))
    def _():
        wg_b[...] = wg_ref[0].astype(BF16)
        wu_b[...] = wu_ref[0].astype(BF16)
        wd_b[...] = wd_ref[0].astype(BF16)

    @pl.when(used)
    def _():
        x = xs_ref[...].astype(BF16)
        hg = _dot(x, wg_b[...])
        hu = _dot(x, wu_b[...])
        hid = (hg * _sigmoid(hg) * hu).astype(BF16)
        ys_ref[...] = _dot(hid, wd_b[...])

    @pl.when(jnp.logical_not(used))
    def _():
        ys_ref[...] = jnp.zeros_like(ys_ref)


def _experts(bexp, eend, xs, e_gate, e_up, e_down, n_blocks):
    n_rows, d = xs.shape
    de = e_gate.shape[2]
    blk = EXPERT_BLOCK
    last = lambda i, nu: jnp.minimum(i, nu[N_EXPERTS - 1] - 1)
    rows = lambda i, be, nu: (last(i, nu), 0)
    wsel = lambda i, be, nu: (be[last(i, nu)], 0, 0)
    return pl.pallas_call(
        _expert_kernel,
        grid_spec=pltpu.PrefetchScalarGridSpec(
            num_scalar_prefetch=2,
            grid=(n_blocks,),
            in_specs=[pl.BlockSpec((blk, d), rows),
                      pl.BlockSpec((1, d, de), wsel),
                      pl.BlockSpec((1, d, de), wsel),
                      pl.BlockSpec((1, de, d), wsel)],
            out_specs=pl.BlockSpec((blk, d), lambda i, be, nu: (i, 0)),
            scratch_shapes=[pltpu.VMEM((d, de), BF16), pltpu.VMEM((d, de), BF16),
                            pltpu.VMEM((de, d), BF16)]),
        out_shape=jax.ShapeDtypeStruct((n_rows, d), F32),
        compiler_params=_params(1),
        name="experts",
    )(bexp, eend, xs, e_gate, e_up, e_down)


def _combine_kernel(dest_ref, dnext_ref, ys_ref, wts_ref, x1_ref, g2_ref, gf_ref, scf_ref, shf_ref, o_ref,
                    buf, sems):
    tm = x1_ref.shape[0]
    i = pl.program_id(0)
    cur = lax.rem(i, 2)

    def gather(d_ref, s):
        def start(g, carry):
            for u in range(ROW_UNROLL):
                r = g * ROW_UNROLL + u
                _row_copy(ys_ref, d_ref[0, 0, r], buf.at[s, 0], r, sems.at[s]).start(priority=u % 2)
                _row_copy(ys_ref, d_ref[0, 1, r], buf.at[s, 1], r, sems.at[s]).start(priority=(u + 1) % 2)
            return carry

        lax.fori_loop(0, tm // ROW_UNROLL, start, 0)

    @pl.when(i == 0)
    def _():
        gather(dest_ref, 0)

    @pl.when(i + 1 < pl.num_programs(0))
    def _():
        gather(dnext_ref, 1 - cur)

    all_rows = pltpu.make_async_copy(ys_ref.at[pl.ds(0, tm), :], buf.at[cur, 0], sems.at[cur])
    all_rows.wait()
    all_rows.wait()
    wts = wts_ref[...]
    moe = buf[cur, 0] * wts[:, 0:1] + buf[cur, 1] * wts[:, 1:2]
    x2 = x1_ref[...] + (1.0 + g2_ref[0]) * moe
    ms = jnp.mean(x2 * x2, axis=-1, keepdims=True)
    y = x2 * lax.rsqrt(ms + NORM_EPS) * gf_ref[...]
    o_ref[...] = y * (1.0 + scf_ref[0]) + shf_ref[0]


def _combine(dest, ys, wts_col, x1, mod, modf, gf, seq):
    nt, _, tm = dest.shape
    n, d = x1.shape
    tps = seq // tm
    return pl.pallas_call(
        _combine_kernel,
        grid=(nt,),
        in_specs=[pl.BlockSpec((1, 2, tm), lambda i: (i, 0, 0), memory_space=pltpu.SMEM),
                  pl.BlockSpec((1, 2, tm), lambda i: (jnp.minimum(i + 1, nt - 1), 0, 0),
                               memory_space=pltpu.SMEM),
                  pl.BlockSpec(memory_space=pl.ANY),
                  pl.BlockSpec((tm, 2), lambda i: (i, 0)),
                  pl.BlockSpec((tm, d), lambda i: (i, 0)),
                  pl.BlockSpec((1, 1, d), lambda i: ((i // tps) * 6 + 5, 0, 0)),
                  pl.BlockSpec((1, d), lambda i: (0, 0)),
                  pl.BlockSpec((1, 1, d), lambda i: ((i // tps) * 2 + 1, 0, 0)),
                  pl.BlockSpec((1, 1, d), lambda i: ((i // tps) * 2 + 0, 0, 0))],
        out_specs=pl.BlockSpec((tm, d), lambda i: (i, 0)),
        out_shape=jax.ShapeDtypeStruct((n, d), F32),
        scratch_shapes=[pltpu.VMEM((2, 2, tm, d), F32), pltpu.SemaphoreType.DMA((2,))],
        compiler_params=_params(1),
        name="combine",
    )(dest, dest, ys, wts_col, x1, mod, gf, modf, modf)


def kernel(x, c, ada_w, ada_b, norm1_g, w_in, shift_mu, w0, w_decay_up, a0, w_aaa_up, w_gate_up,
           k_k, k_a, r_k, gn_g, gn_b, kv_norm_g, k_idx_norm_g, w_uk, w_uv, w_out, norm2_g,
           w_group, b_group, w_expert, b_expert, e_gate, e_up, e_down,
           final_ada_w, final_ada_b, final_norm_g):
    bsz, seq, d = x.shape
    n = bsz * seq
    depth = ada_w.shape[0]
    tm = min(TOKEN_TILE, seq)
    x2 = x.reshape(n, d)

    modf = _ada(c, final_ada_w, final_ada_b).reshape(bsz * 2, 1, d)
    for l in range(depth):
        mod = _ada(c, ada_w[l], ada_b[l]).reshape(bsz * 6, 1, d)

        w1 = w_in[l][:, :RWKV_COLS].astype(BF16)
        w2 = jnp.pad(w_in[l][:, RWKV_COLS:], ((0, 0), (0, DSA_COLS_PAD - DSA_COLS))).astype(BF16)
        p_rwkv, p_dsa = _proj(x2, norm1_g[l].reshape(1, d), mod, w1, w2, seq, min(PROJ_TILE, seq))

        zeros = jnp.zeros((DECAY_LORA, RWKV_WIDTH), F32)
        wda = jnp.concatenate([jnp.concatenate([w_decay_up[l], zeros], axis=1),
                               jnp.concatenate([zeros, w_aaa_up[l]], axis=1)], axis=0).astype(BF16)
        y_rwkv = _rwkv(p_rwkv, bsz, seq, shift_mu[l], w0[l], wda, a0[l], w_gate_up[l].astype(BF16),
                       k_k[l], k_a[l], r_k[l], gn_g[l], gn_b[l])
        y_dsa = _dsa(p_dsa, bsz, seq, kv_norm_g[l], k_idx_norm_g[l], w_uk[l], w_uv[l])

        wr = jnp.zeros((ROUTER_ROWS, d), F32)
        wr = wr.at[0:N_GROUPS].set(w_group[l].T).at[8:8 + N_EXPERTS].set(w_expert[l].T)
        br = jnp.zeros((ROUTER_ROWS, 1), F32)
        br = br.at[0:N_GROUPS, 0].set(b_group[l]).at[8:8 + N_EXPERTS, 0].set(b_expert[l])
        wo = w_out[l].astype(BF16)
        x1, h2, ids, wts = _mix(y_rwkv, y_dsa, x2, wo[:RWKV_WIDTH], wo[RWKV_WIDTH:], mod,
                                norm2_g[l].reshape(1, d), wr, br, seq, tm)

        n_blocks = (n * 2) // EXPERT_BLOCK + N_EXPERTS
        n_blocks_pad = -(-n_blocks // 128) * 128
        dest, bexp, eend = _sort(ids, n_blocks_pad)
        xs = _dispatch(dest, eend, h2, n_blocks * EXPERT_BLOCK)
        ys = _experts(bexp.reshape(-1), eend.reshape(-1), xs, e_gate[l], e_up[l], e_down[l], n_blocks)
        wts_col = jnp.transpose(wts, (0, 2, 1)).reshape(n, 2)
        last = l == depth - 1
        if not last:
            raise NotImplementedError("stacked layers need a residual-only combine")
        x2 = _combine(dest, ys, wts_col, x1, mod, modf, final_norm_g.reshape(1, d), seq)
    return x2.reshape(bsz, seq, d)
```

```python
import functools

import numpy as np
import jax
import jax.numpy as jnp
from jax import lax
from jax.experimental import pallas as pl
from jax.experimental.pallas import tpu as pltpu

F32 = jnp.float32
BF16 = jnp.bfloat16
I32 = jnp.int32

HEAD_DIM = 64
RWKV_WIDTH = 512
RWKV_HEADS = RWKV_WIDTH // HEAD_DIM
GROUP_HEADS = 2
DSA_WIDTH = 512
DSA_HEADS = DSA_WIDTH // HEAD_DIM
DECAY_LORA = 64
AAA_LORA = 64
GATE_LORA = 128
KV_LATENT = 128
IDX_HEADS = 4
IDX_DIM = 64
RWKV_COLS = 3 * RWKV_WIDTH + DECAY_LORA + AAA_LORA + GATE_LORA
DSA_COLS = DSA_WIDTH + KV_LATENT + IDX_HEADS * IDX_DIM + IDX_DIM + IDX_HEADS
DSA_COLS_PAD = 1024
TOPK_MAX = 256
ATTN_CHUNK_LOG2 = 6
Q_BLOCK = 256
RWKV_CHUNK = 64
RWKV_STEP_CHUNKS = 4
N_GROUPS = 4
EXPERTS_PER_GROUP = 8
N_EXPERTS = N_GROUPS * EXPERTS_PER_GROUP
ROUTER_ROWS = 40
NORM_EPS = 1e-6
GN_EPS = HEAD_DIM * 1e-5
NEG_BIG = -1e30
INT_MIN = -2 ** 31

VMEM_LIMIT_BYTES = 56 * 1024 * 1024
EXPERT_BLOCK = 512
PROJ_TILE = 512
TOKEN_TILE = 1024

_NN = (((1,), (0,)), ((), ()))
_NT = (((1,), (1,)), ((), ()))
_TN = (((0,), (0,)), ((), ()))


def _dot(a, b, dims=_NN):
    return lax.dot_general(a, b, dims, preferred_element_type=F32)


def _split(x):
    hi = x.astype(BF16)
    lo = (x - hi.astype(F32)).astype(BF16)
    return hi, lo


def _dot3(a, b, dims=_NN):
    ah, al = _split(a)
    bh, bl = _split(b)
    return _dot(ah, bh, dims) + _dot(ah, bl, dims) + _dot(al, bh, dims)


def _dot_exact_rhs(a, b_bf16, dims=_NN):
    ah, al = _split(a)
    return _dot(ah, b_bf16, dims) + _dot(al, b_bf16, dims)


def _sigmoid(x):
    return 1.0 / (1.0 + jnp.exp(-x))


def _softplus(x):
    return jnp.maximum(x, 0.0) + jnp.log(1.0 + jnp.exp(-jnp.abs(x)))


def _params(n_axes):
    return pltpu.CompilerParams(dimension_semantics=("arbitrary",) * n_axes,
                                vmem_limit_bytes=VMEM_LIMIT_BYTES)


def _ada_kernel(c_ref, w_ref, b_ref, o_ref):
    c = c_ref[...]
    o_ref[...] = _dot3(c * _sigmoid(c), w_ref[...]) + b_ref[...]


def _ada(c, w, b):
    bsz, d = c.shape
    n = w.shape[1]
    tn = 1024
    return pl.pallas_call(
        _ada_kernel,
        grid=(n // tn,),
        in_specs=[pl.BlockSpec((bsz, d), lambda j: (0, 0)),
                  pl.BlockSpec((d, tn), lambda j: (0, j)),
                  pl.BlockSpec((1, tn), lambda j: (0, j))],
        out_specs=pl.BlockSpec((bsz, tn), lambda j: (0, j)),
        out_shape=jax.ShapeDtypeStruct((bsz, n), F32),
        compiler_params=_params(1),
        name="ada",
    )(c, w, b.reshape(1, n))


def _proj_kernel(x_ref, g_ref, sc_ref, sh_ref, w1_ref, w2_ref, o1_ref, o2_ref):
    x = x_ref[...]
    ms = jnp.mean(x * x, axis=-1, keepdims=True)
    y = x * lax.rsqrt(ms + NORM_EPS) * g_ref[...]
    h = (y * (1.0 + sc_ref[0]) + sh_ref[0]).astype(BF16)
    o1_ref[...] = _dot(h, w1_ref[...])
    o2_ref[...] = _dot(h, w2_ref[...])


def _proj(x2, g, mod, w1, w2, seq, tm):
    n, d = x2.shape
    tiles_per_seq = seq // tm
    return pl.pallas_call(
        _proj_kernel,
        grid=(n // tm,),
        in_specs=[pl.BlockSpec((tm, d), lambda i: (i, 0)),
                  pl.BlockSpec((1, d), lambda i: (0, 0)),
                  pl.BlockSpec((1, 1, d), lambda i: ((i // tiles_per_seq) * 6 + 1, 0, 0)),
                  pl.BlockSpec((1, 1, d), lambda i: ((i // tiles_per_seq) * 6 + 0, 0, 0)),
                  pl.BlockSpec(w1.shape, lambda i: (0, 0)),
                  pl.BlockSpec(w2.shape, lambda i: (0, 0))],
        out_specs=[pl.BlockSpec((tm, w1.shape[1]), lambda i: (i, 0)),
                   pl.BlockSpec((tm, w2.shape[1]), lambda i: (i, 0))],
        out_shape=[jax.ShapeDtypeStruct((n, w1.shape[1]), F32),
                   jax.ShapeDtypeStruct((n, w2.shape[1]), F32)],
        compiler_params=_params(1),
        name="proj",
    )(x2, g, mod, mod, w1, w2)


def _rwkv_kernel(p_ref, mu_ref, w0_ref, wda_ref, a0_ref, wg_ref, kk_ref, ka_ref, rk_ref,
                 gng_ref, gnb_ref, blk_ref, tril_ref, masks_ref, o_ref, s_ref, prev_ref):
    c = RWKV_CHUNK
    w = RWKV_WIDTH
    tt = p_ref.shape[0]
    chunks = range(tt // c)

    @pl.when(pl.program_id(1) == 0)
    def _():
        s_ref[...] = jnp.zeros_like(s_ref)
        prev_ref[...] = jnp.zeros_like(prev_ref)

    p = p_ref[...]
    row = lax.broadcasted_iota(I32, p.shape, 0)
    p_prev = jnp.where(row == 0, prev_ref[...], pltpu.roll(p, 1, 0))
    prev_ref[...] = p[tt - 1:tt, :]
    ps = p + mu_ref[...] * (p_prev - p)

    r = ps[:, 0:w]
    k = ps[:, w:2 * w]
    v = ps[:, 2 * w:3 * w]
    lora_in = ps[:, 3 * w:3 * w + DECAY_LORA + AAA_LORA]
    gate_in = ps[:, 3 * w + DECAY_LORA + AAA_LORA:]

    lane = lax.broadcasted_iota(I32, lora_in.shape, 1)
    lora_act = jnp.where(lane < DECAY_LORA, jnp.tanh(lora_in), lora_in).astype(BF16)
    da = _dot(lora_act, wda_ref[...])
    log_w = -_softplus(-(w0_ref[...] + da[:, :w])) - 0.5
    ld = -jnp.exp(log_w)
    a = _sigmoid(a0_ref[...] + da[:, w:])
    gate = _dot(_sigmoid(gate_in).astype(BF16), wg_ref[...])

    blk = blk_ref[...]

    def head_sums(z):
        nb = w // blk.shape[0]
        rows = jnp.concatenate([z[:, i * 128:(i + 1) * 128] for i in range(nb)], axis=0)
        sums = _dot(rows.astype(BF16), blk)
        return jnp.concatenate([sums[i * tt:(i + 1) * tt] for i in range(nb)], axis=1)

    kk = k * kk_ref[...]
    kk = kk / jnp.maximum(jnp.sqrt(head_sums(kk * kk)), 1e-12)
    k2 = k * (1.0 + (a - 1.0) * ka_ref[...])

    cs = _dot_exact_rhs_lhs(tril_ref[...], ld)
    cs_last = [cs[(ch + 1) * c - 1:(ch + 1) * c, :] for ch in chunks]
    cs_end = jnp.concatenate([jnp.broadcast_to(cl, (c, w)) for cl in cs_last], axis=0)
    e_neg = jnp.exp(-cs)
    e_rem = jnp.exp(cs_end - cs)
    kka = kk * a
    a_t = (-kk) * jnp.exp(cs - ld)
    b_t = kka * e_neg
    k_t = k2 * e_neg
    r_t = r * jnp.exp(cs)
    b_h = kka * e_rem
    k_h = k2 * e_rem
    w_c = [jnp.exp(cl) for cl in cs_last]

    bd = masks_ref[0]
    strict = masks_ref[1]
    incl = masks_ref[2]
    eye = masks_ref[3]
    gw = GROUP_HEADS * HEAD_DIM
    stack = lambda z: jnp.concatenate([z] * GROUP_HEADS, axis=0)

    groups = range(RWKV_HEADS // GROUP_HEADS)
    units = [(ch, g) for ch in chunks for g in groups]
    piece = lambda z, u: z[u[0] * c:(u[0] + 1) * c, u[1] * gw:(u[1] + 1) * gw]
    v_x = [(stack(piece(v, u)) * bd).astype(BF16) for u in units]
    ar_x = [jnp.concatenate([(stack(piece(a_t, u)) * bd).astype(BF16),
                             (stack(piece(r_t, u)) * bd).astype(BF16)], axis=0) for u in units]
    bk_r = [jnp.concatenate([stack(piece(b_t, u).astype(BF16)), stack(piece(k_t, u).astype(BF16))], axis=0)
            for u in units]
    bk_h = [jnp.concatenate([stack(piece(b_h, u).astype(BF16)), stack(piece(k_h, u).astype(BF16))], axis=0)
            for u in units]
    quad = [_dot(x, y, _NT) for x, y in zip(ar_x, bk_r)]
    l_ab = [q4[:gw, :gw] * strict for q4 in quad]
    l_ak = [(q4[:gw, gw:] * strict).astype(BF16) for q4 in quad]
    m_cat = [jnp.concatenate([(q4[gw:, :gw] * incl).astype(BF16), (q4[gw:, gw:] * incl).astype(BF16)], axis=1)
             for q4 in quad]

    pw = [p.astype(BF16) for p in l_ab]
    t_inv = [eye + l for l in l_ab]
    pw = [_dot(p, p) for p in pw]
    for _ in range(4):
        pb = [p.astype(BF16) for p in pw]
        both = [_dot(jnp.concatenate([t.astype(BF16), p], axis=0), p) for t, p in zip(t_inv, pb)]
        t_inv = [t + tp[:gw] for t, tp in zip(t_inv, both)]
        pw = [tp[gw:] for tp in both]
    t_inv = [(t + _dot(t.astype(BF16), p.astype(BF16))).astype(BF16) for t, p in zip(t_inv, pw)]

    state = [s_ref[g] for g in groups]
    y_rows = []
    for ch in chunks:
        us = [ch * len(groups) + g for g in groups]
        ar_s = [_dot(ar_x[u], state[g].astype(BF16), _NT) for g, u in zip(groups, us)]
        rhs = [ar_s[g][:gw] + _dot(l_ak[u], v_x[u]) for g, u in zip(groups, us)]
        ub = [_dot(t_inv[u], rhs[g].astype(BF16)).astype(BF16) for g, u in zip(groups, us)]
        uv = [jnp.concatenate([ub[g], v_x[u]], axis=0) for g, u in zip(groups, us)]
        y_x = [ar_s[g][gw:] + _dot(m_cat[u], uv[g]) for g, u in zip(groups, us)]
        y_rows.append(jnp.concatenate(
            [sum(yx[hh * c:(hh + 1) * c] for hh in range(1, GROUP_HEADS)) + yx[0:c] for yx in y_x], axis=1))
        state = [(state[g] * w_c[ch][:, g * gw:(g + 1) * gw] + _dot(uv[g], bk_h[u], _TN)) * bd
                 for g, u in zip(groups, us)]
    for g in groups:
        s_ref[g] = state[g]

    y = jnp.concatenate(y_rows, axis=0)
    inv_n = 1.0 / HEAD_DIM
    mean = head_sums(y) * inv_n
    dlt = y - mean
    var = head_sums(dlt * dlt) * inv_n
    yn = dlt * lax.rsqrt(var + GN_EPS) * gng_ref[...] + gnb_ref[...]
    bonus = head_sums(r * k2 * rk_ref[...]) * v
    o_ref[...] = ((yn + bonus) * gate).astype(o_ref.dtype)


def _dot_exact_rhs_lhs(a_bf16, b):
    bh, bl = _split(b)
    return _dot(a_bf16, bh) + _dot(a_bf16, bl)


def _rwkv(p, bsz, seq, mu, w0, wda, a0, wg, k_k, k_a, r_k, gn_g, gn_b):
    c = RWKV_CHUNK
    w = RWKV_WIDTH
    tt = c * RWKV_STEP_CHUNKS if seq % (c * RWKV_STEP_CHUNKS) == 0 else c
    n_steps = seq // tt
    head_of = np.arange(128) // HEAD_DIM
    blk = jnp.asarray(head_of[:, None] == head_of[None, :], BF16)
    ti = np.arange(tt)
    tril = jnp.asarray((ti[:, None] >= ti[None, :]) & (ti[:, None] // c == ti[None, :] // c), BF16)
    assert c == HEAD_DIM and RWKV_HEADS % GROUP_HEADS == 0
    gi = np.arange(GROUP_HEADS * c)
    same = (gi[:, None] // c) == (gi[None, :] // c)
    later = (gi[:, None] % c) > (gi[None, :] % c)
    masks = jnp.asarray(np.stack([same, same & later, same & (later | (gi[:, None] % c == gi[None, :] % c)),
                                  gi[:, None] == gi[None, :]]), F32)
    row = lambda a: a.reshape(1, -1)
    vec = lambda n: pl.BlockSpec((1, n), lambda b, t: (0, 0))
    full = lambda a: pl.BlockSpec(a.shape, lambda b, t: (0, 0))
    return pl.pallas_call(
        _rwkv_kernel,
        grid=(bsz, n_steps),
        in_specs=[pl.BlockSpec((tt, RWKV_COLS), lambda b, t: (b * n_steps + t, 0)),
                  vec(RWKV_COLS), vec(w), full(wda), vec(w), full(wg), vec(w), vec(w), vec(w),
                  vec(w), vec(w), full(blk), full(tril),
                  pl.BlockSpec(masks.shape, lambda b, t: (0, 0, 0))],
        out_specs=pl.BlockSpec((tt, w), lambda b, t: (b * n_steps + t, 0)),
        out_shape=jax.ShapeDtypeStruct((bsz * seq, w), BF16),
        scratch_shapes=[pltpu.VMEM((RWKV_HEADS // GROUP_HEADS, GROUP_HEADS * HEAD_DIM, GROUP_HEADS * HEAD_DIM), F32),
                        pltpu.VMEM((1, RWKV_COLS), F32)],
        compiler_params=_params(2),
        name="rwkv",
    )(p, row(mu), row(w0), wda, row(a0), wg, row(k_k), row(k_a), row(r_k), row(gn_g), row(gn_b),
      blk, tril, masks)


SLAB_ROWS = 64
LOW_BITS = 12
WALK_STEPS = 3
ATTN_HEAD_GROUP = 2
ONES_ROWS = 16


def _fold_rows(x, op):
    acc = x[0:SLAB_ROWS]
    for r0 in range(SLAB_ROWS, x.shape[0], SLAB_ROWS):
        acc = op(acc, x[r0:r0 + SLAB_ROWS])
    return acc


def _split3(x):
    hi = x.astype(BF16)
    rest = x - hi.astype(F32)
    mid = rest.astype(BF16)
    return hi, mid, (rest - mid.astype(F32)).astype(BF16)


def _dsa_kernel(pd_ref, kvg_ref, kig_ref, wuk_ref, wuv_ref, tril_ref, wsel_ref, o_ref,
                ckv_s, ckvt_s, kcat_s, score_s, *, nq, n_cls, topk):
    qb = Q_BLOCK
    j = pl.program_id(1)

    @pl.when(j == 0)
    def _():
        ckv_s[...] = jnp.zeros_like(ckv_s)
        ckvt_s[...] = jnp.zeros_like(ckvt_s)
        kcat_s[...] = jnp.zeros_like(kcat_s)

    pd = pd_ref[...]
    q = pd[:, :DSA_WIDTH]
    ckv = pd[:, DSA_WIDTH:DSA_WIDTH + KV_LATENT]
    qi = pd[:, DSA_WIDTH + KV_LATENT:DSA_WIDTH + KV_LATENT + IDX_HEADS * IDX_DIM]
    tail = pd[:, DSA_WIDTH + KV_LATENT + IDX_HEADS * IDX_DIM:]
    ki = tail[:, :IDX_DIM]

    ckv_n = ckv * lax.rsqrt(jnp.mean(ckv * ckv, axis=-1, keepdims=True) + NORM_EPS) * kvg_ref[...]
    ki_n = ki * lax.rsqrt(jnp.mean(ki * ki, axis=-1, keepdims=True) + NORM_EPS) * kig_ref[...]
    ki_hi, ki_lo = _split(ki_n)
    rows = pl.ds(pl.multiple_of(j * qb, qb), qb)
    ckv_s[rows, :] = ckv_n.astype(BF16)
    one_row = jnp.where(lax.broadcasted_iota(I32, (ONES_ROWS, qb), 0) == 0, 1.0, 0.0)
    ckvt_s[j] = jnp.concatenate([ckv_n.T, one_row], axis=0).astype(BF16)
    kcat_s[rows, :] = jnp.concatenate([ki_hi, ki_hi, ki_lo, jnp.zeros_like(ki_hi)], axis=1)

    w_t = sum(_dot(wsel_ref[...], part, _NT) for part in _split3(tail))

    per = nq // n_cls
    for cls in range(n_cls):
        @pl.when((j >= cls * per) & (j < (cls + 1) * per))
        def _(sk=(cls + 1) * per * qb):
            _dsa_block(j, q, qi, w_t, wuk_ref, wuv_ref, tril_ref, o_ref, ckv_s, ckvt_s, kcat_s,
                       score_s, sk=sk, topk=topk)


def _dsa_block(j, q, qi, w_t, wuk_ref, wuv_ref, tril_ref, o_ref, ckv_s, ckvt_s, kcat_s, score_s,
               *, sk, topk):
    qb = Q_BLOCK
    kcat = kcat_s[0:sk, :]
    w_s = w_t * (IDX_HEADS ** -0.5)
    q_cat = []
    for hh in range(IDX_HEADS):
        q_hi, q_lo = _split(qi[:, hh * IDX_DIM:(hh + 1) * IDX_DIM] * (IDX_DIM ** -0.5))
        q_cat.append(jnp.concatenate([q_hi, q_lo, q_hi, jnp.zeros_like(q_hi)], axis=1))
    dots = [_dot(kcat, qc, _NT) for qc in q_cat]
    terms = [w_s[hh:hh + 1, :] * jnp.maximum(dots[hh], 0.0) for hh in range(IDX_HEADS)]
    score = sum(terms[1:], terms[0])

    key_pos = lax.broadcasted_iota(I32, (sk, qb), 0)
    q_pos = j * qb + lax.broadcasted_iota(I32, (sk, qb), 1)
    adm = jnp.right_shift(key_pos, ATTN_CHUNK_LOG2) <= jnp.right_shift(q_pos, ATTN_CHUNK_LOG2)

    score_s[0:sk, :] = jnp.where(adm, score, NEG_BIG)

    def as_float(okey):
        return lax.bitcast_convert_type(okey ^ ((okey >> 31) & 0x7FFFFFFF), F32)

    def count_ge(cand):
        acc = jnp.zeros((SLAB_ROWS, qb), F32)
        for r0 in range(0, sk, SLAB_ROWS):
            acc = acc + jnp.where(score_s[r0:r0 + SLAB_ROWS, :] >= cand, 1.0, 0.0)
        return jnp.sum(acc, axis=0, keepdims=True)

    def descend(i, tu):
        cand_u = tu | jnp.left_shift(jnp.int32(1), 31 - i)
        return jnp.where(count_ge(as_float(cand_u ^ INT_MIN)) >= topk, cand_u, tu)

    few_keys = (j + 1) * qb <= topk
    high = 32 - LOW_BITS
    tu = lax.fori_loop(0, jnp.where(few_keys, 0, high), descend, jnp.zeros((1, qb), I32))

    bound = as_float((tu + (1 << LOW_BITS)) ^ INT_MIN)
    guess = jnp.full((1, qb), -jnp.inf, F32)
    settled = jnp.zeros((1, qb), jnp.bool_)
    for _ in range(WALK_STEPS):
        best = jnp.full((SLAB_ROWS, qb), -jnp.inf, F32)
        for r0 in range(0, sk, SLAB_ROWS):
            slab = score_s[r0:r0 + SLAB_ROWS, :]
            best = jnp.maximum(best, jnp.where(slab < bound, slab, -jnp.inf))
        best = jnp.max(best, axis=0, keepdims=True)
        reached = (count_ge(best) >= topk) & (best > -jnp.inf)
        guess = jnp.where(reached & ~settled, best, guess)
        settled = settled | reached
        bound = jnp.where(settled, bound, best)
    missed = jnp.max(jnp.where(settled, 0, 1))
    tu = lax.fori_loop(high, jnp.where(few_keys, high, high + LOW_BITS * missed), descend, tu)
    thr = jnp.where(few_keys, NEG_BIG, jnp.where(missed == 1, as_float(tu ^ INT_MIN), guess))

    score = score_s[0:sk, :]
    gt = score > thr
    eq = score == thr
    need = topk - jnp.sum(_fold_rows(jnp.where(gt, 1.0, 0.0), jnp.add), axis=0, keepdims=True)
    eq_b = jnp.where(eq, 1.0, 0.0).astype(BF16)
    tb = tril_ref.shape[0]
    off = jnp.zeros((1, qb), F32)
    pieces = []
    for kb in range(sk // tb):
        pre = _dot(tril_ref[...], eq_b[kb * tb:(kb + 1) * tb, :])
        pieces.append(pre + off)
        off = off + pre[tb - 1:tb, :]
    prefix = jnp.concatenate(pieces, axis=0)
    tie_bias = jnp.where(prefix <= need, 0.0, NEG_BIG)
    bias = jnp.where(gt, 0.0, jnp.where(eq, tie_bias, NEG_BIG))
    bias = jnp.where(adm, bias, NEG_BIG)

    ckv_all = ckv_s[0:sk, :]
    ckv_t = jnp.concatenate([ckvt_s[b] for b in range(sk // qb)], axis=1)
    q_lat = (_dot(q.astype(BF16), wuk_ref[...]) * (HEAD_DIM ** -0.5)).astype(BF16)
    head_logits = lambda h: _dot(ckv_all, q_lat[:, h * KV_LATENT:(h + 1) * KV_LATENT], _NT)
    outs = []
    groups = [range(h0, h0 + ATTN_HEAD_GROUP) for h0 in range(0, DSA_HEADS, ATTN_HEAD_GROUP)]
    nxt = [head_logits(h) + bias for h in groups[0]]
    for gi in range(len(groups)):
        logits = nxt
        if gi + 1 < len(groups):
            nxt = [head_logits(h) + bias for h in groups[gi + 1]]
        mx = [jnp.max(_fold_rows(lg, jnp.maximum), axis=0, keepdims=True) for lg in logits]
        pr = [jnp.exp((lg - m).astype(BF16)) for lg, m in zip(logits, mx)]
        acc = [_dot(ckv_t, p) for p in pr]
        outs.extend((a[:KV_LATENT] / a[KV_LATENT:KV_LATENT + 1]).astype(BF16) for a in acc)
    o_lat_t = jnp.concatenate(outs, axis=0)
    o_ref[...] = _dot(o_lat_t, wuv_ref[...], _TN).astype(o_ref.dtype)


def _dsa(pd, bsz, seq, kv_g, ki_g, w_uk, w_uv):
    qb = Q_BLOCK
    nq = seq // qb
    topk = min(TOPK_MAX, seq // 4)
    hc = DSA_HEADS * KV_LATENT
    head_of_q = np.arange(DSA_WIDTH) // HEAD_DIM
    head_of_l = np.arange(hc) // KV_LATENT
    mask = jnp.asarray(head_of_q[:, None] == head_of_l[None, :], F32)
    uk = jnp.transpose(w_uk, (1, 2, 0)).reshape(DSA_WIDTH, KV_LATENT)
    wuk_bd = (jnp.tile(uk, (1, DSA_HEADS)) * mask).astype(BF16)
    uv = w_uv.reshape(KV_LATENT, DSA_WIDTH)
    wuv_bd = (jnp.tile(uv, (DSA_HEADS, 1)) * mask.T).astype(BF16)
    n_cls = max(d for d in (4, 2, 1) if nq % d == 0)
    tb = 256 if ((nq // n_cls) * qb) % 256 == 0 else 128
    tril = jnp.asarray(np.tril(np.ones((tb, tb), np.float32)), BF16)
    wsel = jnp.asarray(np.arange(128)[None, :] == (IDX_DIM + np.arange(8))[:, None], BF16)
    wsel = wsel * jnp.asarray(np.arange(8)[:, None] < IDX_HEADS, BF16)
    full = lambda a: pl.BlockSpec(a.shape, lambda b, j: (0, 0))
    return pl.pallas_call(
        functools.partial(_dsa_kernel, nq=nq, n_cls=n_cls, topk=topk),
        grid=(bsz, nq),
        in_specs=[pl.BlockSpec((qb, DSA_COLS_PAD), lambda b, j: (b * nq + j, 0)),
                  pl.BlockSpec((1, KV_LATENT), lambda b, j: (0, 0)),
                  pl.BlockSpec((1, IDX_DIM), lambda b, j: (0, 0)),
                  full(wuk_bd), full(wuv_bd), full(tril), full(wsel)],
        out_specs=pl.BlockSpec((qb, DSA_WIDTH), lambda b, j: (b * nq + j, 0)),
        out_shape=jax.ShapeDtypeStruct((bsz * seq, DSA_WIDTH), BF16),
        scratch_shapes=[pltpu.VMEM((seq, KV_LATENT), BF16),
                        pltpu.VMEM((nq, KV_LATENT + ONES_ROWS, qb), BF16),
                        pltpu.VMEM((seq, 4 * IDX_DIM), BF16),
                        pltpu.VMEM((seq, qb), F32)],
        compiler_params=_params(2),
        name="dsa",
    )(pd, kv_g.reshape(1, -1), ki_g.reshape(1, -1), wuk_bd, wuv_bd, tril, wsel)


def _mix_kernel(yr_ref, yd_ref, x_ref, wo1_ref, wo2_ref, g1_ref, n2_ref, sc_ref, sh_ref,
                wr_ref, br_ref, x1_ref, h2_ref, ids_ref, wts_ref):
    mixed = _dot(yr_ref[...], wo1_ref[...]) + _dot(yd_ref[...], wo2_ref[...])
    x1 = x_ref[...] + (1.0 + g1_ref[0]) * mixed
    x1_ref[...] = x1
    ms = jnp.mean(x1 * x1, axis=-1, keepdims=True)
    h2 = x1 * lax.rsqrt(ms + NORM_EPS) * n2_ref[...] * (1.0 + sc_ref[0]) + sh_ref[0]
    h2_ref[...] = h2

    lt = _dot3(wr_ref[...], h2, _NT) + br_ref[...]
    gl = lt[0:N_GROUPS]
    el = lt[8:8 + N_EXPERTS]
    tm = gl.shape[1]
    gmax = jnp.max(gl, axis=0, keepdims=True)
    gidx = lax.broadcasted_iota(I32, (N_GROUPS, tm), 0)
    gsel = jnp.min(jnp.where(gl == gmax, gidx, N_GROUPS), axis=0, keepdims=True)
    p_group = 1.0 / jnp.sum(jnp.exp(gl - gmax), axis=0, keepdims=True)
    eidx = lax.broadcasted_iota(I32, (N_EXPERTS, tm), 0)
    el = jnp.where(jnp.right_shift(eidx, 3) == gsel, el, NEG_BIG)
    m1 = jnp.max(el, axis=0, keepdims=True)
    i1 = jnp.min(jnp.where(el == m1, eidx, N_EXPERTS), axis=0, keepdims=True)
    el2 = jnp.where(eidx == i1, NEG_BIG, el)
    m2 = jnp.max(el2, axis=0, keepdims=True)
    i2 = jnp.min(jnp.where(el2 == m2, eidx, N_EXPERTS), axis=0, keepdims=True)
    e2 = jnp.exp(m2 - m1)
    w1 = p_group / (1.0 + e2)
    ids_ref[0] = jnp.concatenate([i1, i2], axis=0)
    wts_ref[0] = jnp.concatenate([w1, w1 * e2], axis=0)


def _mix(yr, yd, x2, wo1, wo2, mod, n2g, wr, br, seq, tm):
    n, d = x2.shape
    nt = n // tm
    tps = seq // tm
    modspec = lambda k: pl.BlockSpec((1, 1, d), lambda i: ((i // tps) * 6 + k, 0, 0))
    full = lambda a: pl.BlockSpec(a.shape, lambda i: (0,) * a.ndim)
    return pl.pallas_call(
        _mix_kernel,
        grid=(nt,),
        in_specs=[pl.BlockSpec((tm, yr.shape[1]), lambda i: (i, 0)),
                  pl.BlockSpec((tm, yd.shape[1]), lambda i: (i, 0)),
                  pl.BlockSpec((tm, d), lambda i: (i, 0)),
                  full(wo1), full(wo2), modspec(2), full(n2g), modspec(4), modspec(3),
                  full(wr), full(br)],
        out_specs=[pl.BlockSpec((tm, d), lambda i: (i, 0)),
                   pl.BlockSpec((tm, d), lambda i: (i, 0)),
                   pl.BlockSpec((1, 2, tm), lambda i: (i, 0, 0)),
                   pl.BlockSpec((1, 2, tm), lambda i: (i, 0, 0))],
        out_shape=[jax.ShapeDtypeStruct((n, d), F32),
                   jax.ShapeDtypeStruct((n, d), F32),
                   jax.ShapeDtypeStruct((nt, 2, tm), I32),
                   jax.ShapeDtypeStruct((nt, 2, tm), F32)],
        compiler_params=_params(1),
        name="mix",
    )(yr, yd, x2, wo1, wo2, mod, n2g, mod, mod, wr, br)


def _sort_kernel(ids_ref, triu_ref, tril_ref, dest_ref, bexp_ref, eend_ref,
                 cnt_ref, run_ref, start_ref, *, n_blocks_pad):
    phase = pl.program_id(0)
    i = pl.program_id(1)
    ids = ids_ref[0]
    tm = ids.shape[1]
    eidx = lax.broadcasted_iota(I32, (N_EXPERTS, tm), 0)
    hit0 = eidx == ids[0:1]
    hit1 = eidx == ids[1:2]
    onehot = jnp.where(hit0, 1.0, 0.0) + jnp.where(hit1, 1.0, 0.0)

    @pl.when((phase == 0) & (i == 0))
    def _():
        cnt_ref[...] = jnp.zeros_like(cnt_ref)

    @pl.when(phase == 0)
    def _():
        cnt_ref[...] += jnp.sum(onehot, axis=1, keepdims=True)

    @pl.when((phase == 1) & (i == 0))
    def _():
        run_ref[...] = jnp.zeros_like(run_ref)
        nblk = jnp.floor((cnt_ref[...] + (EXPERT_BLOCK - 1)) * (1.0 / EXPERT_BLOCK))
        nblk_b = jnp.broadcast_to(nblk, (N_EXPERTS, 128))
        first_blk = _dot_exact_rhs_lhs(tril_ref[...], nblk_b)
        start_ref[...] = first_blk[:, 0:1] * EXPERT_BLOCK
        end_blk = first_blk + nblk_b
        bidx = lax.broadcasted_iota(I32, (N_EXPERTS, n_blocks_pad), 1).astype(F32)
        owner = jnp.sum(jnp.where(end_blk[:, 0:1] <= bidx, 1.0, 0.0), axis=0, keepdims=True)
        bexp_ref[...] = jnp.minimum(owner, N_EXPERTS - 1).astype(I32)
        on_diag = (lax.broadcasted_iota(I32, (N_EXPERTS, 128), 0)
                   == lax.broadcasted_iota(I32, (N_EXPERTS, 128), 1))
        eend_ref[...] = jnp.sum(jnp.where(on_diag, end_blk, 0.0), axis=0, keepdims=True).astype(I32)

    @pl.when(phase == 1)
    def _():
        before = _dot(onehot.astype(BF16), triu_ref[...])
        pos = start_ref[...] + run_ref[...] + before
        d0 = jnp.sum(jnp.where(hit0, pos, 0.0), axis=0, keepdims=True)
        d1 = jnp.sum(jnp.where(hit1, pos, 0.0), axis=0, keepdims=True)
        dest_ref[0] = jnp.concatenate([d0, d1], axis=0).astype(I32)
        run_ref[...] += jnp.sum(onehot, axis=1, keepdims=True)


def _sort(ids, n_blocks_pad):
    nt, _, tm = ids.shape
    triu = jnp.asarray(np.triu(np.ones((tm, tm), np.float32), 1), BF16)
    tril = jnp.asarray(np.tril(np.ones((N_EXPERTS, N_EXPERTS), np.float32), -1), BF16)
    return pl.pallas_call(
        functools.partial(_sort_kernel, n_blocks_pad=n_blocks_pad),
        grid=(2, nt),
        in_specs=[pl.BlockSpec((1, 2, tm), lambda p, i: (i, 0, 0)),
                  pl.BlockSpec(triu.shape, lambda p, i: (0, 0)),
                  pl.BlockSpec(tril.shape, lambda p, i: (0, 0))],
        out_specs=[pl.BlockSpec((1, 2, tm), lambda p, i: (i * p, 0, 0)),
                   pl.BlockSpec((1, n_blocks_pad), lambda p, i: (0, 0)),
                   pl.BlockSpec((1, 128), lambda p, i: (0, 0))],
        out_shape=[jax.ShapeDtypeStruct((nt, 2, tm), I32),
                   jax.ShapeDtypeStruct((1, n_blocks_pad), I32),
                   jax.ShapeDtypeStruct((1, 128), I32)],
        scratch_shapes=[pltpu.VMEM((N_EXPERTS, 1), F32),
                        pltpu.VMEM((N_EXPERTS, 1), F32),
                        pltpu.VMEM((N_EXPERTS, 1), F32)],
        compiler_params=_params(2),
        name="sort",
    )(ids, triu, tril)


def _row_copy(src_ref, src_row, dst_ref, dst_row, sem):
    return pltpu.make_async_copy(src_ref.at[pl.ds(src_row, 1), :], dst_ref.at[pl.ds(dst_row, 1), :], sem)


ROW_UNROLL = 8


def _dispatch_kernel(dest_ref, eend_ref, h_ref, xs_ref, zbuf, sem, zsem):
    tm = h_ref.shape[0]

    @pl.when(pl.program_id(0) == 0)
    def _():
        zbuf[...] = jnp.zeros_like(zbuf)

        def last_block_copy(e):
            end_b = eend_ref[0, e]
            begin_b = jnp.where(e == 0, 0, eend_ref[0, jnp.maximum(e - 1, 0)])
            row0 = pl.multiple_of((end_b - 1) * EXPERT_BLOCK, EXPERT_BLOCK)
            return end_b > begin_b, pltpu.make_async_copy(zbuf, xs_ref.at[pl.ds(row0, EXPERT_BLOCK), :], zsem)

        def fill(e, carry):
            owns_rows, cp = last_block_copy(e)
            pl.when(owns_rows)(cp.start)
            return carry

        def drain(e, carry):
            owns_rows, cp = last_block_copy(e)
            pl.when(owns_rows)(cp.wait)
            return carry

        lax.fori_loop(0, N_EXPERTS, fill, 0)
        lax.fori_loop(0, N_EXPERTS, drain, 0)

        def spare_block_copy(b):
            row0 = pl.multiple_of(b * EXPERT_BLOCK, EXPERT_BLOCK)
            return pltpu.make_async_copy(zbuf, xs_ref.at[pl.ds(row0, EXPERT_BLOCK), :], zsem)

        n_used = eend_ref[0, N_EXPERTS - 1]
        n_blocks = xs_ref.shape[0] // EXPERT_BLOCK
        lax.fori_loop(n_used, n_blocks, lambda b, carry: (spare_block_copy(b).start(), carry)[1], 0)
        lax.fori_loop(n_used, n_blocks, lambda b, carry: (spare_block_copy(b).wait(), carry)[1], 0)

    def start(g, carry):
        for u in range(ROW_UNROLL):
            r = g * ROW_UNROLL + u
            _row_copy(h_ref, r, xs_ref, dest_ref[0, 0, r], sem).start(priority=u % 2)
            _row_copy(h_ref, r, xs_ref, dest_ref[0, 1, r], sem).start(priority=(u + 1) % 2)
        return carry

    lax.fori_loop(0, tm // ROW_UNROLL, start, 0)
    all_rows = pltpu.make_async_copy(h_ref, xs_ref.at[pl.ds(0, tm), :], sem)
    all_rows.wait()
    all_rows.wait()


def _dispatch(dest, eend, h2, n_rows):
    nt, _, tm = dest.shape
    n, d = h2.shape
    return pl.pallas_call(
        _dispatch_kernel,
        grid=(nt,),
        in_specs=[pl.BlockSpec((1, 2, tm), lambda i: (i, 0, 0), memory_space=pltpu.SMEM),
                  pl.BlockSpec(eend.shape, lambda i: (0, 0), memory_space=pltpu.SMEM),
                  pl.BlockSpec((tm, d), lambda i: (i, 0))],
        out_specs=pl.BlockSpec(memory_space=pl.ANY),
        out_shape=jax.ShapeDtypeStruct((n_rows, d), F32),
        scratch_shapes=[pltpu.VMEM((EXPERT_BLOCK, d), F32),
                        pltpu.SemaphoreType.DMA(()), pltpu.SemaphoreType.DMA(())],
        compiler_params=_params(1),
        name="dispatch",
    )(dest, eend, h2)


def _expert_kernel(bexp_ref, eend_ref, xs_ref, wg_ref, wu_ref, wd_ref, ys_ref, wg_b, wu_b, wd_b):
    i = pl.program_id(0)
    used = i < eend_ref[N_EXPERTS - 1]
    new_expert = jnp.logical_or(i == 0, bexp_ref[i] != bexp_ref[jnp.maximum(i - 1, 0)])

    @pl.when(jnp.logical_and(used, new_expert))
    def _():
        wg_b[...] = wg_ref[0].astype(BF16)
        wu_b[...] = wu_ref[0].astype(BF16)
        wd_b[...] = wd_ref[0].astype(BF16)

    @pl.when(used)
    def _():
        x = xs_ref[...].astype(BF16)
        hg = _dot(x, wg_b[...])
        hu = _dot(x, wu_b[...])
        hid = (hg * _sigmoid(hg) * hu).astype(BF16)
        ys_ref[...] = _dot(hid, wd_b[...])

    @pl.when(jnp.logical_not(used))
    def _():
        ys_ref[...] = jnp.zeros_like(ys_ref)


def _experts(bexp, eend, xs, e_gate, e_up, e_down, n_blocks):
    n_rows, d = xs.shape
    de = e_gate.shape[2]
    blk = EXPERT_BLOCK
    last = lambda i, nu: jnp.minimum(i, nu[N_EXPERTS - 1] - 1)
    rows = lambda i, be, nu: (last(i, nu), 0)
    wsel = lambda i, be, nu: (be[last(i, nu)], 0, 0)
    return pl.pallas_call(
        _expert_kernel,
        grid_spec=pltpu.PrefetchScalarGridSpec(
            num_scalar_prefetch=2,
            grid=(n_blocks,),
            in_specs=[pl.BlockSpec((blk, d), rows),
                      pl.BlockSpec((1, d, de), wsel),
                      pl.BlockSpec((1, d, de), wsel),
                      pl.BlockSpec((1, de, d), wsel)],
            out_specs=pl.BlockSpec((blk, d), lambda i, be, nu: (i, 0)),
            scratch_shapes=[pltpu.VMEM((d, de), BF16), pltpu.VMEM((d, de), BF16),
                            pltpu.VMEM((de, d), BF16)]),
        out_shape=jax.ShapeDtypeStruct((n_rows, d), F32),
        compiler_params=_params(1),
        name="experts",
    )(bexp, eend, xs, e_gate, e_up, e_down)


def _combine_kernel(dest_ref, dnext_ref, ys_ref, wts_ref, x1_ref, g2_ref, gf_ref, scf_ref, shf_ref, o_ref,
                    buf, sems):
    tm = x1_ref.shape[0]
    i = pl.program_id(0)
    cur = lax.rem(i, 2)

    def gather(d_ref, s):
        def start(g, carry):
            for u in range(ROW_UNROLL):
                r = g * ROW_UNROLL + u
                _row_copy(ys_ref, d_ref[0, 0, r], buf.at[s, 0], r, sems.at[s]).start(priority=u % 2)
                _row_copy(ys_ref, d_ref[0, 1, r], buf.at[s, 1], r, sems.at[s]).start(priority=(u + 1) % 2)
            return carry

        lax.fori_loop(0, tm // ROW_UNROLL, start, 0)

    @pl.when(i == 0)
    def _():
        gather(dest_ref, 0)

    @pl.when(i + 1 < pl.num_programs(0))
    def _():
        gather(dnext_ref, 1 - cur)

    all_rows = pltpu.make_async_copy(ys_ref.at[pl.ds(0, tm), :], buf.at[cur, 0], sems.at[cur])
    all_rows.wait()
    all_rows.wait()
    wts = wts_ref[...]
    moe = buf[cur, 0] * wts[:, 0:1] + buf[cur, 1] * wts[:, 1:2]
    x2 = x1_ref[...] + (1.0 + g2_ref[0]) * moe
    ms = jnp.mean(x2 * x2, axis=-1, keepdims=True)
    y = x2 * lax.rsqrt(ms + NORM_EPS) * gf_ref[...]
    o_ref[...] = y * (1.0 + scf_ref[0]) + shf_ref[0]


def _combine(dest, ys, wts_col, x1, mod, modf, gf, seq):
    nt, _, tm = dest.shape
    n, d = x1.shape
    tps = seq // tm
    return pl.pallas_call(
        _combine_kernel,
        grid=(nt,),
        in_specs=[pl.BlockSpec((1, 2, tm), lambda i: (i, 0, 0), memory_space=pltpu.SMEM),
                  pl.BlockSpec((1, 2, tm), lambda i: (jnp.minimum(i + 1, nt - 1), 0, 0),
                               memory_space=pltpu.SMEM),
                  pl.BlockSpec(memory_space=pl.ANY),
                  pl.BlockSpec((tm, 2), lambda i: (i, 0)),
                  pl.BlockSpec((tm, d), lambda i: (i, 0)),
                  pl.BlockSpec((1, 1, d), lambda i: ((i // tps) * 6 + 5, 0, 0)),
                  pl.BlockSpec((1, d), lambda i: (0, 0)),
                  pl.BlockSpec((1, 1, d), lambda i: ((i // tps) * 2 + 1, 0, 0)),
                  pl.BlockSpec((1, 1, d), lambda i: ((i // tps) * 2 + 0, 0, 0))],
        out_specs=pl.BlockSpec((tm, d), lambda i: (i, 0)),
        out_shape=jax.ShapeDtypeStruct((n, d), F32),
        scratch_shapes=[pltpu.VMEM((2, 2, tm, d), F32), pltpu.SemaphoreType.DMA((2,))],
        compiler_params=_params(1),
        name="combine",
    )(dest, dest, ys, wts_col, x1, mod, gf, modf, modf)


def kernel(x, c, ada_w, ada_b, norm1_g, w_in, shift_mu, w0, w_decay_up, a0, w_aaa_up, w_gate_up,
           k_k, k_a, r_k, gn_g, gn_b, kv_norm_g, k_idx_norm_g, w_uk, w_uv, w_out, norm2_g,
           w_group, b_group, w_expert, b_expert, e_gate, e_up, e_down,
           final_ada_w, final_ada_b, final_norm_g):
    bsz, seq, d = x.shape
    n = bsz * seq
    depth = ada_w.shape[0]
    tm = min(TOKEN_TILE, seq)
    x2 = x.reshape(n, d)

    modf = _ada(c, final_ada_w, final_ada_b).reshape(bsz * 2, 1, d)
    for l in range(depth):
        mod = _ada(c, ada_w[l], ada_b[l]).reshape(bsz * 6, 1, d)

        w1 = w_in[l][:, :RWKV_COLS].astype(BF16)
        w2 = jnp.pad(w_in[l][:, RWKV_COLS:], ((0, 0), (0, DSA_COLS_PAD - DSA_COLS))).astype(BF16)
        p_rwkv, p_dsa = _proj(x2, norm1_g[l].reshape(1, d), mod, w1, w2, seq, min(PROJ_TILE, seq))

        zeros = jnp.zeros((DECAY_LORA, RWKV_WIDTH), F32)
        wda = jnp.concatenate([jnp.concatenate([w_decay_up[l], zeros], axis=1),
                               jnp.concatenate([zeros, w_aaa_up[l]], axis=1)], axis=0).astype(BF16)
        y_rwkv = _rwkv(p_rwkv, bsz, seq, shift_mu[l], w0[l], wda, a0[l], w_gate_up[l].astype(BF16),
                       k_k[l], k_a[l], r_k[l], gn_g[l], gn_b[l])
        y_dsa = _dsa(p_dsa, bsz, seq, kv_norm_g[l], k_idx_norm_g[l], w_uk[l], w_uv[l])

        wr = jnp.zeros((ROUTER_ROWS, d), F32)
        wr = wr.at[0:N_GROUPS].set(w_group[l].T).at[8:8 + N_EXPERTS].set(w_expert[l].T)
        br = jnp.zeros((ROUTER_ROWS, 1), F32)
        br = br.at[0:N_GROUPS, 0].set(b_group[l]).at[8:8 + N_EXPERTS, 0].set(b_expert[l])
        wo = w_out[l].astype(BF16)
        x1, h2, ids, wts = _mix(y_rwkv, y_dsa, x2, wo[:RWKV_WIDTH], wo[RWKV_WIDTH:], mod,
                                norm2_g[l].reshape(1, d), wr, br, seq, tm)

        n_blocks = (n * 2) // EXPERT_BLOCK + N_EXPERTS
        n_blocks_pad = -(-n_blocks // 128) * 128
        dest, bexp, eend = _sort(ids, n_blocks_pad)
        xs = _dispatch(dest, eend, h2, n_blocks * EXPERT_BLOCK)
        ys = _experts(bexp.reshape(-1), eend.reshape(-1), xs, e_gate[l], e_up[l], e_down[l], n_blocks)
        wts_col = jnp.transpose(wts, (0, 2, 1)).reshape(n, 2)
        last = l == depth - 1
        if not last:
            raise NotImplementedError("stacked layers need a residual-only combine")
        x2 = _combine(dest, ys, wts_col, x1, mod, modf, final_norm_g.reshape(1, d), seq)
    return x2.reshape(bsz, seq, d)
```

```python
import functools

import numpy as np
import jax
import jax.numpy as jnp
from jax import lax
from jax.experimental import pallas as pl
from jax.experimental.pallas import tpu as pltpu

F32 = jnp.float32
BF16 = jnp.bfloat16
I32 = jnp.int32

HEAD_DIM = 64
RWKV_WIDTH = 512
RWKV_HEADS = RWKV_WIDTH // HEAD_DIM
GROUP_HEADS = 2
DSA_WIDTH = 512
DSA_HEADS = DSA_WIDTH // HEAD_DIM
DECAY_LORA = 64
AAA_LORA = 64
GATE_LORA = 128
KV_LATENT = 128
IDX_HEADS = 4
IDX_DIM = 64
RWKV_COLS = 3 * RWKV_WIDTH + DECAY_LORA + AAA_LORA + GATE_LORA
DSA_COLS = DSA_WIDTH + KV_LATENT + IDX_HEADS * IDX_DIM + IDX_DIM + IDX_HEADS
DSA_COLS_PAD = 1024
TOPK_MAX = 256
ATTN_CHUNK_LOG2 = 6
Q_BLOCK = 256
RWKV_CHUNK = 64
RWKV_STEP_CHUNKS = 4
N_GROUPS = 4
EXPERTS_PER_GROUP = 8
N_EXPERTS = N_GROUPS * EXPERTS_PER_GROUP
ROUTER_ROWS = 40
NORM_EPS = 1e-6
GN_EPS = HEAD_DIM * 1e-5
NEG_BIG = -1e30
INT_MIN = -2 ** 31

VMEM_LIMIT_BYTES = 56 * 1024 * 1024
EXPERT_BLOCK = 512
PROJ_TILE = 512
TOKEN_TILE = 1024

_NN = (((1,), (0,)), ((), ()))
_NT = (((1,), (1,)), ((), ()))
_TN = (((0,), (0,)), ((), ()))


def _dot(a, b, dims=_NN):
    return lax.dot_general(a, b, dims, preferred_element_type=F32)


def _split(x):
    hi = x.astype(BF16)
    lo = (x - hi.astype(F32)).astype(BF16)
    return hi, lo


def _dot3(a, b, dims=_NN):
    ah, al = _split(a)
    bh, bl = _split(b)
    return _dot(ah, bh, dims) + _dot(ah, bl, dims) + _dot(al, bh, dims)


def _dot_exact_rhs(a, b_bf16, dims=_NN):
    ah, al = _split(a)
    return _dot(ah, b_bf16, dims) + _dot(al, b_bf16, dims)


def _sigmoid(x):
    return 1.0 / (1.0 + jnp.exp(-x))


def _softplus(x):
    return jnp.maximum(x, 0.0) + jnp.log(1.0 + jnp.exp(-jnp.abs(x)))


def _params(n_axes):
    return pltpu.CompilerParams(dimension_semantics=("arbitrary",) * n_axes,
                                vmem_limit_bytes=VMEM_LIMIT_BYTES)


def _ada_kernel(c_ref, w_ref, b_ref, o_ref):
    c = c_ref[...]
    o_ref[...] = _dot3(c * _sigmoid(c), w_ref[0]) + b_ref[0]


def _ada(c, w, b, layer):
    bsz, d = c.shape
    n = w.shape[2]
    tn = 1024
    return pl.pallas_call(
        _ada_kernel,
        grid=(n // tn,),
        in_specs=[pl.BlockSpec((bsz, d), lambda j: (0, 0)),
                  pl.BlockSpec((1, d, tn), lambda j: (layer, 0, j)),
                  pl.BlockSpec((1, 1, tn), lambda j: (layer, 0, j))],
        out_specs=pl.BlockSpec((bsz, tn), lambda j: (0, j)),
        out_shape=jax.ShapeDtypeStruct((bsz, n), F32),
        compiler_params=_params(1),
        name="ada",
    )(c, w, b.reshape(b.shape[0], 1, n))


def _proj_kernel(x_ref, g_ref, sc_ref, sh_ref, w1_ref, w2_ref, o1_ref, o2_ref):
    x = x_ref[...]
    ms = jnp.mean(x * x, axis=-1, keepdims=True)
    y = x * lax.rsqrt(ms + NORM_EPS) * g_ref[...]
    h = (y * (1.0 + sc_ref[0]) + sh_ref[0]).astype(BF16)
    o1_ref[...] = _dot(h, w1_ref[...])
    o2_ref[...] = _dot(h, w2_ref[...])


def _proj(x2, g, mod, w1, w2, seq, tm):
    n, d = x2.shape
    tiles_per_seq = seq // tm
    return pl.pallas_call(
        _proj_kernel,
        grid=(n // tm,),
        in_specs=[pl.BlockSpec((tm, d), lambda i: (i, 0)),
                  pl.BlockSpec((1, d), lambda i: (0, 0)),
                  pl.BlockSpec((1, 1, d), lambda i: ((i // tiles_per_seq) * 6 + 1, 0, 0)),
                  pl.BlockSpec((1, 1, d), lambda i: ((i // tiles_per_seq) * 6 + 0, 0, 0)),
                  pl.BlockSpec(w1.shape, lambda i: (0, 0)),
                  pl.BlockSpec(w2.shape, lambda i: (0, 0))],
        out_specs=[pl.BlockSpec((tm, w1.shape[1]), lambda i: (i, 0)),
                   pl.BlockSpec((tm, w2.shape[1]), lambda i: (i, 0))],
        out_shape=[jax.ShapeDtypeStruct((n, w1.shape[1]), F32),
                   jax.ShapeDtypeStruct((n, w2.shape[1]), F32)],
        compiler_params=_params(1),
        name="proj",
    )(x2, g, mod, mod, w1, w2)


def _rwkv_kernel(p_ref, mu_ref, w0_ref, wda_ref, a0_ref, wg_ref, kk_ref, ka_ref, rk_ref,
                 gng_ref, gnb_ref, blk_ref, tril_ref, masks_ref, o_ref, s_ref, prev_ref):
    c = RWKV_CHUNK
    w = RWKV_WIDTH
    tt = p_ref.shape[0]
    chunks = range(tt // c)

    @pl.when(pl.program_id(1) == 0)
    def _():
        s_ref[...] = jnp.zeros_like(s_ref)
        prev_ref[...] = jnp.zeros_like(prev_ref)

    p = p_ref[...]
    row = lax.broadcasted_iota(I32, p.shape, 0)
    p_prev = jnp.where(row == 0, prev_ref[...], pltpu.roll(p, 1, 0))
    prev_ref[...] = p[tt - 1:tt, :]
    ps = p + mu_ref[...] * (p_prev - p)

    r = ps[:, 0:w]
    k = ps[:, w:2 * w]
    v = ps[:, 2 * w:3 * w]
    lora_in = ps[:, 3 * w:3 * w + DECAY_LORA + AAA_LORA]
    gate_in = ps[:, 3 * w + DECAY_LORA + AAA_LORA:]

    lane = lax.broadcasted_iota(I32, lora_in.shape, 1)
    lora_act = jnp.where(lane < DECAY_LORA, jnp.tanh(lora_in), lora_in).astype(BF16)
    da = _dot(lora_act, wda_ref[...])
    log_w = -_softplus(-(w0_ref[...] + da[:, :w])) - 0.5
    ld = -jnp.exp(log_w)
    a = _sigmoid(a0_ref[...] + da[:, w:])
    gate = _dot(_sigmoid(gate_in).astype(BF16), wg_ref[...])

    blk = blk_ref[...]

    def head_sums(z):
        nb = w // blk.shape[0]
        rows = jnp.concatenate([z[:, i * 128:(i + 1) * 128] for i in range(nb)], axis=0)
        sums = _dot(rows.astype(BF16), blk)
        return jnp.concatenate([sums[i * tt:(i + 1) * tt] for i in range(nb)], axis=1)

    kk = k * kk_ref[...]
    kk = kk / jnp.maximum(jnp.sqrt(head_sums(kk * kk)), 1e-12)
    k2 = k * (1.0 + (a - 1.0) * ka_ref[...])

    cs = _dot_exact_rhs_lhs(tril_ref[...], ld)
    cs_last = [cs[(ch + 1) * c - 1:(ch + 1) * c, :] for ch in chunks]
    cs_end = jnp.concatenate([jnp.broadcast_to(cl, (c, w)) for cl in cs_last], axis=0)
    e_neg = jnp.exp(-cs)
    e_rem = jnp.exp(cs_end - cs)
    kka = kk * a
    a_t = (-kk) * jnp.exp(cs - ld)
    b_t = kka * e_neg
    k_t = k2 * e_neg
    r_t = r * jnp.exp(cs)
    b_h = kka * e_rem
    k_h = k2 * e_rem
    w_c = [jnp.exp(cl) for cl in cs_last]

    bd = masks_ref[0]
    strict = masks_ref[1]
    incl = masks_ref[2]
    eye = masks_ref[3]
    gw = GROUP_HEADS * HEAD_DIM
    stack = lambda z: jnp.concatenate([z] * GROUP_HEADS, axis=0)

    groups = range(RWKV_HEADS // GROUP_HEADS)
    units = [(ch, g) for ch in chunks for g in groups]
    piece = lambda z, u: z[u[0] * c:(u[0] + 1) * c, u[1] * gw:(u[1] + 1) * gw]
    v_x = [(stack(piece(v, u)) * bd).astype(BF16) for u in units]
    ar_x = [jnp.concatenate([(stack(piece(a_t, u)) * bd).astype(BF16),
                             (stack(piece(r_t, u)) * bd).astype(BF16)], axis=0) for u in units]
    bk_r = [jnp.concatenate([stack(piece(b_t, u).astype(BF16)), stack(piece(k_t, u).astype(BF16))], axis=0)
            for u in units]
    bk_h = [jnp.concatenate([stack(piece(b_h, u).astype(BF16)), stack(piece(k_h, u).astype(BF16))], axis=0)
            for u in units]
    quad = [_dot(x, y, _NT) for x, y in zip(ar_x, bk_r)]
    l_ab = [q4[:gw, :gw] * strict for q4 in quad]
    l_ak = [(q4[:gw, gw:] * strict).astype(BF16) for q4 in quad]
    m_cat = [jnp.concatenate([(q4[gw:, :gw] * incl).astype(BF16), (q4[gw:, gw:] * incl).astype(BF16)], axis=1)
             for q4 in quad]

    pw = [p.astype(BF16) for p in l_ab]
    t_inv = [eye + l for l in l_ab]
    pw = [_dot(p, p) for p in pw]
    for _ in range(4):
        pb = [p.astype(BF16) for p in pw]
        both = [_dot(jnp.concatenate([t.astype(BF16), p], axis=0), p) for t, p in zip(t_inv, pb)]
        t_inv = [t + tp[:gw] for t, tp in zip(t_inv, both)]
        pw = [tp[gw:] for tp in both]
    t_inv = [(t + _dot(t.astype(BF16), p.astype(BF16))).astype(BF16) for t, p in zip(t_inv, pw)]

    state = [s_ref[g] for g in groups]
    y_rows = []
    for ch in chunks:
        us = [ch * len(groups) + g for g in groups]
        ar_s = [_dot(ar_x[u], state[g].astype(BF16), _NT) for g, u in zip(groups, us)]
        rhs = [ar_s[g][:gw] + _dot(l_ak[u], v_x[u]) for g, u in zip(groups, us)]
        ub = [_dot(t_inv[u], rhs[g].astype(BF16)).astype(BF16) for g, u in zip(groups, us)]
        uv = [jnp.concatenate([ub[g], v_x[u]], axis=0) for g, u in zip(groups, us)]
        y_x = [ar_s[g][gw:] + _dot(m_cat[u], uv[g]) for g, u in zip(groups, us)]
        y_rows.append(jnp.concatenate(
            [sum(yx[hh * c:(hh + 1) * c] for hh in range(1, GROUP_HEADS)) + yx[0:c] for yx in y_x], axis=1))
        state = [(state[g] * w_c[ch][:, g * gw:(g + 1) * gw] + _dot(uv[g], bk_h[u], _TN)) * bd
                 for g, u in zip(groups, us)]
    for g in groups:
        s_ref[g] = state[g]

    y = jnp.concatenate(y_rows, axis=0)
    inv_n = 1.0 / HEAD_DIM
    mean = head_sums(y) * inv_n
    dlt = y - mean
    var = head_sums(dlt * dlt) * inv_n
    yn = dlt * lax.rsqrt(var + GN_EPS) * gng_ref[...] + gnb_ref[...]
    bonus = head_sums(r * k2 * rk_ref[...]) * v
    o_ref[...] = ((yn + bonus) * gate).astype(o_ref.dtype)


def _dot_exact_rhs_lhs(a_bf16, b):
    bh, bl = _split(b)
    return _dot(a_bf16, bh) + _dot(a_bf16, bl)


def _rwkv(p, bsz, seq, mu, w0, wda, a0, wg, k_k, k_a, r_k, gn_g, gn_b):
    c = RWKV_CHUNK
    w = RWKV_WIDTH
    tt = c * RWKV_STEP_CHUNKS if seq % (c * RWKV_STEP_CHUNKS) == 0 else c
    n_steps = seq // tt
    head_of = np.arange(128) // HEAD_DIM
    blk = jnp.asarray(head_of[:, None] == head_of[None, :], BF16)
    ti = np.arange(tt)
    tril = jnp.asarray((ti[:, None] >= ti[None, :]) & (ti[:, None] // c == ti[None, :] // c), BF16)
    assert c == HEAD_DIM and RWKV_HEADS % GROUP_HEADS == 0
    gi = np.arange(GROUP_HEADS * c)
    same = (gi[:, None] // c) == (gi[None, :] // c)
    later = (gi[:, None] % c) > (gi[None, :] % c)
    masks = jnp.asarray(np.stack([same, same & later, same & (later | (gi[:, None] % c == gi[None, :] % c)),
                                  gi[:, None] == gi[None, :]]), F32)
    row = lambda a: a.reshape(1, -1)
    vec = lambda n: pl.BlockSpec((1, n), lambda b, t: (0, 0))
    full = lambda a: pl.BlockSpec(a.shape, lambda b, t: (0, 0))
    return pl.pallas_call(
        _rwkv_kernel,
        grid=(bsz, n_steps),
        in_specs=[pl.BlockSpec((tt, RWKV_COLS), lambda b, t: (b * n_steps + t, 0)),
                  vec(RWKV_COLS), vec(w), full(wda), vec(w), full(wg), vec(w), vec(w), vec(w),
                  vec(w), vec(w), full(blk), full(tril),
                  pl.BlockSpec(masks.shape, lambda b, t: (0, 0, 0))],
        out_specs=pl.BlockSpec((tt, w), lambda b, t: (b * n_steps + t, 0)),
        out_shape=jax.ShapeDtypeStruct((bsz * seq, w), BF16),
        scratch_shapes=[pltpu.VMEM((RWKV_HEADS // GROUP_HEADS, GROUP_HEADS * HEAD_DIM, GROUP_HEADS * HEAD_DIM), F32),
                        pltpu.VMEM((1, RWKV_COLS), F32)],
        compiler_params=_params(2),
        name="rwkv",
    )(p, row(mu), row(w0), wda, row(a0), wg, row(k_k), row(k_a), row(r_k), row(gn_g), row(gn_b),
      blk, tril, masks)


SLAB_ROWS = 64
LOW_BITS = 12
WALK_STEPS = 3
ATTN_HEAD_GROUP = 2
ONES_ROWS = 16


def _fold_rows(x, op):
    acc = x[0:SLAB_ROWS]
    for r0 in range(SLAB_ROWS, x.shape[0], SLAB_ROWS):
        acc = op(acc, x[r0:r0 + SLAB_ROWS])
    return acc


def _split3(x):
    hi = x.astype(BF16)
    rest = x - hi.astype(F32)
    mid = rest.astype(BF16)
    return hi, mid, (rest - mid.astype(F32)).astype(BF16)


def _dsa_kernel(pd_ref, kvg_ref, kig_ref, wuk_ref, wuv_ref, tril_ref, wsel_ref, o_ref,
                ckv_s, ckvt_s, kcat_s, score_s, *, nq, n_cls, topk):
    qb = Q_BLOCK
    j = pl.program_id(1)

    @pl.when(j == 0)
    def _():
        ckv_s[...] = jnp.zeros_like(ckv_s)
        ckvt_s[...] = jnp.zeros_like(ckvt_s)
        kcat_s[...] = jnp.zeros_like(kcat_s)

    pd = pd_ref[...]
    q = pd[:, :DSA_WIDTH]
    ckv = pd[:, DSA_WIDTH:DSA_WIDTH + KV_LATENT]
    qi = pd[:, DSA_WIDTH + KV_LATENT:DSA_WIDTH + KV_LATENT + IDX_HEADS * IDX_DIM]
    tail = pd[:, DSA_WIDTH + KV_LATENT + IDX_HEADS * IDX_DIM:]
    ki = tail[:, :IDX_DIM]

    ckv_n = ckv * lax.rsqrt(jnp.mean(ckv * ckv, axis=-1, keepdims=True) + NORM_EPS) * kvg_ref[...]
    ki_n = ki * lax.rsqrt(jnp.mean(ki * ki, axis=-1, keepdims=True) + NORM_EPS) * kig_ref[...]
    ki_hi, ki_lo = _split(ki_n)
    rows = pl.ds(pl.multiple_of(j * qb, qb), qb)
    ckv_s[rows, :] = ckv_n.astype(BF16)
    one_row = jnp.where(lax.broadcasted_iota(I32, (ONES_ROWS, qb), 0) == 0, 1.0, 0.0)
    ckvt_s[j] = jnp.concatenate([ckv_n.T, one_row], axis=0).astype(BF16)
    kcat_s[rows, :] = jnp.concatenate([ki_hi, ki_hi, ki_lo, jnp.zeros_like(ki_hi)], axis=1)

    w_t = sum(_dot(wsel_ref[...], part, _NT) for part in _split3(tail))

    per = nq // n_cls
    for cls in range(n_cls):
        @pl.when((j >= cls * per) & (j < (cls + 1) * per))
        def _(sk=(cls + 1) * per * qb):
            _dsa_block(j, q, qi, w_t, wuk_ref, wuv_ref, tril_ref, o_ref, ckv_s, ckvt_s, kcat_s,
                       score_s, sk=sk, topk=topk)


def _dsa_block(j, q, qi, w_t, wuk_ref, wuv_ref, tril_ref, o_ref, ckv_s, ckvt_s, kcat_s, score_s,
               *, sk, topk):
    qb = Q_BLOCK
    kcat = kcat_s[0:sk, :]
    w_s = w_t * (IDX_HEADS ** -0.5)
    q_cat = []
    for hh in range(IDX_HEADS):
        q_hi, q_lo = _split(qi[:, hh * IDX_DIM:(hh + 1) * IDX_DIM] * (IDX_DIM ** -0.5))
        q_cat.append(jnp.concatenate([q_hi, q_lo, q_hi, jnp.zeros_like(q_hi)], axis=1))
    dots = [_dot(kcat, qc, _NT) for qc in q_cat]
    terms = [w_s[hh:hh + 1, :] * jnp.maximum(dots[hh], 0.0) for hh in range(IDX_HEADS)]
    score = sum(terms[1:], terms[0])

    key_pos = lax.broadcasted_iota(I32, (sk, qb), 0)
    q_pos = j * qb + lax.broadcasted_iota(I32, (sk, qb), 1)
    adm = jnp.right_shift(key_pos, ATTN_CHUNK_LOG2) <= jnp.right_shift(q_pos, ATTN_CHUNK_LOG2)

    score_s[0:sk, :] = jnp.where(adm, score, NEG_BIG)

    def as_float(okey):
        return lax.bitcast_convert_type(okey ^ ((okey >> 31) & 0x7FFFFFFF), F32)

    def count_ge(cand):
        acc = jnp.zeros((SLAB_ROWS, qb), F32)
        for r0 in range(0, sk, SLAB_ROWS):
            acc = acc + jnp.where(score_s[r0:r0 + SLAB_ROWS, :] >= cand, 1.0, 0.0)
        return jnp.sum(acc, axis=0, keepdims=True)

    def descend(i, tu):
        cand_u = tu | jnp.left_shift(jnp.int32(1), 31 - i)
        return jnp.where(count_ge(as_float(cand_u ^ INT_MIN)) >= topk, cand_u, tu)

    few_keys = (j + 1) * qb <= topk
    high = 32 - LOW_BITS
    tu = lax.fori_loop(0, jnp.where(few_keys, 0, high), descend, jnp.zeros((1, qb), I32))

    bound = as_float((tu + (1 << LOW_BITS)) ^ INT_MIN)
    guess = jnp.full((1, qb), -jnp.inf, F32)
    settled = jnp.zeros((1, qb), jnp.bool_)
    for _ in range(WALK_STEPS):
        best = jnp.full((SLAB_ROWS, qb), -jnp.inf, F32)
        for r0 in range(0, sk, SLAB_ROWS):
            slab = score_s[r0:r0 + SLAB_ROWS, :]
            best = jnp.maximum(best, jnp.where(slab < bound, slab, -jnp.inf))
        best = jnp.max(best, axis=0, keepdims=True)
        reached = (count_ge(best) >= topk) & (best > -jnp.inf)
        guess = jnp.where(reached & ~settled, best, guess)
        settled = settled | reached
        bound = jnp.where(settled, bound, best)
    missed = jnp.max(jnp.where(settled, 0, 1))
    tu = lax.fori_loop(high, jnp.where(few_keys, high, high + LOW_BITS * missed), descend, tu)
    thr = jnp.where(few_keys, NEG_BIG, jnp.where(missed == 1, as_float(tu ^ INT_MIN), guess))

    score = score_s[0:sk, :]
    gt = score > thr
    eq = score == thr
    need = topk - jnp.sum(_fold_rows(jnp.where(gt, 1.0, 0.0), jnp.add), axis=0, keepdims=True)
    eq_b = jnp.where(eq, 1.0, 0.0).astype(BF16)
    tb = tril_ref.shape[0]
    off = jnp.zeros((1, qb), F32)
    pieces = []
    for kb in range(sk // tb):
        pre = _dot(tril_ref[...], eq_b[kb * tb:(kb + 1) * tb, :])
        pieces.append(pre + off)
        off = off + pre[tb - 1:tb, :]
    prefix = jnp.concatenate(pieces, axis=0)
    tie_bias = jnp.where(prefix <= need, 0.0, NEG_BIG)
    bias = jnp.where(gt, 0.0, jnp.where(eq, tie_bias, NEG_BIG))
    bias = jnp.where(adm, bias, NEG_BIG)

    ckv_all = ckv_s[0:sk, :]
    ckv_t = jnp.concatenate([ckvt_s[b] for b in range(sk // qb)], axis=1)
    q_lat = (_dot(q.astype(BF16), wuk_ref[...]) * (HEAD_DIM ** -0.5)).astype(BF16)
    head_logits = lambda h: _dot(ckv_all, q_lat[:, h * KV_LATENT:(h + 1) * KV_LATENT], _NT)
    outs = []
    groups = [range(h0, h0 + ATTN_HEAD_GROUP) for h0 in range(0, DSA_HEADS, ATTN_HEAD_GROUP)]
    nxt = [head_logits(h) + bias for h in groups[0]]
    for gi in range(len(groups)):
        logits = nxt
        if gi + 1 < len(groups):
            nxt = [head_logits(h) + bias for h in groups[gi + 1]]
        mx = [jnp.max(_fold_rows(lg, jnp.maximum), axis=0, keepdims=True) for lg in logits]
        pr = [jnp.exp((lg - m).astype(BF16)) for lg, m in zip(logits, mx)]
        acc = [_dot(ckv_t, p) for p in pr]
        outs.extend((a[:KV_LATENT] / a[KV_LATENT:KV_LATENT + 1]).astype(BF16) for a in acc)
    o_lat_t = jnp.concatenate(outs, axis=0)
    o_ref[...] = _dot(o_lat_t, wuv_ref[...], _TN).astype(o_ref.dtype)


def _dsa(pd, bsz, seq, kv_g, ki_g, w_uk, w_uv):
    qb = Q_BLOCK
    nq = seq // qb
    topk = min(TOPK_MAX, seq // 4)
    hc = DSA_HEADS * KV_LATENT
    head_of_q = np.arange(DSA_WIDTH) // HEAD_DIM
    head_of_l = np.arange(hc) // KV_LATENT
    mask = jnp.asarray(head_of_q[:, None] == head_of_l[None, :], F32)
    uk = jnp.transpose(w_uk, (1, 2, 0)).reshape(DSA_WIDTH, KV_LATENT)
    wuk_bd = (jnp.tile(uk, (1, DSA_HEADS)) * mask).astype(BF16)
    uv = w_uv.reshape(KV_LATENT, DSA_WIDTH)
    wuv_bd = (jnp.tile(uv, (DSA_HEADS, 1)) * mask.T).astype(BF16)
    n_cls = max(d for d in (4, 2, 1) if nq % d == 0)
    tb = 256 if ((nq // n_cls) * qb) % 256 == 0 else 128
    tril = jnp.asarray(np.tril(np.ones((tb, tb), np.float32)), BF16)
    wsel = jnp.asarray(np.arange(128)[None, :] == (IDX_DIM + np.arange(8))[:, None], BF16)
    wsel = wsel * jnp.asarray(np.arange(8)[:, None] < IDX_HEADS, BF16)
    full = lambda a: pl.BlockSpec(a.shape, lambda b, j: (0, 0))
    return pl.pallas_call(
        functools.partial(_dsa_kernel, nq=nq, n_cls=n_cls, topk=topk),
        grid=(bsz, nq),
        in_specs=[pl.BlockSpec((qb, DSA_COLS_PAD), lambda b, j: (b * nq + j, 0)),
                  pl.BlockSpec((1, KV_LATENT), lambda b, j: (0, 0)),
                  pl.BlockSpec((1, IDX_DIM), lambda b, j: (0, 0)),
                  full(wuk_bd), full(wuv_bd), full(tril), full(wsel)],
        out_specs=pl.BlockSpec((qb, DSA_WIDTH), lambda b, j: (b * nq + j, 0)),
        out_shape=jax.ShapeDtypeStruct((bsz * seq, DSA_WIDTH), BF16),
        scratch_shapes=[pltpu.VMEM((seq, KV_LATENT), BF16),
                        pltpu.VMEM((nq, KV_LATENT + ONES_ROWS, qb), BF16),
                        pltpu.VMEM((seq, 4 * IDX_DIM), BF16),
                        pltpu.VMEM((seq, qb), F32)],
        compiler_params=_params(2),
        name="dsa",
    )(pd, kv_g.reshape(1, -1), ki_g.reshape(1, -1), wuk_bd, wuv_bd, tril, wsel)


def _mix_kernel(yr_ref, yd_ref, x_ref, wo1_ref, wo2_ref, g1_ref, n2_ref, sc_ref, sh_ref,
                wr_ref, br_ref, x1_ref, h2_ref, ids_ref, wts_ref):
    mixed = _dot(yr_ref[...], wo1_ref[...]) + _dot(yd_ref[...], wo2_ref[...])
    x1 = x_ref[...] + (1.0 + g1_ref[0]) * mixed
    x1_ref[...] = x1
    ms = jnp.mean(x1 * x1, axis=-1, keepdims=True)
    h2 = x1 * lax.rsqrt(ms + NORM_EPS) * n2_ref[...] * (1.0 + sc_ref[0]) + sh_ref[0]
    h2_ref[...] = h2

    lt = _dot3(wr_ref[...], h2, _NT) + br_ref[...]
    gl = lt[0:N_GROUPS]
    el = lt[8:8 + N_EXPERTS]
    tm = gl.shape[1]
    gmax = jnp.max(gl, axis=0, keepdims=True)
    gidx = lax.broadcasted_iota(I32, (N_GROUPS, tm), 0)
    gsel = jnp.min(jnp.where(gl == gmax, gidx, N_GROUPS), axis=0, keepdims=True)
    p_group = 1.0 / jnp.sum(jnp.exp(gl - gmax), axis=0, keepdims=True)
    eidx = lax.broadcasted_iota(I32, (N_EXPERTS, tm), 0)
    el = jnp.where(jnp.right_shift(eidx, 3) == gsel, el, NEG_BIG)
    m1 = jnp.max(el, axis=0, keepdims=True)
    i1 = jnp.min(jnp.where(el == m1, eidx, N_EXPERTS), axis=0, keepdims=True)
    el2 = jnp.where(eidx == i1, NEG_BIG, el)
    m2 = jnp.max(el2, axis=0, keepdims=True)
    i2 = jnp.min(jnp.where(el2 == m2, eidx, N_EXPERTS), axis=0, keepdims=True)
    e2 = jnp.exp(m2 - m1)
    w1 = p_group / (1.0 + e2)
    ids_ref[0] = jnp.concatenate([i1, i2], axis=0)
    wts_ref[0] = jnp.concatenate([w1, w1 * e2], axis=0)


def _mix(yr, yd, x2, wo1, wo2, mod, n2g, wr, br, seq, tm):
    n, d = x2.shape
    nt = n // tm
    tps = seq // tm
    modspec = lambda k: pl.BlockSpec((1, 1, d), lambda i: ((i // tps) * 6 + k, 0, 0))
    full = lambda a: pl.BlockSpec(a.shape, lambda i: (0,) * a.ndim)
    return pl.pallas_call(
        _mix_kernel,
        grid=(nt,),
        in_specs=[pl.BlockSpec((tm, yr.shape[1]), lambda i: (i, 0)),
                  pl.BlockSpec((tm, yd.shape[1]), lambda i: (i, 0)),
                  pl.BlockSpec((tm, d), lambda i: (i, 0)),
                  full(wo1), full(wo2), modspec(2), full(n2g), modspec(4), modspec(3),
                  full(wr), full(br)],
        out_specs=[pl.BlockSpec((tm, d), lambda i: (i, 0)),
                   pl.BlockSpec((tm, d), lambda i: (i, 0)),
                   pl.BlockSpec((1, 2, tm), lambda i: (i, 0, 0)),
                   pl.BlockSpec((1, 2, tm), lambda i: (i, 0, 0))],
        out_shape=[jax.ShapeDtypeStruct((n, d), F32),
                   jax.ShapeDtypeStruct((n, d), F32),
                   jax.ShapeDtypeStruct((nt, 2, tm), I32),
                   jax.ShapeDtypeStruct((nt, 2, tm), F32)],
        compiler_params=_params(1),
        name="mix",
    )(yr, yd, x2, wo1, wo2, mod, n2g, mod, mod, wr, br)


def _sort_kernel(ids_ref, triu_ref, tril_ref, dest_ref, bexp_ref, eend_ref,
                 cnt_ref, run_ref, start_ref, *, n_blocks_pad):
    phase = pl.program_id(0)
    i = pl.program_id(1)
    ids = ids_ref[0]
    tm = ids.shape[1]
    eidx = lax.broadcasted_iota(I32, (N_EXPERTS, tm), 0)
    hit0 = eidx == ids[0:1]
    hit1 = eidx == ids[1:2]
    onehot = jnp.where(hit0, 1.0, 0.0) + jnp.where(hit1, 1.0, 0.0)

    @pl.when((phase == 0) & (i == 0))
    def _():
        cnt_ref[...] = jnp.zeros_like(cnt_ref)

    @pl.when(phase == 0)
    def _():
        cnt_ref[...] += jnp.sum(onehot, axis=1, keepdims=True)

    @pl.when((phase == 1) & (i == 0))
    def _():
        run_ref[...] = jnp.zeros_like(run_ref)
        nblk = jnp.floor((cnt_ref[...] + (EXPERT_BLOCK - 1)) * (1.0 / EXPERT_BLOCK))
        nblk_b = jnp.broadcast_to(nblk, (N_EXPERTS, 128))
        first_blk = _dot_exact_rhs_lhs(tril_ref[...], nblk_b)
        start_ref[...] = first_blk[:, 0:1] * EXPERT_BLOCK
        end_blk = first_blk + nblk_b
        bidx = lax.broadcasted_iota(I32, (N_EXPERTS, n_blocks_pad), 1).astype(F32)
        owner = jnp.sum(jnp.where(end_blk[:, 0:1] <= bidx, 1.0, 0.0), axis=0, keepdims=True)
        bexp_ref[...] = jnp.minimum(owner, N_EXPERTS - 1).astype(I32)
        on_diag = (lax.broadcasted_iota(I32, (N_EXPERTS, 128), 0)
                   == lax.broadcasted_iota(I32, (N_EXPERTS, 128), 1))
        eend_ref[...] = jnp.sum(jnp.where(on_diag, end_blk, 0.0), axis=0, keepdims=True).astype(I32)

    @pl.when(phase == 1)
    def _():
        before = _dot(onehot.astype(BF16), triu_ref[...])
        pos = start_ref[...] + run_ref[...] + before
        d0 = jnp.sum(jnp.where(hit0, pos, 0.0), axis=0, keepdims=True)
        d1 = jnp.sum(jnp.where(hit1, pos, 0.0), axis=0, keepdims=True)
        dest_ref[0] = jnp.concatenate([d0, d1], axis=0).astype(I32)
        run_ref[...] += jnp.sum(onehot, axis=1, keepdims=True)


def _sort(ids, n_blocks_pad):
    nt, _, tm = ids.shape
    triu = jnp.asarray(np.triu(np.ones((tm, tm), np.float32), 1), BF16)
    tril = jnp.asarray(np.tril(np.ones((N_EXPERTS, N_EXPERTS), np.float32), -1), BF16)
    return pl.pallas_call(
        functools.partial(_sort_kernel, n_blocks_pad=n_blocks_pad),
        grid=(2, nt),
        in_specs=[pl.BlockSpec((1, 2, tm), lambda p, i: (i, 0, 0)),
                  pl.BlockSpec(triu.shape, lambda p, i: (0, 0)),
                  pl.BlockSpec(tril.shape, lambda p, i: (0, 0))],
        out_specs=[pl.BlockSpec((1, 2, tm), lambda p, i: (i * p, 0, 0)),
                   pl.BlockSpec((1, n_blocks_pad), lambda p, i: (0, 0)),
                   pl.BlockSpec((1, 128), lambda p, i: (0, 0))],
        out_shape=[jax.ShapeDtypeStruct((nt, 2, tm), I32),
                   jax.ShapeDtypeStruct((1, n_blocks_pad), I32),
                   jax.ShapeDtypeStruct((1, 128), I32)],
        scratch_shapes=[pltpu.VMEM((N_EXPERTS, 1), F32),
                        pltpu.VMEM((N_EXPERTS, 1), F32),
                        pltpu.VMEM((N_EXPERTS, 1), F32)],
        compiler_params=_params(2),
        name="sort",
    )(ids, triu, tril)


def _row_copy(src_ref, src_row, dst_ref, dst_row, sem):
    return pltpu.make_async_copy(src_ref.at[pl.ds(src_row, 1), :], dst_ref.at[pl.ds(dst_row, 1), :], sem)


ROW_UNROLL = 8


def _dispatch_kernel(dest_ref, eend_ref, h_ref, xs_ref, zbuf, sem, zsem):
    tm = h_ref.shape[0]

    @pl.when(pl.program_id(0) == 0)
    def _():
        zbuf[...] = jnp.zeros_like(zbuf)

        def last_block_copy(e):
            end_b = eend_ref[0, e]
            begin_b = jnp.where(e == 0, 0, eend_ref[0, jnp.maximum(e - 1, 0)])
            row0 = pl.multiple_of((end_b - 1) * EXPERT_BLOCK, EXPERT_BLOCK)
            return end_b > begin_b, pltpu.make_async_copy(zbuf, xs_ref.at[pl.ds(row0, EXPERT_BLOCK), :], zsem)

        def fill(e, carry):
            owns_rows, cp = last_block_copy(e)
            pl.when(owns_rows)(cp.start)
            return carry

        def drain(e, carry):
            owns_rows, cp = last_block_copy(e)
            pl.when(owns_rows)(cp.wait)
            return carry

        lax.fori_loop(0, N_EXPERTS, fill, 0)
        lax.fori_loop(0, N_EXPERTS, drain, 0)

        def spare_block_copy(b):
            row0 = pl.multiple_of(b * EXPERT_BLOCK, EXPERT_BLOCK)
            return pltpu.make_async_copy(zbuf, xs_ref.at[pl.ds(row0, EXPERT_BLOCK), :], zsem)

        n_used = eend_ref[0, N_EXPERTS - 1]
        n_blocks = xs_ref.shape[0] // EXPERT_BLOCK
        lax.fori_loop(n_used, n_blocks, lambda b, carry: (spare_block_copy(b).start(), carry)[1], 0)
        lax.fori_loop(n_used, n_blocks, lambda b, carry: (spare_block_copy(b).wait(), carry)[1], 0)

    def start(g, carry):
        for u in range(ROW_UNROLL):
            r = g * ROW_UNROLL + u
            _row_copy(h_ref, r, xs_ref, dest_ref[0, 0, r], sem).start(priority=u % 2)
            _row_copy(h_ref, r, xs_ref, dest_ref[0, 1, r], sem).start(priority=(u + 1) % 2)
        return carry

    lax.fori_loop(0, tm // ROW_UNROLL, start, 0)
    all_rows = pltpu.make_async_copy(h_ref, xs_ref.at[pl.ds(0, tm), :], sem)
    all_rows.wait()
    all_rows.wait()


def _dispatch(dest, eend, h2, n_rows):
    nt, _, tm = dest.shape
    n, d = h2.shape
    return pl.pallas_call(
        _dispatch_kernel,
        grid=(nt,),
        in_specs=[pl.BlockSpec((1, 2, tm), lambda i: (i, 0, 0), memory_space=pltpu.SMEM),
                  pl.BlockSpec(eend.shape, lambda i: (0, 0), memory_space=pltpu.SMEM),
                  pl.BlockSpec((tm, d), lambda i: (i, 0))],
        out_specs=pl.BlockSpec(memory_space=pl.ANY),
        out_shape=jax.ShapeDtypeStruct((n_rows, d), F32),
        scratch_shapes=[pltpu.VMEM((EXPERT_BLOCK, d), F32),
                        pltpu.SemaphoreType.DMA(()), pltpu.SemaphoreType.DMA(())],
        compiler_params=_params(1),
        name="dispatch",
    )(dest, eend, h2)


def _expert_kernel(bexp_ref, eend_ref, xs_ref, wg_ref, wu_ref, wd_ref, ys_ref, wg_b, wu_b, wd_b):
    i = pl.program_id(0)
    used = i < eend_ref[N_EXPERTS - 1]
    new_expert = jnp.logical_or(i == 0, bexp_ref[i] != bexp_ref[jnp.maximum(i - 1, 0)])

    @pl.when(jnp.logical_and(used, new_expert))
    def _():
        wg_b[...] = wg_ref[0].astype(BF16)
        wu_b[...] = wu_ref[0].astype(BF16)
        wd_b[...] = wd_ref[0].astype(BF16)

    @pl.when(used)
    def _():
        x = xs_ref[...].astype(BF16)
        hg = _dot(x, wg_b[...])
        hu = _dot(x, wu_b[...])
        hid = (hg * _sigmoid(hg) * hu).astype(BF16)
        ys_ref[...] = _dot(hid, wd_b[...])

    @pl.when(jnp.logical_not(used))
    def _():
        ys_ref[...] = jnp.zeros_like(ys_ref)


def _experts(bexp, eend, xs, e_gate, e_up, e_down, n_blocks):
    n_rows, d = xs.shape
    de = e_gate.shape[2]
    blk = EXPERT_BLOCK
    last = lambda i, nu: jnp.minimum(i, nu[N_EXPERTS - 1] - 1)
    rows = lambda i, be, nu: (last(i, nu), 0)
    wsel = lambda i, be, nu: (be[last(i, nu)], 0, 0)
    return pl.pallas_call(
        _expert_kernel,
        grid_spec=pltpu.PrefetchScalarGridSpec(
            num_scalar_prefetch=2,
            grid=(n_blocks,),
            in_specs=[pl.BlockSpec((blk, d), rows),
                      pl.BlockSpec((1, d, de), wsel),
                      pl.BlockSpec((1, d, de), wsel),
                      pl.BlockSpec((1, de, d), wsel)],
            out_specs=pl.BlockSpec((blk, d), lambda i, be, nu: (i, 0)),
            scratch_shapes=[pltpu.VMEM((d, de), BF16), pltpu.VMEM((d, de), BF16),
                            pltpu.VMEM((de, d), BF16)]),
        out_shape=jax.ShapeDtypeStruct((n_rows, d), F32),
        compiler_params=_params(1),
        name="experts",
    )(bexp, eend, xs, e_gate, e_up, e_down)


def _combine_kernel(dest_ref, dnext_ref, ys_ref, wts_ref, x1_ref, g2_ref, gf_ref, scf_ref, shf_ref, o_ref,
                    buf, sems):
    tm = x1_ref.shape[0]
    i = pl.program_id(0)
    cur = lax.rem(i, 2)

    def gather(d_ref, s):
        def start(g, carry):
            for u in range(ROW_UNROLL):
                r = g * ROW_UNROLL + u
                _row_copy(ys_ref, d_ref[0, 0, r], buf.at[s, 0], r, sems.at[s]).start(priority=u % 2)
                _row_copy(ys_ref, d_ref[0, 1, r], buf.at[s, 1], r, sems.at[s]).start(priority=(u + 1) % 2)
            return carry

        lax.fori_loop(0, tm // ROW_UNROLL, start, 0)

    @pl.when(i == 0)
    def _():
        gather(dest_ref, 0)

    @pl.when(i + 1 < pl.num_programs(0))
    def _():
        gather(dnext_ref, 1 - cur)

    all_rows = pltpu.make_async_copy(ys_ref.at[pl.ds(0, tm), :], buf.at[cur, 0], sems.at[cur])
    all_rows.wait()
    all_rows.wait()
    wts = wts_ref[...]
    moe = buf[cur, 0] * wts[:, 0:1] + buf[cur, 1] * wts[:, 1:2]
    x2 = x1_ref[...] + (1.0 + g2_ref[0]) * moe
    ms = jnp.mean(x2 * x2, axis=-1, keepdims=True)
    y = x2 * lax.rsqrt(ms + NORM_EPS) * gf_ref[...]
    o_ref[...] = y * (1.0 + scf_ref[0]) + shf_ref[0]


def _combine(dest, ys, wts_col, x1, mod, modf, gf, seq):
    nt, _, tm = dest.shape
    n, d = x1.shape
    tps = seq // tm
    return pl.pallas_call(
        _combine_kernel,
        grid=(nt,),
        in_specs=[pl.BlockSpec((1, 2, tm), lambda i: (i, 0, 0), memory_space=pltpu.SMEM),
                  pl.BlockSpec((1, 2, tm), lambda i: (jnp.minimum(i + 1, nt - 1), 0, 0),
                               memory_space=pltpu.SMEM),
                  pl.BlockSpec(memory_space=pl.ANY),
                  pl.BlockSpec((tm, 2), lambda i: (i, 0)),
                  pl.BlockSpec((tm, d), lambda i: (i, 0)),
                  pl.BlockSpec((1, 1, d), lambda i: ((i // tps) * 6 + 5, 0, 0)),
                  pl.BlockSpec((1, d), lambda i: (0, 0)),
                  pl.BlockSpec((1, 1, d), lambda i: ((i // tps) * 2 + 1, 0, 0)),
                  pl.BlockSpec((1, 1, d), lambda i: ((i // tps) * 2 + 0, 0, 0))],
        out_specs=pl.BlockSpec((tm, d), lambda i: (i, 0)),
        out_shape=jax.ShapeDtypeStruct((n, d), F32),
        scratch_shapes=[pltpu.VMEM((2, 2, tm, d), F32), pltpu.SemaphoreType.DMA((2,))],
        compiler_params=_params(1),
        name="combine",
    )(dest, dest, ys, wts_col, x1, mod, gf, modf, modf)


def kernel(x, c, ada_w, ada_b, norm1_g, w_in, shift_mu, w0, w_decay_up, a0, w_aaa_up, w_gate_up,
           k_k, k_a, r_k, gn_g, gn_b, kv_norm_g, k_idx_norm_g, w_uk, w_uv, w_out, norm2_g,
           w_group, b_group, w_expert, b_expert, e_gate, e_up, e_down,
           final_ada_w, final_ada_b, final_norm_g):
    bsz, seq, d = x.shape
    n = bsz * seq
    depth = ada_w.shape[0]
    tm = min(TOKEN_TILE, seq)
    x2 = x.reshape(n, d)

    modf = _ada(c, final_ada_w[None], final_ada_b[None], 0).reshape(bsz * 2, 1, d)
    for l in range(depth):
        mod = _ada(c, ada_w, ada_b, l).reshape(bsz * 6, 1, d)

        w1 = w_in[l][:, :RWKV_COLS].astype(BF16)
        w2 = jnp.pad(w_in[l][:, RWKV_COLS:], ((0, 0), (0, DSA_COLS_PAD - DSA_COLS))).astype(BF16)
        p_rwkv, p_dsa = _proj(x2, norm1_g[l].reshape(1, d), mod, w1, w2, seq, min(PROJ_TILE, seq))

        zeros = jnp.zeros((DECAY_LORA, RWKV_WIDTH), F32)
        wda = jnp.concatenate([jnp.concatenate([w_decay_up[l], zeros], axis=1),
                               jnp.concatenate([zeros, w_aaa_up[l]], axis=1)], axis=0).astype(BF16)
        y_rwkv = _rwkv(p_rwkv, bsz, seq, shift_mu[l], w0[l], wda, a0[l], w_gate_up[l].astype(BF16),
                       k_k[l], k_a[l], r_k[l], gn_g[l], gn_b[l])
        y_dsa = _dsa(p_dsa, bsz, seq, kv_norm_g[l], k_idx_norm_g[l], w_uk[l], w_uv[l])

        wr = jnp.zeros((ROUTER_ROWS, d), F32)
        wr = wr.at[0:N_GROUPS].set(w_group[l].T).at[8:8 + N_EXPERTS].set(w_expert[l].T)
        br = jnp.zeros((ROUTER_ROWS, 1), F32)
        br = br.at[0:N_GROUPS, 0].set(b_group[l]).at[8:8 + N_EXPERTS, 0].set(b_expert[l])
        wo = w_out[l].astype(BF16)
        x1, h2, ids, wts = _mix(y_rwkv, y_dsa, x2, wo[:RWKV_WIDTH], wo[RWKV_WIDTH:], mod,
                                norm2_g[l].reshape(1, d), wr, br, seq, tm)

        n_blocks = (n * 2) // EXPERT_BLOCK + N_EXPERTS
        n_blocks_pad = -(-n_blocks // 128) * 128
        dest, bexp, eend = _sort(ids, n_blocks_pad)
        xs = _dispatch(dest, eend, h2, n_blocks * EXPERT_BLOCK)
        ys = _experts(bexp.reshape(-1), eend.reshape(-1), xs, e_gate[l], e_up[l], e_down[l], n_blocks)
        wts_col = jnp.transpose(wts, (0, 2, 1)).reshape(n, 2)
        last = l == depth - 1
        if not last:
            raise NotImplementedError("stacked layers need a residual-only combine")
        x2 = _combine(dest, ys, wts_col, x1, mod, modf, final_norm_g.reshape(1, d), seq)
    return x2.reshape(bsz, seq, d)
```
